```python
import math
import jax, jax.numpy as jnp
from jax import lax
import numpy as np

D_MODEL = 4096
BATCH = 1
SEQ = 16384
DEPTH = 2

N_BRANCHES = 4
HEAD_DIM = 128
BRANCH_HEADS = D_MODEL // (N_BRANCHES * HEAD_DIM)
BRANCH_WIDTH = BRANCH_HEADS * HEAD_DIM
ROT_DIM = HEAD_DIM // 4
ROPE_THETA = 500000.0
RMS_EPS = 1e-6
NEG_INF = -1e30
Q_BLOCK = 128
KEY_GROUPS = 16
GATE_RANK = 256

MLA_Q_LORA = 896
MLA_KV_LORA = 256
MLA_NOPE = 128
MLA_ROPE = 64
MLA_V = HEAD_DIM
MLA_QK = MLA_NOPE + MLA_ROPE

DIL_PATTERNS = ((128, 1), (512, 4), (2048, 16))

DSA_TOPK = 256
IDX_HEADS = 8
IDX_DIM = 64
IDX_ROT = IDX_DIM // 4

MEM_TOKENS = 256
MEM_HEADS = 4
MEM_HEAD_DIM = 128
MEM_WIDTH = MEM_HEADS * MEM_HEAD_DIM

FFN_DENSE = 14336
N_EXPERTS = 8
MOE_TOP_K = 2
FFN_EXPERT = 1792

IN_SPLITS = (MLA_Q_LORA, MLA_KV_LORA, MLA_ROPE,
             BRANCH_WIDTH, BRANCH_WIDTH, BRANCH_WIDTH,
             BRANCH_WIDTH, BRANCH_WIDTH, BRANCH_WIDTH,
             BRANCH_WIDTH, HEAD_DIM, HEAD_DIM,
             IDX_HEADS * IDX_DIM, IDX_DIM, IDX_HEADS)
N_IN = sum(IN_SPLITS)

kernel_name = 'hybrid_gated_mla_sb_dilated_dsa_moe'


def rmsnorm(x, g):
    x32 = x.astype(jnp.float32)
    y = x32 * lax.rsqrt(jnp.mean(x32 * x32, axis=-1, keepdims=True) + RMS_EPS)
    return (y * g.astype(jnp.float32)).astype(x.dtype)


def rope_angles(S, rot_dim):
    inv_freq = ROPE_THETA ** (-jnp.arange(0, rot_dim, 2, dtype=jnp.float32) / rot_dim)
    ang = jnp.arange(S, dtype=jnp.float32)[:, None] * inv_freq[None, :]
    return jnp.cos(ang), jnp.sin(ang)


def rotate_prefix(x, cos, sin):
    half = cos.shape[-1]
    xr = x[..., :2 * half].astype(jnp.float32)
    x1, x2 = xr[..., :half], xr[..., half:]
    c, s = cos[None, :, None, :], sin[None, :, None, :]
    rot = jnp.concatenate([x1 * c - x2 * s, x2 * c + x1 * s], axis=-1).astype(x.dtype)
    return jnp.concatenate([rot, x[..., 2 * half:]], axis=-1)


def to_blocks(a):
    B, S = a.shape[:2]
    return jnp.moveaxis(a.reshape((B, S // Q_BLOCK, Q_BLOCK) + a.shape[2:]), 1, 0)


def from_blocks(a):
    a = jnp.moveaxis(a, 0, 1)
    return a.reshape((a.shape[0], a.shape[1] * a.shape[2]) + a.shape[3:])


def causal_sweep(block_fn, qs, kvs):
    S = qs[0].shape[1]
    nb = S // Q_BLOCK
    n_groups = math.gcd(nb, KEY_GROUPS)
    span = (nb // n_groups) * Q_BLOCK
    starts = jnp.arange(span // Q_BLOCK, dtype=jnp.int32) * Q_BLOCK
    outs = []
    for g in range(n_groups):
        lo, hi = g * span, (g + 1) * span
        kv = tuple(a[:, :hi] for a in kvs)
        xs = (tuple(to_blocks(a[:, lo:hi]) for a in qs), lo + starts)
        outs.append(from_blocks(lax.map(lambda args, kv=kv: block_fn(args[0], args[1], kv), xs)))
    return jnp.concatenate(outs, axis=1)


def mla_block(qargs, t0, kv):
    (qb,) = qargs
    k, v = kv
    L = k.shape[1]
    qpos = t0 + jnp.arange(Q_BLOCK, dtype=jnp.int32)
    kpos = jnp.arange(L, dtype=jnp.int32)
    s = jnp.einsum('bqhd,bkhd->bhqk', qb, k).astype(jnp.float32) * (MLA_QK ** -0.5)
    s = jnp.where(kpos[None, :] <= qpos[:, None], s, NEG_INF)
    p = jax.nn.softmax(s, axis=-1).astype(v.dtype)
    return jnp.einsum('bhqk,bkhd->bqhd', p, v)


def stick_breaking_block(qargs, t0, kv):
    (qb,) = qargs
    k, v = kv
    L = k.shape[1]
    qpos = t0 + jnp.arange(Q_BLOCK, dtype=jnp.int32)
    kpos = jnp.arange(L, dtype=jnp.int32)
    z = jnp.einsum('bqhd,bkhd->bhqk', qb, k).astype(jnp.float32) * (HEAD_DIM ** -0.5)
    past = kpos[None, :] < qpos[:, None]
    log_stay = jnp.where(past, jax.nn.log_sigmoid(-z), 0.0)
    rev = lax.cumsum(log_stay, axis=3, reverse=True)
    a = jnp.where(past, jnp.exp(jnp.minimum(z + rev, 0.0)), 0.0).astype(v.dtype)
    return jnp.einsum('bhqk,bkhd->bqhd', a, v)


def dilated_window_attention(q, k, v):
    B, S, H, d = q.shape
    scale = d ** -0.5
    outs, lses = [], []
    for window, dil in DIL_PATTERNS:
        n = S // dil
        m = window // dil
        blk = math.gcd(n, Q_BLOCK)
        nblk = n // blk
        sub = lambda a: jnp.swapaxes(a.reshape(B, n, dil, H, d), 1, 2)
        pad = ((0, 0), (0, 0), (m, 0), (0, 0), (0, 0))
        kp, vp = jnp.pad(sub(k), pad), jnp.pad(sub(v), pad)
        idx = (jnp.arange(nblk, dtype=jnp.int32) * blk)[:, None] + jnp.arange(blk + m, dtype=jnp.int32)[None, :]
        kw = jnp.take(kp, idx, axis=2)
        vw = jnp.take(vp, idx, axis=2)
        qb = sub(q).reshape(B, dil, nblk, blk, H, d)
        s = jnp.einsum('brnqhd,brnkhd->brnhqk', qb, kw).astype(jnp.float32) * scale
        diff = jnp.arange(blk, dtype=jnp.int32)[:, None] + m - jnp.arange(blk + m, dtype=jnp.int32)[None, :]
        valid = ((diff >= 0) & (diff <= m))[None] & ((idx - m) >= 0)[:, None, :]
        s = jnp.where(valid[None, None, :, None], s, NEG_INF)
        lse = jax.nn.logsumexp(s, axis=-1)
        p = jnp.exp(s - lse[..., None]).astype(v.dtype)
        o = jnp.einsum('brnhqk,brnkhd->brnqhd', p, vw).reshape(B, dil, n, H, d)
        outs.append(jnp.swapaxes(o, 1, 2).reshape(B, S, H, d))
        lse = jnp.swapaxes(lse, 3, 4).reshape(B, dil, n, H)
        lses.append(jnp.swapaxes(lse, 1, 2).reshape(B, S, H))
    alpha = jax.nn.softmax(jnp.stack(lses, axis=0), axis=0)[..., None].astype(v.dtype)
    return jnp.sum(alpha * jnp.stack(outs, axis=0), axis=0)


def gather_rows(a, idx):
    return jax.vmap(lambda aa, ii: aa[ii])(a, idx)


def dsa_block(qargs, t0, kv, top_k):
    qb, qib, wib = qargs
    k, v, ki = kv
    L = k.shape[1]
    qpos = t0 + jnp.arange(Q_BLOCK, dtype=jnp.int32)
    kpos = jnp.arange(L, dtype=jnp.int32)
    rel = jax.nn.relu(jnp.einsum('bqhe,bke->bqhk', qib, ki).astype(jnp.float32) * (IDX_DIM ** -0.5))
    index = jnp.einsum('bqhk,bqh->bqk', rel, wib.astype(jnp.float32))
    index = jnp.where((kpos[None, :] <= qpos[:, None])[None], index, -jnp.inf)
    _, sel = lax.top_k(index, min(top_k, L))
    ks = gather_rows(k, sel)
    vs = gather_rows(v, sel)
    s = jnp.einsum('bqhd,bqkd->bhqk', qb, ks).astype(jnp.float32) * (HEAD_DIM ** -0.5)
    s = jnp.where((sel <= qpos[None, :, None])[:, None], s, NEG_INF)
    p = jax.nn.softmax(s, axis=-1).astype(v.dtype)
    return jnp.einsum('bhqk,bqkd->bqhd', p, vs)


def hybrid_token_mixer(h, w_in, g_cq, g_ckv, w_uq, w_ukv, g_q_mla, g_k_mla, g_q_dil, g_k_dil,
                       g_q_dsa, g_k_dsa, w_gate_a, w_gate_b, b_gate, w_branch, w_out):
    B, S, _ = h.shape
    H = BRANCH_HEADS
    cos_p, sin_p = rope_angles(S, ROT_DIM)
    cos_i, sin_i = rope_angles(S, IDX_ROT)
    cos_m, sin_m = rope_angles(S, MLA_ROPE)
    cuts, acc = [], 0
    for n in IN_SPLITS[:-1]:
        acc += n
        cuts.append(acc)
    (c_q, c_kv, k_pe, q_sb, k_sb, v_sb, q_dl, k_dl, v_dl,
     q_ds, k_ds, v_ds, q_ix, k_ix, w_ix) = jnp.split(h @ w_in, cuts, axis=-1)
    heads = lambda t: t.reshape(B, S, -1, HEAD_DIM)

    q = (rmsnorm(c_q, g_cq) @ w_uq).reshape(B, S, H, MLA_QK)
    kv = (rmsnorm(c_kv, g_ckv) @ w_ukv).reshape(B, S, H, MLA_NOPE + MLA_V)
    k = jnp.concatenate([kv[..., :MLA_NOPE],
                         jnp.broadcast_to(k_pe[:, :, None, :], (B, S, H, MLA_ROPE))], axis=-1)
    q, k = rmsnorm(q, g_q_mla), rmsnorm(k, g_k_mla)
    q = jnp.concatenate([q[..., :MLA_NOPE], rotate_prefix(q[..., MLA_NOPE:], cos_m, sin_m)], axis=-1)
    k = jnp.concatenate([k[..., :MLA_NOPE], rotate_prefix(k[..., MLA_NOPE:], cos_m, sin_m)], axis=-1)
    o_mla = causal_sweep(mla_block, (q,), (k, kv[..., MLA_NOPE:])).reshape(B, S, BRANCH_WIDTH)

    o_sb = causal_sweep(stick_breaking_block, (heads(q_sb),),
                        (heads(k_sb), heads(v_sb))).reshape(B, S, BRANCH_WIDTH)

    qd = rotate_prefix(rmsnorm(heads(q_dl), g_q_dil), cos_p, sin_p)
    kd = rotate_prefix(rmsnorm(heads(k_dl), g_k_dil), cos_p, sin_p)
    o_dil = dilated_window_attention(qd, kd, heads(v_dl)).reshape(B, S, BRANCH_WIDTH)

    qs = rotate_prefix(rmsnorm(heads(q_ds), g_q_dsa), cos_p, sin_p)
    ks = rotate_prefix(rmsnorm(k_ds[:, :, None, :], g_k_dsa), cos_p, sin_p)[:, :, 0]
    qi = rotate_prefix(q_ix.reshape(B, S, IDX_HEADS, IDX_DIM), cos_i, sin_i)
    ki = rotate_prefix(k_ix[:, :, None, :], cos_i, sin_i)[:, :, 0]
    top_k = min(DSA_TOPK, S // 4)
    o_dsa = causal_sweep(lambda qa, t0, kvp: dsa_block(qa, t0, kvp, top_k),
                         (qs, qi, w_ix * IDX_HEADS ** -0.5),
                         (ks, v_ds, ki)).reshape(B, S, BRANCH_WIDTH)

    g_lat = h @ w_gate_a
    merged = jnp.zeros_like(h)
    for i, o in enumerate((o_mla, o_sb, o_dil, o_dsa)):
        gate = jax.nn.sigmoid(g_lat @ w_gate_b[i] + b_gate[i])
        merged = merged + gate * (o @ w_branch[i])
    return merged @ w_out


def memory_cross_attention(h, m, w_xq, w_xk, w_xv, g_q_x, g_k_x, w_xo):
    B, S, _ = h.shape
    M = m.shape[1]
    q = rmsnorm((h @ w_xq).reshape(B, S, MEM_HEADS, MEM_HEAD_DIM), g_q_x)
    k = rmsnorm((m @ w_xk).reshape(B, M, MEM_HEADS, MEM_HEAD_DIM), g_k_x)
    v = (m @ w_xv).reshape(B, M, MEM_HEADS, MEM_HEAD_DIM)
    s = jnp.einsum('bqhd,bmhd->bhqm', q, k).astype(jnp.float32) * MEM_HEAD_DIM ** -0.5
    p = jax.nn.softmax(s, axis=-1).astype(v.dtype)
    return jnp.einsum('bhqm,bmhd->bqhd', p, v).reshape(B, S, MEM_WIDTH) @ w_xo


def swiglu(h, wg, wu, wd):
    return (jax.nn.silu(h @ wg) * (h @ wu)) @ wd


def moe_swiglu(h, w_router, w_g, w_u, w_d):
    logits = (h @ w_router).astype(jnp.float32)
    top_val, top_idx = lax.top_k(logits, MOE_TOP_K)
    top_w = jax.nn.softmax(top_val, axis=-1)
    gates = jnp.einsum('bske,bsk->bse', jax.nn.one_hot(top_idx, N_EXPERTS, dtype=jnp.float32),
                       top_w).astype(h.dtype)
    out = jnp.zeros_like(h)
    for e in range(N_EXPERTS):
        out = out + gates[..., e:e + 1] * swiglu(h, w_g[e], w_u[e], w_d[e])
    return out


def setup_inputs(seed: int = 0) -> dict:
    key = jax.random.key(seed)
    it = iter(jax.random.split(key, 40))
    L, ND, NM = DEPTH, (DEPTH + 1) // 2, DEPTH // 2
    D = D_MODEL

    def dense(shape, fan_in):
        return jax.random.normal(next(it), shape, jnp.float32) * (fan_in ** -0.5)

    def gain(shape):
        return 1.0 + 0.02 * jax.random.normal(next(it), shape, jnp.float32)

    return {
        'x': jax.random.normal(next(it), (BATCH, SEQ, D), jnp.float32),
        'mem': jax.random.normal(next(it), (BATCH, MEM_TOKENS, D), jnp.float32),
        'ln_mix': gain((L, D)),
        'w_in': dense((L, D, N_IN), D),
        'g_cq': gain((L, MLA_Q_LORA)),
        'g_ckv': gain((L, MLA_KV_LORA)),
        'w_uq': dense((L, MLA_Q_LORA, BRANCH_HEADS * MLA_QK), MLA_Q_LORA),
        'w_ukv': dense((L, MLA_KV_LORA, BRANCH_HEADS * (MLA_NOPE + MLA_V)), MLA_KV_LORA),
        'g_q_mla': gain((L, MLA_QK)),
        'g_k_mla': gain((L, MLA_QK)),
        'g_q_dil': gain((L, HEAD_DIM)),
        'g_k_dil': gain((L, HEAD_DIM)),
        'g_q_dsa': gain((L, HEAD_DIM)),
        'g_k_dsa': gain((L, HEAD_DIM)),
        'w_gate_a': dense((L, D, GATE_RANK), D),
        'w_gate_b': dense((L, N_BRANCHES, GATE_RANK, D), GATE_RANK),
        'b_gate': 0.02 * jax.random.normal(next(it), (L, N_BRANCHES, D), jnp.float32),
        'w_branch': dense((L, N_BRANCHES, BRANCH_WIDTH, D), BRANCH_WIDTH),
        'w_out': dense((L, D, D), D),
        'ln_xattn': gain((L, D)),
        'ln_mem': gain((L, D)),
        'w_xq': dense((L, D, MEM_WIDTH), D),
        'w_xk': dense((L, D, MEM_WIDTH), D),
        'w_xv': dense((L, D, MEM_WIDTH), D),
        'g_q_x': gain((L, MEM_HEAD_DIM)),
        'g_k_x': gain((L, MEM_HEAD_DIM)),
        'w_xo': dense((L, MEM_WIDTH, D), MEM_WIDTH),
        'ln_ffn': gain((L, D)),
        'w_ff_gate': dense((ND, D, FFN_DENSE), D),
        'w_ff_up': dense((ND, D, FFN_DENSE), D),
        'w_ff_down': dense((ND, FFN_DENSE, D), FFN_DENSE),
        'w_router': dense((NM, D, N_EXPERTS), D),
        'w_e_gate': dense((NM, N_EXPERTS, D, FFN_EXPERT), D),
        'w_e_up': dense((NM, N_EXPERTS, D, FFN_EXPERT), D),
        'w_e_down': dense((NM, N_EXPERTS, FFN_EXPERT, D), FFN_EXPERT),
    }


def reference(x, mem, ln_mix, w_in, g_cq, g_ckv, w_uq, w_ukv, g_q_mla, g_k_mla, g_q_dil, g_k_dil,
              g_q_dsa, g_k_dsa, w_gate_a, w_gate_b, b_gate, w_branch, w_out, ln_xattn, ln_mem,
              w_xq, w_xk, w_xv, g_q_x, g_k_x, w_xo, ln_ffn, w_ff_gate, w_ff_up, w_ff_down,
              w_router, w_e_gate, w_e_up, w_e_down):
    for i in range(DEPTH):
        h = rmsnorm(x, ln_mix[i])
        x = x + hybrid_token_mixer(h, w_in[i], g_cq[i], g_ckv[i], w_uq[i], w_ukv[i], g_q_mla[i],
                                   g_k_mla[i], g_q_dil[i], g_k_dil[i], g_q_dsa[i], g_k_dsa[i],
                                   w_gate_a[i], w_gate_b[i], b_gate[i], w_branch[i], w_out[i])
        x = x + memory_cross_attention(rmsnorm(x, ln_xattn[i]), rmsnorm(mem, ln_mem[i]), w_xq[i],
                                       w_xk[i], w_xv[i], g_q_x[i], g_k_x[i], w_xo[i])
        h = rmsnorm(x, ln_ffn[i])
        j = i // 2
        if i % 2 == 0:
            x = x + swiglu(h, w_ff_gate[j], w_ff_up[j], w_ff_down[j])
        else:
            x = x + moe_swiglu(h, w_router[j], w_e_gate[j], w_e_up[j], w_e_down[j])
    return x
```

```python
import functools
import math

import numpy as np
import jax
import jax.numpy as jnp
from jax import lax
from jax.experimental import pallas as pl
from jax.experimental.pallas import tpu as pltpu

F32 = jnp.float32
BF16 = jnp.bfloat16
I32 = jnp.int32

N_BRANCHES = 4
HEAD_DIM = 128
N_HEADS = 8
BRANCH_WIDTH = N_HEADS * HEAD_DIM
ROT_DIM = HEAD_DIM // 4
ROPE_THETA = 500000.0
RMS_EPS = 1e-6
NEG = -1e30
GATE_RANK = 256
MLA_Q_LORA = 896
MLA_KV_LORA = 256
MLA_NOPE = 128
MLA_ROPE = 64
MLA_QK = MLA_NOPE + MLA_ROPE
MLA_PAD = 256
DIL_PATTERNS = ((128, 1), (512, 4), (2048, 16))
DSA_TOPK = 256
IDX_HEADS = 8
IDX_DIM = 64
IDX_ROT = IDX_DIM // 4
MEM_HEADS = 4
MEM_HEAD_DIM = 128
MEM_WIDTH = MEM_HEADS * MEM_HEAD_DIM
N_EXPERTS = 8
MOE_TOP_K = 2
INT_MIN = -2 ** 31

V7X_VMEM_LIMIT_BYTES = 56 * 1024 * 1024
LANES = 128

NT_DIMS = (((1,), (1,)), ((), ()))


def _cparams(n_axes):
    return pltpu.CompilerParams(dimension_semantics=("arbitrary",) * n_axes,
                                vmem_limit_bytes=V7X_VMEM_LIMIT_BYTES)


def _pick(n, candidates):
    for c in candidates:
        if n % c == 0:
            return c
    return n


def _rmsnorm_kernel(x_ref, g_ref, o_ref):
    x = x_ref[...].astype(F32)
    ms = jnp.mean(x * x, axis=-1, keepdims=True)
    o_ref[...] = (x * lax.rsqrt(ms + RMS_EPS) * g_ref[...]).astype(o_ref.dtype)


def rmsnorm_rows(x, g, out_dtype=BF16):
    m, d = x.shape
    tm = _pick(m, (512, 256, 128))
    return pl.pallas_call(
        _rmsnorm_kernel, grid=(m // tm,),
        in_specs=[pl.BlockSpec((tm, d), lambda i: (i, 0)), pl.BlockSpec((1, d), lambda i: (0, 0))],
        out_specs=pl.BlockSpec((tm, d), lambda i: (i, 0)),
        out_shape=jax.ShapeDtypeStruct((m, d), out_dtype),
        compiler_params=_cparams(1), name="rmsnorm_rows",
    )(x, g.reshape(1, d).astype(F32))


def _mm_kernel(*refs, nk, has_res):
    a_ref, b_ref = refs[0], refs[1]
    r_ref = refs[2] if has_res else None
    o_ref = refs[2 + has_res]
    part = jnp.dot(a_ref[...], b_ref[...], preferred_element_type=F32)
    if nk == 1:
        if has_res:
            part = r_ref[...] + part
        o_ref[...] = part.astype(o_ref.dtype)
        return
    acc_ref = refs[3 + has_res]
    k = pl.program_id(2)

    @pl.when(k == 0)
    def _():
        acc_ref[...] = part

    @pl.when(k > 0)
    def _():
        acc_ref[...] += part

    @pl.when(k == nk - 1)
    def _():
        res = acc_ref[...]
        if has_res:
            res = r_ref[...] + res
        o_ref[...] = res.astype(o_ref.dtype)


def matmul(a, b, *, out_dtype, res=None, tm=None, tn=None, tk=None):
    m, kdim = a.shape
    n = b.shape[1]
    tm = tm or _pick(m, (1024, 512, 256, 128))
    tn = tn or _pick(n, (512, 256, 128))
    tk = tk or (kdim if kdim <= 4096 else _pick(kdim, (2048, 1024, 512)))
    nk = kdim // tk
    in_specs = [pl.BlockSpec((tm, tk), lambda i, j, k: (i, k)),
                pl.BlockSpec((tk, tn), lambda i, j, k: (k, j))]
    args = [a, b]
    if res is not None:
        in_specs.append(pl.BlockSpec((tm, tn), lambda i, j, k: (i, j)))
        args.append(res)
    scratch = [pltpu.VMEM((tm, tn), F32)] if nk > 1 else []
    return pl.pallas_call(
        functools.partial(_mm_kernel, nk=nk, has_res=res is not None),
        grid=(m // tm, n // tn, nk),
        in_specs=in_specs,
        out_specs=pl.BlockSpec((tm, tn), lambda i, j, k: (i, j)),
        out_shape=jax.ShapeDtypeStruct((m, n), out_dtype),
        scratch_shapes=scratch,
        compiler_params=_cparams(3), name="matmul",
    )(*args)


def _glu_kernel(*refs, gated, tiles_per_expert):
    a_ref, wg_ref, wu_ref = refs[:3]
    o_ref = refs[3 + gated]
    a = a_ref[...]
    g = jnp.dot(a, wg_ref[...], preferred_element_type=F32)
    u = jnp.dot(a, wu_ref[...], preferred_element_type=F32)
    y = g * jax.nn.sigmoid(g) * u
    if gated:
        gates = refs[3][...]
        e = pl.program_id(1) // tiles_per_expert
        lane = lax.broadcasted_iota(I32, gates.shape, 1)
        y = y * jnp.sum(jnp.where(lane == e, gates, 0.0), axis=1, keepdims=True)
    o_ref[...] = y.astype(o_ref.dtype)


def glu_up(a, wg, wu, gates=None):
    m, kdim = a.shape
    n_e, _, f = wg.shape
    tm = _pick(m, (1024, 512, 256, 128))
    tn = _pick(f, (512, 256, 128))
    tpe = f // tn
    w_spec = pl.BlockSpec((None, kdim, tn), lambda i, j: (j // tpe, 0, j % tpe))
    in_specs = [pl.BlockSpec((tm, kdim), lambda i, j: (i, 0)), w_spec, w_spec]
    args = [a, wg, wu]
    if gates is not None:
        in_specs.append(pl.BlockSpec((tm, LANES), lambda i, j: (i, 0)))
        args.append(gates)
    return pl.pallas_call(
        functools.partial(_glu_kernel, gated=gates is not None, tiles_per_expert=tpe),
        grid=(m // tm, n_e * tpe),
        in_specs=in_specs,
        out_specs=pl.BlockSpec((tm, tn), lambda i, j: (i, j)),
        out_shape=jax.ShapeDtypeStruct((m, n_e * f), BF16),
        compiler_params=_cparams(2), name="glu_up",
    )(*args)


def _router_kernel(x_ref, g_ref, w_ref, o_ref):
    x = x_ref[...]
    ms = jnp.mean(x * x, axis=-1, keepdims=True)
    h = x * lax.rsqrt(ms + RMS_EPS) * g_ref[...]
    logits = jnp.dot(h, w_ref[...], preferred_element_type=F32, precision=lax.Precision.HIGHEST)
    lane = lax.broadcasted_iota(I32, logits.shape, 1).astype(F32)
    logits = jnp.where(lane < N_EXPERTS, logits, -jnp.inf)
    m1 = jnp.max(logits, axis=1, keepdims=True)
    i1 = jnp.min(jnp.where(logits == m1, lane, float(LANES)), axis=1, keepdims=True)
    rest = jnp.where(lane == i1, -jnp.inf, logits)
    m2 = jnp.max(rest, axis=1, keepdims=True)
    i2 = jnp.min(jnp.where(rest == m2, lane, float(LANES)), axis=1, keepdims=True)
    e2 = jnp.exp(m2 - m1)
    w1 = 1.0 / (1.0 + e2)
    w2 = e2 / (1.0 + e2)
    o_ref[...] = jnp.where(lane == i1, w1, 0.0) + jnp.where(lane == i2, w2, 0.0)


def router_gates(x, g, w_router):
    m, d = x.shape
    tm = _pick(m, (256, 128))
    w = jnp.zeros((d, LANES), F32).at[:, :N_EXPERTS].set(w_router.astype(F32))
    return pl.pallas_call(
        _router_kernel, grid=(m // tm,),
        in_specs=[pl.BlockSpec((tm, d), lambda i: (i, 0)), pl.BlockSpec((1, d), lambda i: (0, 0)),
                  pl.BlockSpec((d, LANES), lambda i: (0, 0))],
        out_specs=pl.BlockSpec((tm, LANES), lambda i: (i, 0)),
        out_shape=jax.ShapeDtypeStruct((m, LANES), F32),
        compiler_params=_cparams(1), name="router_gates",
    )(x, g.reshape(1, d).astype(F32), w)


def _rope(x, tab_ref, half):
    w = x.shape[-1]
    return (x * tab_ref[0] + pltpu.roll(x, w - half, 1) * tab_ref[1]
            + pltpu.roll(x, half, 1) * tab_ref[2])


def _rope_tables(seq, rot_dim, period):
    half = rot_dim // 2
    inv_freq = ROPE_THETA ** (-jnp.arange(0, rot_dim, 2, dtype=F32) / rot_dim)
    ang = jnp.arange(seq, dtype=F32)[:, None] * inv_freq[None, :]
    cos, sin = jnp.cos(ang), jnp.sin(ang)
    ones = jnp.ones((seq, period - rot_dim), F32)
    zeros = jnp.zeros((seq, period - rot_dim), F32)
    zh = jnp.zeros((seq, half), F32)
    c = jnp.concatenate([cos, cos, ones], axis=1)
    sa = jnp.concatenate([-sin, zh, zeros], axis=1)
    sb = jnp.concatenate([zh, sin, zeros], axis=1)
    rep = LANES // period
    return jnp.stack([jnp.tile(c, (1, rep)), jnp.tile(sa, (1, rep)), jnp.tile(sb, (1, rep))], axis=0)


def _tab_spec(tm):
    return pl.BlockSpec((3, tm, LANES), lambda i: (0, i, 0))


def _row_spec(tm, w):
    return pl.BlockSpec((tm, w), lambda i: (i, 0))


def _const_spec(w):
    return pl.BlockSpec((1, w), lambda i: (0, 0))


def _prep_latent_kernel(p_ref, gq_ref, gkv_ref, cq_ref, ckv_ref, kpe_ref):
    cq = p_ref[:, :MLA_Q_LORA]
    ms = jnp.mean(cq * cq, axis=-1, keepdims=True)
    cq_ref[...] = (cq * lax.rsqrt(ms + RMS_EPS) * gq_ref[...]).astype(cq_ref.dtype)
    ckv = p_ref[:, MLA_Q_LORA:MLA_Q_LORA + MLA_KV_LORA]
    ms = jnp.mean(ckv * ckv, axis=-1, keepdims=True)
    ckv_ref[...] = (ckv * lax.rsqrt(ms + RMS_EPS) * gkv_ref[...]).astype(ckv_ref.dtype)
    kpe_ref[...] = p_ref[:, MLA_Q_LORA + MLA_KV_LORA:]


def prep_latent(p, g_cq, g_ckv):
    s = p.shape[0]
    tm = _pick(s, (512, 256, 128))
    return pl.pallas_call(
        _prep_latent_kernel, grid=(s // tm,),
        in_specs=[_row_spec(tm, p.shape[1]), _const_spec(MLA_Q_LORA), _const_spec(MLA_KV_LORA)],
        out_specs=[_row_spec(tm, MLA_Q_LORA), _row_spec(tm, MLA_KV_LORA), _row_spec(tm, LANES)],
        out_shape=[jax.ShapeDtypeStruct((s, MLA_Q_LORA), BF16),
                   jax.ShapeDtypeStruct((s, MLA_KV_LORA), BF16),
                   jax.ShapeDtypeStruct((s, LANES), F32)],
        compiler_params=_cparams(1), name="prep_latent",
    )(p, g_cq.reshape(1, -1), g_ckv.reshape(1, -1))


def _prep_mla_kernel(qup_ref, kvup_ref, kpe_ref, gq_ref, gk_ref, tab_ref, q_ref, k_ref, v_ref):
    half = MLA_ROPE // 2
    gqn, gqr = gq_ref[:, :LANES], gq_ref[:, LANES:]
    gkn, gkr = gk_ref[:, :LANES], gk_ref[:, LANES:]
    kpe = kpe_ref[...]
    kpe_ss = jnp.sum(kpe * kpe, axis=-1, keepdims=True)
    kr_base = _rope(kpe * gkr, tab_ref, half)
    scale = MLA_QK ** -0.5
    for h in range(N_HEADS):
        qn = qup_ref[:, h * MLA_PAD:h * MLA_PAD + LANES]
        qr = qup_ref[:, h * MLA_PAD + LANES:(h + 1) * MLA_PAD]
        ss = jnp.sum(qn * qn, axis=-1, keepdims=True) + jnp.sum(qr * qr, axis=-1, keepdims=True)
        r = lax.rsqrt(ss * (1.0 / MLA_QK) + RMS_EPS) * scale
        q_ref[:, h * MLA_PAD:h * MLA_PAD + LANES] = (qn * r * gqn).astype(q_ref.dtype)
        q_ref[:, h * MLA_PAD + LANES:(h + 1) * MLA_PAD] = _rope(qr * r * gqr, tab_ref, half).astype(q_ref.dtype)
        kn = kvup_ref[:, h * LANES:(h + 1) * LANES]
        ss = jnp.sum(kn * kn, axis=-1, keepdims=True) + kpe_ss
        r = lax.rsqrt(ss * (1.0 / MLA_QK) + RMS_EPS)
        k_ref[:, h * MLA_PAD:h * MLA_PAD + LANES] = (kn * r * gkn).astype(k_ref.dtype)
        k_ref[:, h * MLA_PAD + LANES:(h + 1) * MLA_PAD] = (kr_base * r).astype(k_ref.dtype)
    v_ref[...] = kvup_ref[:, BRANCH_WIDTH:].astype(v_ref.dtype)


def prep_mla(qup, kvup, kpe, g_q, g_k, tab):
    s = qup.shape[0]
    tm = _pick(s, (256, 128))
    pad = lambda g: jnp.zeros((1, MLA_PAD), F32).at[0, :MLA_QK].set(g)
    wq = N_HEADS * MLA_PAD
    return pl.pallas_call(
        _prep_mla_kernel, grid=(s // tm,),
        in_specs=[_row_spec(tm, wq), _row_spec(tm, 2 * BRANCH_WIDTH), _row_spec(tm, LANES),
                  _const_spec(MLA_PAD), _const_spec(MLA_PAD), _tab_spec(tm)],
        out_specs=[_row_spec(tm, wq), _row_spec(tm, wq), _row_spec(tm, BRANCH_WIDTH)],
        out_shape=[jax.ShapeDtypeStruct((s, wq), BF16), jax.ShapeDtypeStruct((s, wq), BF16),
                   jax.ShapeDtypeStruct((s, BRANCH_WIDTH), BF16)],
        compiler_params=_cparams(1), name="prep_mla",
    )(qup, kvup, kpe, pad(g_q), pad(g_k), tab)


def _head_norm_rope(x, g, tab_ref, scale):
    ms = jnp.mean(x * x, axis=-1, keepdims=True)
    y = x * (lax.rsqrt(ms + RMS_EPS) * scale) * g
    return _rope(y, tab_ref, ROT_DIM // 2)


def _prep_dil_kernel(p_ref, gq_ref, gk_ref, tab_ref, q_ref, k_ref, v_ref):
    gq, gk = gq_ref[...], gk_ref[...]
    for h in range(N_HEADS):
        sl = slice(h * LANES, (h + 1) * LANES)
        q_ref[:, sl] = _head_norm_rope(p_ref[:, sl], gq, tab_ref, HEAD_DIM ** -0.5).astype(q_ref.dtype)
        ksl = slice(BRANCH_WIDTH + h * LANES, BRANCH_WIDTH + (h + 1) * LANES)
        k_ref[:, sl] = _head_norm_rope(p_ref[:, ksl], gk, tab_ref, 1.0).astype(k_ref.dtype)
    v_ref[...] = p_ref[:, 2 * BRANCH_WIDTH:].astype(v_ref.dtype)


def prep_dil(p, g_q, g_k, tab):
    s = p.shape[0]
    tm = _pick(s, (256, 128))
    o = jax.ShapeDtypeStruct((s, BRANCH_WIDTH), BF16)
    return pl.pallas_call(
        _prep_dil_kernel, grid=(s // tm,),
        in_specs=[_row_spec(tm, 3 * BRANCH_WIDTH), _const_spec(LANES), _const_spec(LANES), _tab_spec(tm)],
        out_specs=[_row_spec(tm, BRANCH_WIDTH)] * 3, out_shape=[o, o, o],
        compiler_params=_cparams(1), name="prep_dil",
    )(p, g_q.reshape(1, -1), g_k.reshape(1, -1), tab)


def _prep_dsa_kernel(p_ref, gq_ref, gk_ref, tab_ref, q_ref, k_ref, v_ref):
    gq, gk = gq_ref[...], gk_ref[...]
    for h in range(N_HEADS):
        sl = slice(h * LANES, (h + 1) * LANES)
        q_ref[h] = _head_norm_rope(p_ref[:, sl], gq, tab_ref, HEAD_DIM ** -0.5).astype(q_ref.dtype)
    k_ref[...] = _head_norm_rope(p_ref[:, BRANCH_WIDTH:BRANCH_WIDTH + LANES], gk, tab_ref, 1.0).astype(k_ref.dtype)
    v_ref[...] = p_ref[:, BRANCH_WIDTH + LANES:].astype(v_ref.dtype)


def prep_dsa(p, g_q, g_k, tab):
    s = p.shape[0]
    tm = _pick(s, (256, 128))
    o = jax.ShapeDtypeStruct((s, LANES), BF16)
    return pl.pallas_call(
        _prep_dsa_kernel, grid=(s // tm,),
        in_specs=[_row_spec(tm, BRANCH_WIDTH + 2 * LANES), _const_spec(LANES), _const_spec(LANES), _tab_spec(tm)],
        out_specs=[pl.BlockSpec((N_HEADS, tm, LANES), lambda i: (0, i, 0)), _row_spec(tm, LANES), _row_spec(tm, LANES)],
        out_shape=[jax.ShapeDtypeStruct((N_HEADS, s, LANES), BF16), o, o],
        compiler_params=_cparams(1), name="prep_dsa",
    )(p, g_q.reshape(1, -1), g_k.reshape(1, -1), tab)


def _prep_idx_kernel(p_ref, tab_ref, qi_ref, ki_ref, wi_ref):
    half = IDX_ROT // 2
    lane = lax.broadcasted_iota(I32, (p_ref.shape[0], LANES), 1)
    first = lane < IDX_DIM
    zero = jnp.zeros((p_ref.shape[0], LANES), F32)
    for b in range(IDX_HEADS // 2):
        x = _rope(p_ref[:, b * LANES:(b + 1) * LANES], tab_ref, half) * (IDX_DIM ** -0.5)
        hi = x.astype(BF16).astype(F32)
        lo = x - hi
        rhi = pltpu.roll(hi, IDX_DIM, 1)
        qi_ref[2 * b, :, :LANES] = jnp.where(first, hi, pltpu.roll(lo, IDX_DIM, 1)).astype(BF16)
        qi_ref[2 * b, :, LANES:] = jnp.where(first, hi, zero).astype(BF16)
        qi_ref[2 * b + 1, :, :LANES] = jnp.where(first, rhi, lo).astype(BF16)
        qi_ref[2 * b + 1, :, LANES:] = jnp.where(first, rhi, zero).astype(BF16)
    kx = _rope(p_ref[:, IDX_HEADS * IDX_DIM:IDX_HEADS * IDX_DIM + LANES], tab_ref, half)
    hi = kx.astype(BF16).astype(F32)
    lo = kx - hi
    ki_ref[:, :LANES] = jnp.where(first, hi, pltpu.roll(hi, IDX_DIM, 1)).astype(BF16)
    ki_ref[:, LANES:] = jnp.where(first, lo, zero).astype(BF16)
    wi_ref[...] = p_ref[:, IDX_HEADS * IDX_DIM + LANES:] * (IDX_HEADS ** -0.5)


def prep_idx(p, tab):
    s = p.shape[0]
    tm = _pick(s, (256, 128))
    return pl.pallas_call(
        _prep_idx_kernel, grid=(s // tm,),
        in_specs=[_row_spec(tm, IDX_HEADS * IDX_DIM + 2 * LANES), _tab_spec(tm)],
        out_specs=[pl.BlockSpec((IDX_HEADS, tm, 2 * LANES), lambda i: (0, i, 0)),
                   _row_spec(tm, 2 * LANES), _row_spec(tm, LANES)],
        out_shape=[jax.ShapeDtypeStruct((IDX_HEADS, s, 2 * LANES), BF16),
                   jax.ShapeDtypeStruct((s, 2 * LANES), BF16),
                   jax.ShapeDtypeStruct((s, LANES), F32)],
        compiler_params=_cparams(1), name="prep_idx",
    )(p, tab)


def _softmax_step(s, v, m_sc, l_sc, acc_sc):
    m_prev = m_sc[...]
    m_new = jnp.maximum(m_prev, jnp.max(s, axis=1, keepdims=True))
    alpha = jnp.exp(m_prev - m_new)
    p = jnp.exp(s - m_new)
    l_sc[...] = alpha * l_sc[...] + jnp.sum(p, axis=1, keepdims=True)
    acc_sc[...] = alpha * acc_sc[...] + jnp.dot(p.astype(v.dtype), v, preferred_element_type=F32)
    m_sc[...] = m_new


def _softmax_init(m_sc, l_sc, acc_sc):
    m_sc[...] = jnp.full(m_sc.shape, NEG, F32)
    l_sc[...] = jnp.zeros(l_sc.shape, F32)
    acc_sc[...] = jnp.zeros(acc_sc.shape, F32)


def _mla_kernel(q_ref, k_ref, v_ref, o_ref, m_sc, l_sc, acc_sc):
    i = pl.program_id(1)
    t = q_ref.shape[0]
    q = q_ref[...]
    _softmax_init(m_sc, l_sc, acc_sc)

    def full_block(j, c):
        s = lax.dot_general(q, k_ref[j], NT_DIMS, preferred_element_type=F32)
        _softmax_step(s, v_ref[j], m_sc, l_sc, acc_sc)
        return c

    lax.fori_loop(0, i, full_block, 0)
    s = lax.dot_general(q, k_ref[i], NT_DIMS, preferred_element_type=F32)
    row = lax.broadcasted_iota(I32, (t, t), 0)
    col = lax.broadcasted_iota(I32, (t, t), 1)
    _softmax_step(jnp.where(col <= row, s, NEG), v_ref[i], m_sc, l_sc, acc_sc)
    o_ref[...] = (acc_sc[...] / l_sc[...]).astype(o_ref.dtype)


def _attn_scratch(t):
    return [pltpu.VMEM((t, 1), F32), pltpu.VMEM((t, 1), F32), pltpu.VMEM((t, HEAD_DIM), F32)]


def _resident_spec(nb, t, w, col_of_head):
    return pl.BlockSpec((nb, t, w), lambda h, i: (0, 0, col_of_head(h)))


def mla_attention(q, k, v):
    s = q.shape[0]
    t = _pick(s, (512, 256, 128))
    nb = s // t
    return pl.pallas_call(
        _mla_kernel, grid=(N_HEADS, nb),
        in_specs=[pl.BlockSpec((t, MLA_PAD), lambda h, i: (i, h)),
                  _resident_spec(nb, t, MLA_PAD, lambda h: h),
                  _resident_spec(nb, t, HEAD_DIM, lambda h: h)],
        out_specs=pl.BlockSpec((t, HEAD_DIM), lambda h, i: (i, h)),
        out_shape=jax.ShapeDtypeStruct((s, BRANCH_WIDTH), BF16),
        scratch_shapes=_attn_scratch(t),
        compiler_params=_cparams(2), name="mla_attention",
    )(q, k.reshape(nb, t, -1), v.reshape(nb, t, -1))


SB_SUB = 128


def _sb_kernel(q_ref, k_ref, v_ref, u_ref, o_ref, carry_sc, acc_sc):
    i = pl.program_id(1)
    t = q_ref.shape[0]
    q = q_ref[...]
    u2 = u_ref[...]
    scale = HEAD_DIM ** -0.5
    carry_sc[...] = jnp.zeros(carry_sc.shape, F32)
    acc_sc[...] = jnp.zeros(acc_sc.shape, F32)
    row = lax.broadcasted_iota(I32, (t, SB_SUB), 0)
    col = lax.broadcasted_iota(I32, (t, SB_SUB), 1)

    def block(j, diag):
        z = lax.dot_general(q, k_ref[j], NT_DIMS, preferred_element_type=F32) * scale
        carry = carry_sc[...]
        parts = [None] * (t // SB_SUB)
        for c in reversed(range(t // SB_SUB)):
            zc = z[:, c * SB_SUB:(c + 1) * SB_SUB]
            ls = jnp.minimum(-zc, 0.0) - jnp.log(1.0 + jnp.exp(-jnp.abs(zc)))
            if diag:
                past = col + c * SB_SUB < row
                ls = jnp.where(past, ls, 0.0)
            hi = ls.astype(BF16)
            lo = (ls - hi.astype(F32)).astype(BF16)
            rev = jnp.dot(jnp.concatenate([hi, lo], axis=1), u2, preferred_element_type=F32) + carry
            a = jnp.exp(jnp.minimum(zc + rev, 0.0))
            if diag:
                a = jnp.where(past, a, 0.0)
            parts[c] = a.astype(BF16)
            carry = rev[:, 0:1]
        carry_sc[...] = carry
        acc_sc[...] += jnp.dot(jnp.concatenate(parts, axis=1), v_ref[j], preferred_element_type=F32)

    block(i, True)

    def back(jj, c):
        block(i - 1 - jj, False)
        return c

    lax.fori_loop(0, i, back, 0)
    o_ref[...] = acc_sc[...].astype(o_ref.dtype)


def sb_attention(qkv):
    s = qkv.shape[0]
    t = _pick(s, (512, 256, 128))
    nb = s // t
    tri = (np.arange(SB_SUB)[:, None] >= np.arange(SB_SUB)[None, :]).astype(np.float32)
    u2 = jnp.asarray(np.concatenate([tri, tri], axis=0), dtype=BF16)
    kv3 = qkv.reshape(nb, t, -1)
    return pl.pallas_call(
        _sb_kernel, grid=(N_HEADS, nb),
        in_specs=[pl.BlockSpec((t, HEAD_DIM), lambda h, i: (i, h)),
                  _resident_spec(nb, t, HEAD_DIM, lambda h: N_HEADS + h),
                  _resident_spec(nb, t, HEAD_DIM, lambda h: 2 * N_HEADS + h),
                  pl.BlockSpec((2 * SB_SUB, SB_SUB), lambda h, i: (0, 0))],
        out_specs=pl.BlockSpec((t, HEAD_DIM), lambda h, i: (i, h)),
        out_shape=jax.ShapeDtypeStruct((s, BRANCH_WIDTH), BF16),
        scratch_shapes=[pltpu.VMEM((t, 1), F32), pltpu.VMEM((t, HEAD_DIM), F32)],
        compiler_params=_cparams(2), name="sb_attention",
    )(qkv, kv3, kv3, u2)


def _dil_log_weights(t):
    span = max(w for w, _ in DIL_PATTERNS)
    nback = -(-span // t)
    d = np.arange(nback + 1)[:, None, None] * t + np.arange(t)[None, :, None] - np.arange(t)[None, None, :]
    mult = np.zeros(d.shape, np.float64)
    for window, dil in DIL_PATTERNS:
        mult += ((d >= 0) & (d <= window) & (d % dil == 0))
    return np.where(mult > 0, np.log(np.maximum(mult, 1.0)), NEG).astype(np.float32)


def _dil_kernel(q_ref, k_ref, v_ref, w_ref, o_ref, m_sc, l_sc, acc_sc):
    i = pl.program_id(1)
    nback = w_ref.shape[0] - 1
    q = q_ref[...]
    _softmax_init(m_sc, l_sc, acc_sc)

    def body(d, c):
        j = i - d
        s = lax.dot_general(q, k_ref[j], NT_DIMS, preferred_element_type=F32) + w_ref[d]
        _softmax_step(s, v_ref[j], m_sc, l_sc, acc_sc)
        return c

    lax.fori_loop(0, jnp.minimum(i, nback) + 1, body, 0)
    o_ref[...] = (acc_sc[...] / l_sc[...]).astype(o_ref.dtype)


def dil_attention(q, k, v):
    s = q.shape[0]
    t = _pick(s, (256, 128))
    nb = s // t
    logw = jnp.asarray(_dil_log_weights(t))
    return pl.pallas_call(
        _dil_kernel, grid=(N_HEADS, nb),
        in_specs=[pl.BlockSpec((t, HEAD_DIM), lambda h, i: (i, h)),
                  _resident_spec(nb, t, HEAD_DIM, lambda h: h),
                  _resident_spec(nb, t, HEAD_DIM, lambda h: h),
                  pl.BlockSpec(logw.shape, lambda h, i: (0, 0, 0))],
        out_specs=pl.BlockSpec((t, HEAD_DIM), lambda h, i: (i, h)),
        out_shape=jax.ShapeDtypeStruct((s, BRANCH_WIDTH), BF16),
        scratch_shapes=_attn_scratch(t),
        compiler_params=_cparams(2), name="dil_attention",
    )(q, k.reshape(nb, t, -1), v.reshape(nb, t, -1), logw)


DSA_TQ = 128
DSA_TK = 512


def _dsa_kernel(qs_ref, qi_ref, wi_ref, ki_ref, k_ref, v_ref, o_ref,
                key_sc, q1_sc, m_sc, l_sc, acc_sc, *, top_k, pos_bits):
    i = pl.program_id(0)
    tq, tk = DSA_TQ, DSA_TK
    nkb = (i * tq + tq + tk - 1) // tk
    qi = qi_ref[...].reshape(IDX_HEADS * tq, 2 * LANES)
    wi = wi_ref[...]
    row = lax.broadcasted_iota(I32, (tq, tk), 0) + i * tq
    col = lax.broadcasted_iota(I32, (tq, tk), 1)

    def idx_body(j, c):
        a = lax.dot_general(qi, ki_ref[j], NT_DIMS, preferred_element_type=F32)
        idx = jnp.zeros((tq, tk), F32)
        for h in range(IDX_HEADS):
            idx = idx + jnp.maximum(a[h * tq:(h + 1) * tq], 0.0) * wi[:, h:h + 1]
        bits = pltpu.bitcast(idx + 0.0, I32)
        key = bits ^ ((bits >> 31) & 0x7FFFFFFF)
        key_sc[j] = jnp.where(col + j * tk <= row, key, INT_MIN)
        return c

    lax.fori_loop(0, nkb, idx_body, 0)

    def count(hits):
        def body(j, acc):
            hit = hits(key_sc[j], j)
            for c in range(tk // LANES):
                acc = acc + hit[:, c * LANES:(c + 1) * LANES]
            return acc
        acc = lax.fori_loop(0, nkb, body, jnp.zeros((tq, LANES), F32))
        return jnp.sum(acc, axis=1, keepdims=True)

    def bit_body(b, carry):
        thr, cge = carry
        cand = thr + jnp.left_shift(jnp.int32(1), 31 - b)
        c = count(lambda kb, j: jnp.where(kb >= cand, 1.0, 0.0))
        ok = c >= top_k
        return jnp.where(ok, cand, thr), jnp.where(ok, c, cge)

    thr0 = jnp.full((tq, 1), INT_MIN, I32)
    cge0 = jnp.zeros((tq, 1), F32) + (nkb * tk).astype(F32)
    thr, cge = lax.fori_loop(0, 32, bit_body, (thr0, cge0))

    q1_sc[...] = jnp.full((tq, 1), 2 ** 30, I32)

    @pl.when(jnp.max(cge) > top_k)
    def _():
        need = top_k - count(lambda kb, j: jnp.where(kb > thr, 1.0, 0.0))

        def pos_body(b, qpos):
            cand = qpos + jnp.left_shift(jnp.int32(1), pos_bits - 1 - b)
            g = count(lambda kb, j: jnp.where(kb == thr, jnp.where(col + j * tk < cand, 1.0, 0.0), 0.0))
            return jnp.where(g < need, cand, qpos)

        q1_sc[...] = lax.fori_loop(0, pos_bits, pos_body, jnp.zeros((tq, 1), I32)) + 1

    q1 = q1_sc[...]
    qs = qs_ref[...].reshape(N_HEADS * tq, HEAD_DIM)
    _softmax_init(m_sc, l_sc, acc_sc)

    def att_body(j, c):
        kb = key_sc[j]
        pos = col + j * tk
        bias = jnp.where(kb > thr, 0.0, jnp.where(kb == thr, jnp.where(pos < q1, 0.0, NEG), NEG))
        bias = jnp.where(pos <= row, bias, NEG)
        s = lax.dot_general(qs, k_ref[j], NT_DIMS, preferred_element_type=F32)
        s = (s.reshape(N_HEADS, tq, tk) + bias[None]).reshape(N_HEADS * tq, tk)
        _softmax_step(s, v_ref[j], m_sc, l_sc, acc_sc)
        return c

    lax.fori_loop(0, nkb, att_body, 0)
    o = acc_sc[...] / l_sc[...]
    for h in range(N_HEADS):
        o_ref[:, h * HEAD_DIM:(h + 1) * HEAD_DIM] = o[h * tq:(h + 1) * tq].astype(o_ref.dtype)


def dsa_attention(qs, qi, wi, ki, k, v):
    s = k.shape[0]
    tq, tk = DSA_TQ, DSA_TK
    nkb = s // tk
    top_k = min(DSA_TOPK, s // 4)
    full3 = lambda w: pl.BlockSpec((nkb, tk, w), lambda i: (0, 0, 0))
    return pl.pallas_call(
        functools.partial(_dsa_kernel, top_k=top_k, pos_bits=(s - 1).bit_length()),
        grid=(s // tq,),
        in_specs=[pl.BlockSpec((N_HEADS, tq, HEAD_DIM), lambda i: (0, i, 0)),
                  pl.BlockSpec((IDX_HEADS, tq, 2 * LANES), lambda i: (0, i, 0)),
                  pl.BlockSpec((tq, LANES), lambda i: (i, 0)),
                  full3(2 * LANES), full3(HEAD_DIM), full3(HEAD_DIM)],
        out_specs=pl.BlockSpec((tq, BRANCH_WIDTH), lambda i: (i, 0)),
        out_shape=jax.ShapeDtypeStruct((s, BRANCH_WIDTH), BF16),
        scratch_shapes=[pltpu.VMEM((nkb, tq, tk), I32), pltpu.VMEM((tq, 1), I32),
                        pltpu.VMEM((N_HEADS * tq, 1), F32), pltpu.VMEM((N_HEADS * tq, 1), F32),
                        pltpu.VMEM((N_HEADS * tq, HEAD_DIM), F32)],
        compiler_params=_cparams(1), name="dsa_attention",
    )(qs, qi, wi, ki.reshape(nkb, tk, -1), k.reshape(nkb, tk, -1), v.reshape(nkb, tk, -1))


def _merge_kernel(g_ref, o0_ref, o1_ref, o2_ref, o3_ref, wgb_ref, b_ref, wbr_ref, out_ref):
    g = g_ref[...]
    acc = None
    for n, o_ref in enumerate((o0_ref, o1_ref, o2_ref, o3_ref)):
        gate = jax.nn.sigmoid(jnp.dot(g, wgb_ref[n], preferred_element_type=F32) + b_ref[n])
        val = gate * jnp.dot(o_ref[...], wbr_ref[n], preferred_element_type=F32)
        acc = val if acc is None else acc + val
    out_ref[...] = acc.astype(out_ref.dtype)


def gated_merge(g_lat, outs, w_gate_b, b_gate, w_branch):
    s = g_lat.shape[0]
    d = w_gate_b.shape[-1]
    tm = _pick(s, (1024, 512, 256, 128))
    tn = _pick(d, (512, 256, 128))
    o_spec = pl.BlockSpec((tm, BRANCH_WIDTH), lambda i, j: (i, 0))
    return pl.pallas_call(
        _merge_kernel, grid=(s // tm, d // tn),
        in_specs=[pl.BlockSpec((tm, GATE_RANK), lambda i, j: (i, 0)), o_spec, o_spec, o_spec, o_spec,
                  pl.BlockSpec((N_BRANCHES, GATE_RANK, tn), lambda i, j: (0, 0, j)),
                  pl.BlockSpec((N_BRANCHES, 1, tn), lambda i, j: (0, 0, j)),
                  pl.BlockSpec((N_BRANCHES, BRANCH_WIDTH, tn), lambda i, j: (0, 0, j))],
        out_specs=pl.BlockSpec((tm, tn), lambda i, j: (i, j)),
        out_shape=jax.ShapeDtypeStruct((s, d), BF16),
        compiler_params=_cparams(2), name="gated_merge",
    )(g_lat, *outs, w_gate_b, b_gate.reshape(N_BRANCHES, 1, d), w_branch)


def _cross_kernel(x_ref, g_ref, wq_ref, gq_ref, kraw_ref, gk_ref, v_ref, wo_ref, o_ref):
    x = x_ref[...]
    ms = jnp.mean(x * x, axis=-1, keepdims=True)
    h = (x * lax.rsqrt(ms + RMS_EPS) * g_ref[...]).astype(BF16)
    q = jnp.dot(h, wq_ref[...], preferred_element_type=F32)
    outs = []
    for hd in range(MEM_HEADS):
        sl = slice(hd * MEM_HEAD_DIM, (hd + 1) * MEM_HEAD_DIM)
        qh = q[:, sl]
        qh = qh * (lax.rsqrt(jnp.mean(qh * qh, axis=-1, keepdims=True) + RMS_EPS) * MEM_HEAD_DIM ** -0.5) * gq_ref[...]
        kh = kraw_ref[:, sl]
        kh = kh * lax.rsqrt(jnp.mean(kh * kh, axis=-1, keepdims=True) + RMS_EPS) * gk_ref[...]
        s = lax.dot_general(qh.astype(BF16), kh.astype(BF16), NT_DIMS, preferred_element_type=F32)
        p = jnp.exp(s - jnp.max(s, axis=1, keepdims=True))
        p = p / jnp.sum(p, axis=1, keepdims=True)
        outs.append(jnp.dot(p.astype(BF16), v_ref[:, sl], preferred_element_type=F32).astype(BF16))
    o = jnp.concatenate(outs, axis=1)
    o_ref[...] = x + jnp.dot(o, wo_ref[...], preferred_element_type=F32)


def cross_attention(x, g, w_xq, g_q, k_raw, g_k, v, w_xo):
    s, d = x.shape
    m = k_raw.shape[0]
    tm = _pick(s, (256, 128))
    c2 = lambda shape: pl.BlockSpec(shape, lambda i: (0, 0))
    return pl.pallas_call(
        _cross_kernel, grid=(s // tm,),
        in_specs=[_row_spec(tm, d), c2((1, d)), c2((d, MEM_WIDTH)), c2((1, MEM_HEAD_DIM)),
                  c2((m, MEM_WIDTH)), c2((1, MEM_HEAD_DIM)), c2((m, MEM_WIDTH)), c2((MEM_WIDTH, d))],
        out_specs=_row_spec(tm, d),
        out_shape=jax.ShapeDtypeStruct((s, d), F32),
        compiler_params=_cparams(1), name="cross_attention",
    )(x, g.reshape(1, d), w_xq, g_q.reshape(1, -1), k_raw, g_k.reshape(1, -1), v, w_xo)


def _split_w_in(w_in):
    cuts = np.cumsum([0, MLA_Q_LORA, MLA_KV_LORA, MLA_ROPE, BRANCH_WIDTH, BRANCH_WIDTH, BRANCH_WIDTH,
                      BRANCH_WIDTH, BRANCH_WIDTH, BRANCH_WIDTH, BRANCH_WIDTH, HEAD_DIM, HEAD_DIM,
                      IDX_HEADS * IDX_DIM, IDX_DIM, IDX_HEADS])
    seg = lambda a, b: w_in[:, cuts[a]:cuts[b]]
    zeros = lambda n: jnp.zeros((w_in.shape[0], n), w_in.dtype)
    w_lat = jnp.concatenate([seg(0, 3), zeros(LANES - MLA_ROPE)], axis=1)
    w_sb = seg(3, 6)
    w_dil = seg(6, 9)
    w_dsa = seg(9, 12)
    w_idx = jnp.concatenate([seg(12, 14), zeros(LANES - IDX_DIM), seg(14, 15), zeros(LANES - IDX_HEADS)], axis=1)
    return [w.astype(BF16) for w in (w_lat, w_sb, w_dil, w_dsa, w_idx)]


def _pad_mla_up(w_uq, w_ukv):
    r = w_uq.shape[0]
    wq = w_uq.reshape(r, N_HEADS, MLA_QK)
    wq = jnp.concatenate([wq, jnp.zeros((r, N_HEADS, MLA_PAD - MLA_QK), wq.dtype)], axis=2)
    wkv = w_ukv.reshape(w_ukv.shape[0], N_HEADS, 2 * HEAD_DIM)
    wkv = jnp.concatenate([wkv[:, :, :MLA_NOPE].reshape(-1, BRANCH_WIDTH),
                           wkv[:, :, MLA_NOPE:].reshape(-1, BRANCH_WIDTH)], axis=1)
    return wq.reshape(r, N_HEADS * MLA_PAD).astype(BF16), wkv.astype(BF16)


def _branches(h, p, tabs):
    tab_p, tab_i, tab_m = tabs
    w_lat, w_sb, w_dil, w_dsa, w_idx = _split_w_in(p['w_in'])
    w_uq, w_ukv = _pad_mla_up(p['w_uq'], p['w_ukv'])

    cq, ckv, kpe = prep_latent(matmul(h, w_lat, out_dtype=F32), p['g_cq'], p['g_ckv'])
    q, k, v = prep_mla(matmul(cq, w_uq, out_dtype=F32), matmul(ckv, w_ukv, out_dtype=F32), kpe,
                       p['g_q_mla'], p['g_k_mla'], tab_m)
    o_mla = mla_attention(q, k, v)
    o_sb = sb_attention(matmul(h, w_sb, out_dtype=BF16))
    q, k, v = prep_dil(matmul(h, w_dil, out_dtype=F32), p['g_q_dil'], p['g_k_dil'], tab_p)
    o_dil = dil_attention(q, k, v)
    qs, ks, vs = prep_dsa(matmul(h, w_dsa, out_dtype=F32), p['g_q_dsa'], p['g_k_dsa'], tab_p)
    qi, ki, wi = prep_idx(matmul(h, w_idx, out_dtype=F32), tab_i)
    o_dsa = dsa_attention(qs, qi, wi, ki, ks, vs)
    return o_mla, o_sb, o_dil, o_dsa


def _token_mixer(x, h, p, tabs):
    o_mla, o_sb, o_dil, o_dsa = _branches(h, p, tabs)
    g_lat = matmul(h, p['w_gate_a'].astype(BF16), out_dtype=BF16)
    merged = gated_merge(g_lat, (o_mla, o_sb, o_dil, o_dsa), p['w_gate_b'].astype(BF16),
                         p['b_gate'].astype(F32), p['w_branch'].astype(BF16))
    return matmul(merged, p['w_out'].astype(BF16), out_dtype=F32, res=x)


def _cross_block(x, mem, p):
    m_n = rmsnorm_rows(mem, p['ln_mem'])
    k_raw = matmul(m_n, p['w_xk'].astype(BF16), out_dtype=F32)
    v = matmul(m_n, p['w_xv'].astype(BF16), out_dtype=BF16)
    return cross_attention(x, p['ln_xattn'], p['w_xq'].astype(BF16), p['g_q_x'], k_raw, p['g_k_x'], v,
                           p['w_xo'].astype(BF16))


def _ffn_block(x, g, wg, wu, wd, w_router=None):
    h = rmsnorm_rows(x, g)
    gates = None if w_router is None else router_gates(x, g, w_router)
    act = glu_up(h, wg.astype(BF16), wu.astype(BF16), gates)
    return matmul(act, wd.reshape(-1, wd.shape[-1]).astype(BF16), out_dtype=F32, res=x)


def kernel(x, mem, ln_mix, w_in, g_cq, g_ckv, w_uq, w_ukv, g_q_mla, g_k_mla, g_q_dil, g_k_dil, g_q_dsa, g_k_dsa, w_gate_a, w_gate_b, b_gate, w_branch, w_out, ln_xattn, ln_mem, w_xq, w_xk, w_xv, g_q_x, g_k_x, w_xo, ln_ffn, w_ff_gate, w_ff_up, w_ff_down, w_router, w_e_gate, w_e_up, w_e_down):
    b, s, d = x.shape
    per_layer = dict(w_in=w_in, g_cq=g_cq, g_ckv=g_ckv, w_uq=w_uq, w_ukv=w_ukv, g_q_mla=g_q_mla,
                     g_k_mla=g_k_mla, g_q_dil=g_q_dil, g_k_dil=g_k_dil, g_q_dsa=g_q_dsa, g_k_dsa=g_k_dsa,
                     w_gate_a=w_gate_a, w_gate_b=w_gate_b, b_gate=b_gate, w_branch=w_branch, w_out=w_out,
                     ln_xattn=ln_xattn, ln_mem=ln_mem, w_xq=w_xq, w_xk=w_xk, w_xv=w_xv, g_q_x=g_q_x,
                     g_k_x=g_k_x, w_xo=w_xo)
    tabs = (_rope_tables(s, ROT_DIM, HEAD_DIM), _rope_tables(s, IDX_ROT, IDX_DIM),
            _rope_tables(s, MLA_ROPE, HEAD_DIM))
    depth = ln_mix.shape[0]
    outs = []
    for bi in range(b):
        xb, mb = x[bi], mem[bi]
        for i in range(depth):
            p = {name: val[i] for name, val in per_layer.items()}
            xb = _token_mixer(xb, rmsnorm_rows(xb, ln_mix[i]), p, tabs)
            xb = _cross_block(xb, mb, p)
            j = i // 2
            if i % 2 == 0:
                xb = _ffn_block(xb, ln_ffn[i], w_ff_gate[j:j + 1], w_ff_up[j:j + 1], w_ff_down[j:j + 1])
            else:
                xb = _ffn_block(xb, ln_ffn[i], w_e_gate[j], w_e_up[j], w_e_down[j], w_router[j])
        outs.append(xb)
    return jnp.stack(outs, axis=0)
```

```python
import functools
import math

import numpy as np
import jax
import jax.numpy as jnp
from jax import lax
from jax.experimental import pallas as pl
from jax.experimental.pallas import tpu as pltpu

F32 = jnp.float32
BF16 = jnp.bfloat16
I32 = jnp.int32

N_BRANCHES = 4
HEAD_DIM = 128
N_HEADS = 8
BRANCH_WIDTH = N_HEADS * HEAD_DIM
ROT_DIM = HEAD_DIM // 4
ROPE_THETA = 500000.0
RMS_EPS = 1e-6
NEG = -1e30
GATE_RANK = 256
MLA_Q_LORA = 896
MLA_KV_LORA = 256
MLA_NOPE = 128
MLA_ROPE = 64
MLA_QK = MLA_NOPE + MLA_ROPE
MLA_PAD = 256
DIL_PATTERNS = ((128, 1), (512, 4), (2048, 16))
DSA_TOPK = 256
IDX_HEADS = 8
IDX_DIM = 64
IDX_ROT = IDX_DIM // 4
MEM_HEADS = 4
MEM_HEAD_DIM = 128
MEM_WIDTH = MEM_HEADS * MEM_HEAD_DIM
N_EXPERTS = 8
MOE_TOP_K = 2
INT_MIN = -2 ** 31
LOG2E = math.log2(math.e)

ATT_T = 512
DSA_TQ = 128
COUNT_ROWS = 64
SB_EXIT = -105.0

V7X_VMEM_LIMIT_BYTES = 56 * 1024 * 1024
LANES = 128

NT_DIMS = (((1,), (1,)), ((), ()))


def _cparams(n_axes):
    return pltpu.CompilerParams(dimension_semantics=("arbitrary",) * n_axes,
                                vmem_limit_bytes=V7X_VMEM_LIMIT_BYTES)


def _pick(n, candidates):
    for c in candidates:
        if n % c == 0:
            return c
    return n


def _rmsnorm_kernel(x_ref, g_ref, o_ref):
    x = x_ref[...].astype(F32)
    ms = jnp.mean(x * x, axis=-1, keepdims=True)
    o_ref[...] = (x * lax.rsqrt(ms + RMS_EPS) * g_ref[...]).astype(o_ref.dtype)


def rmsnorm_rows(x, g, out_dtype=BF16):
    m, d = x.shape
    tm = _pick(m, (512, 256, 128))
    return pl.pallas_call(
        _rmsnorm_kernel, grid=(m // tm,),
        in_specs=[pl.BlockSpec((tm, d), lambda i: (i, 0)), pl.BlockSpec((1, d), lambda i: (0, 0))],
        out_specs=pl.BlockSpec((tm, d), lambda i: (i, 0)),
        out_shape=jax.ShapeDtypeStruct((m, d), out_dtype),
        compiler_params=_cparams(1), name="rmsnorm_rows",
    )(x, g.reshape(1, d).astype(F32))


def _mm_kernel(*refs, nk, has_res):
    a_ref, b_ref = refs[0], refs[1]
    r_ref = refs[2] if has_res else None
    o_ref = refs[2 + has_res]
    part = jnp.dot(a_ref[...], b_ref[...], preferred_element_type=F32)
    if nk == 1:
        if has_res:
            part = r_ref[...] + part
        o_ref[...] = part.astype(o_ref.dtype)
        return
    acc_ref = refs[3 + has_res]
    k = pl.program_id(2)

    @pl.when(k == 0)
    def _():
        acc_ref[...] = part

    @pl.when(k > 0)
    def _():
        acc_ref[...] += part

    @pl.when(k == nk - 1)
    def _():
        res = acc_ref[...]
        if has_res:
            res = r_ref[...] + res
        o_ref[...] = res.astype(o_ref.dtype)


def matmul(a, b, *, out_dtype, res=None, tm=None, tn=None, tk=None):
    m, kdim = a.shape
    n = b.shape[1]
    tm = tm or _pick(m, (1024, 512, 256, 128))
    tn = tn or _pick(n, (512, 256, 128))
    tk = tk or (kdim if kdim <= 4096 else _pick(kdim, (2048, 1024, 512)))
    nk = kdim // tk
    in_specs = [pl.BlockSpec((tm, tk), lambda i, j, k: (i, k)),
                pl.BlockSpec((tk, tn), lambda i, j, k: (k, j))]
    args = [a, b]
    if res is not None:
        in_specs.append(pl.BlockSpec((tm, tn), lambda i, j, k: (i, j)))
        args.append(res)
    scratch = [pltpu.VMEM((tm, tn), F32)] if nk > 1 else []
    return pl.pallas_call(
        functools.partial(_mm_kernel, nk=nk, has_res=res is not None),
        grid=(m // tm, n // tn, nk),
        in_specs=in_specs,
        out_specs=pl.BlockSpec((tm, tn), lambda i, j, k: (i, j)),
        out_shape=jax.ShapeDtypeStruct((m, n), out_dtype),
        scratch_shapes=scratch,
        compiler_params=_cparams(3), name="matmul",
    )(*args)


def _glu_kernel(*refs, gated, tiles_per_expert):
    a_ref, wg_ref, wu_ref = refs[:3]
    o_ref = refs[3 + gated]
    a = a_ref[...]
    g = jnp.dot(a, wg_ref[...], preferred_element_type=F32)
    u = jnp.dot(a, wu_ref[...], preferred_element_type=F32)
    y = g * jax.nn.sigmoid(g) * u
    if gated:
        gates = refs[3][...]
        e = pl.program_id(1) // tiles_per_expert
        lane = lax.broadcasted_iota(I32, gates.shape, 1)
        y = y * jnp.sum(jnp.where(lane == e, gates, 0.0), axis=1, keepdims=True)
    o_ref[...] = y.astype(o_ref.dtype)


def glu_up(a, wg, wu, gates=None):
    m, kdim = a.shape
    n_e, _, f = wg.shape
    tm = _pick(m, (1024, 512, 256, 128))
    tn = _pick(f, (512, 256, 128))
    tpe = f // tn
    w_spec = pl.BlockSpec((None, kdim, tn), lambda i, j: (j // tpe, 0, j % tpe))
    in_specs = [pl.BlockSpec((tm, kdim), lambda i, j: (i, 0)), w_spec, w_spec]
    args = [a, wg, wu]
    if gates is not None:
        in_specs.append(pl.BlockSpec((tm, LANES), lambda i, j: (i, 0)))
        args.append(gates)
    return pl.pallas_call(
        functools.partial(_glu_kernel, gated=gates is not None, tiles_per_expert=tpe),
        grid=(m // tm, n_e * tpe),
        in_specs=in_specs,
        out_specs=pl.BlockSpec((tm, tn), lambda i, j: (i, j)),
        out_shape=jax.ShapeDtypeStruct((m, n_e * f), BF16),
        compiler_params=_cparams(2), name="glu_up",
    )(*args)


def _router_kernel(x_ref, g_ref, w_ref, o_ref):
    x = x_ref[...]
    ms = jnp.mean(x * x, axis=-1, keepdims=True)
    h = x * lax.rsqrt(ms + RMS_EPS) * g_ref[...]
    logits = jnp.dot(h, w_ref[...], preferred_element_type=F32, precision=lax.Precision.HIGHEST)
    lane = lax.broadcasted_iota(I32, logits.shape, 1).astype(F32)
    logits = jnp.where(lane < N_EXPERTS, logits, -jnp.inf)
    m1 = jnp.max(logits, axis=1, keepdims=True)
    i1 = jnp.min(jnp.where(logits == m1, lane, float(LANES)), axis=1, keepdims=True)
    rest = jnp.where(lane == i1, -jnp.inf, logits)
    m2 = jnp.max(rest, axis=1, keepdims=True)
    i2 = jnp.min(jnp.where(rest == m2, lane, float(LANES)), axis=1, keepdims=True)
    e2 = jnp.exp(m2 - m1)
    w1 = 1.0 / (1.0 + e2)
    w2 = e2 / (1.0 + e2)
    o_ref[...] = jnp.where(lane == i1, w1, 0.0) + jnp.where(lane == i2, w2, 0.0)


def router_gates(x, g, w_router):
    m, d = x.shape
    tm = _pick(m, (256, 128))
    w = jnp.zeros((d, LANES), F32).at[:, :N_EXPERTS].set(w_router.astype(F32))
    return pl.pallas_call(
        _router_kernel, grid=(m // tm,),
        in_specs=[pl.BlockSpec((tm, d), lambda i: (i, 0)), pl.BlockSpec((1, d), lambda i: (0, 0)),
                  pl.BlockSpec((d, LANES), lambda i: (0, 0))],
        out_specs=pl.BlockSpec((tm, LANES), lambda i: (i, 0)),
        out_shape=jax.ShapeDtypeStruct((m, LANES), F32),
        compiler_params=_cparams(1), name="router_gates",
    )(x, g.reshape(1, d).astype(F32), w)


def _rope(x, tab_ref, half):
    w = x.shape[-1]
    return (x * tab_ref[0] + pltpu.roll(x, w - half, 1) * tab_ref[1]
            + pltpu.roll(x, half, 1) * tab_ref[2])


def _rope_tables(seq, rot_dim, period):
    half = rot_dim // 2
    inv_freq = ROPE_THETA ** (-jnp.arange(0, rot_dim, 2, dtype=F32) / rot_dim)
    ang = jnp.arange(seq, dtype=F32)[:, None] * inv_freq[None, :]
    cos, sin = jnp.cos(ang), jnp.sin(ang)
    ones = jnp.ones((seq, period - rot_dim), F32)
    zeros = jnp.zeros((seq, period - rot_dim), F32)
    zh = jnp.zeros((seq, half), F32)
    c = jnp.concatenate([cos, cos, ones], axis=1)
    sa = jnp.concatenate([-sin, zh, zeros], axis=1)
    sb = jnp.concatenate([zh, sin, zeros], axis=1)
    rep = LANES // period
    return jnp.stack([jnp.tile(c, (1, rep)), jnp.tile(sa, (1, rep)), jnp.tile(sb, (1, rep))], axis=0)


def _tab_spec(tm):
    return pl.BlockSpec((3, tm, LANES), lambda i: (0, i, 0))


def _row_spec(tm, w):
    return pl.BlockSpec((tm, w), lambda i: (i, 0))


def _const_spec(w):
    return pl.BlockSpec((1, w), lambda i: (0, 0))


def _prep_latent_kernel(p_ref, gq_ref, gkv_ref, cq_ref, ckv_ref, kpe_ref):
    cq = p_ref[:, :MLA_Q_LORA]
    ms = jnp.mean(cq * cq, axis=-1, keepdims=True)
    cq_ref[...] = (cq * lax.rsqrt(ms + RMS_EPS) * gq_ref[...]).astype(cq_ref.dtype)
    ckv = p_ref[:, MLA_Q_LORA:MLA_Q_LORA + MLA_KV_LORA]
    ms = jnp.mean(ckv * ckv, axis=-1, keepdims=True)
    ckv_ref[...] = (ckv * lax.rsqrt(ms + RMS_EPS) * gkv_ref[...]).astype(ckv_ref.dtype)
    kpe_ref[...] = p_ref[:, MLA_Q_LORA + MLA_KV_LORA:]


def prep_latent(p, g_cq, g_ckv):
    s = p.shape[0]
    tm = _pick(s, (512, 256, 128))
    return pl.pallas_call(
        _prep_latent_kernel, grid=(s // tm,),
        in_specs=[_row_spec(tm, p.shape[1]), _const_spec(MLA_Q_LORA), _const_spec(MLA_KV_LORA)],
        out_specs=[_row_spec(tm, MLA_Q_LORA), _row_spec(tm, MLA_KV_LORA), _row_spec(tm, LANES)],
        out_shape=[jax.ShapeDtypeStruct((s, MLA_Q_LORA), BF16),
                   jax.ShapeDtypeStruct((s, MLA_KV_LORA), BF16),
                   jax.ShapeDtypeStruct((s, LANES), F32)],
        compiler_params=_cparams(1), name="prep_latent",
    )(p, g_cq.reshape(1, -1), g_ckv.reshape(1, -1))


def _prep_mla_kernel(qup_ref, kvup_ref, kpe_ref, gq_ref, gk_ref, tab_ref, qT_ref, k_ref, vT_ref):
    half = MLA_ROPE // 2
    gqn, gqr = gq_ref[:, :LANES], gq_ref[:, LANES:]
    gkn, gkr = gk_ref[:, :LANES], gk_ref[:, LANES:]
    kpe = kpe_ref[...]
    kpe_ss = jnp.sum(kpe * kpe, axis=-1, keepdims=True)
    kr_base = _rope(kpe * gkr, tab_ref, half)
    scale = LOG2E * MLA_QK ** -0.5
    for h in range(N_HEADS):
        qn = qup_ref[:, h * MLA_PAD:h * MLA_PAD + LANES]
        qr = qup_ref[:, h * MLA_PAD + LANES:(h + 1) * MLA_PAD]
        ss = jnp.sum(qn * qn, axis=-1, keepdims=True) + jnp.sum(qr * qr, axis=-1, keepdims=True)
        r = lax.rsqrt(ss * (1.0 / MLA_QK) + RMS_EPS) * scale
        qT_ref[h, :LANES, :] = (qn * r * gqn).T.astype(BF16)
        qT_ref[h, LANES:, :] = _rope(qr * r * gqr, tab_ref, half).T.astype(BF16)
        kn = kvup_ref[:, h * LANES:(h + 1) * LANES]
        ss = jnp.sum(kn * kn, axis=-1, keepdims=True) + kpe_ss
        r = lax.rsqrt(ss * (1.0 / MLA_QK) + RMS_EPS)
        k_ref[:, h * MLA_PAD:h * MLA_PAD + LANES] = (kn * r * gkn).astype(BF16)
        k_ref[:, h * MLA_PAD + LANES:(h + 1) * MLA_PAD] = (kr_base * r).astype(BF16)
        vT_ref[h] = kvup_ref[:, BRANCH_WIDTH + h * LANES:BRANCH_WIDTH + (h + 1) * LANES].T.astype(BF16)


def _qT_spec(d, t):
    return pl.BlockSpec((N_HEADS, d, t), lambda i: (0, 0, i))


def _vT_spec(t):
    return pl.BlockSpec((N_HEADS, None, HEAD_DIM, t), lambda i: (0, i, 0, 0))


def _qkv_shapes(s, dq, wk, t):
    return [jax.ShapeDtypeStruct((N_HEADS, dq, s), BF16), jax.ShapeDtypeStruct((s, wk), BF16),
            jax.ShapeDtypeStruct((N_HEADS, s // t, HEAD_DIM, t), BF16)]


def prep_mla(qup, kvup, kpe, g_q, g_k, tab):
    s = qup.shape[0]
    t = ATT_T
    pad = lambda g: jnp.zeros((1, MLA_PAD), F32).at[0, :MLA_QK].set(g)
    wq = N_HEADS * MLA_PAD
    return pl.pallas_call(
        _prep_mla_kernel, grid=(s // t,),
        in_specs=[_row_spec(t, wq), _row_spec(t, 2 * BRANCH_WIDTH), _row_spec(t, LANES),
                  _const_spec(MLA_PAD), _const_spec(MLA_PAD), _tab_spec(t)],
        out_specs=[_qT_spec(MLA_PAD, t), _row_spec(t, wq), _vT_spec(t)],
        out_shape=_qkv_shapes(s, MLA_PAD, wq, t),
        compiler_params=_cparams(1), name="prep_mla",
    )(qup, kvup, kpe, pad(g_q), pad(g_k), tab)


def _head_norm_rope(x, g, tab_ref, scale):
    ms = jnp.mean(x * x, axis=-1, keepdims=True)
    y = x * (lax.rsqrt(ms + RMS_EPS) * scale) * g
    return _rope(y, tab_ref, ROT_DIM // 2)


def _prep_sb_kernel(p_ref, qT_ref, k_ref, vT_ref, kn_ref):
    for h in range(N_HEADS):
        sl = slice(h * LANES, (h + 1) * LANES)
        qT_ref[h] = (p_ref[:, sl] * HEAD_DIM ** -0.5).T.astype(BF16)
        vT_ref[h] = p_ref[:, 2 * BRANCH_WIDTH + h * LANES:2 * BRANCH_WIDTH + (h + 1) * LANES].T.astype(BF16)
        kb = p_ref[:, BRANCH_WIDTH + h * LANES:BRANCH_WIDTH + (h + 1) * LANES].astype(BF16)
        k_ref[:, sl] = kb
        kf = kb.astype(F32)
        kn_ref[h] = jnp.zeros((8, LANES), F32) + jnp.max(jnp.sum(kf * kf, axis=1, keepdims=True))


def prep_sb(p):
    s = p.shape[0]
    t = ATT_T
    return pl.pallas_call(
        _prep_sb_kernel, grid=(s // t,),
        in_specs=[_row_spec(t, 3 * BRANCH_WIDTH)],
        out_specs=[_qT_spec(HEAD_DIM, t), _row_spec(t, BRANCH_WIDTH), _vT_spec(t),
                   pl.BlockSpec((N_HEADS, None, 8, LANES), lambda i: (0, i, 0, 0))],
        out_shape=_qkv_shapes(s, HEAD_DIM, BRANCH_WIDTH, t)
        + [jax.ShapeDtypeStruct((N_HEADS, s // t, 8, LANES), F32)],
        compiler_params=_cparams(1), name="prep_sb",
    )(p)


def _prep_dil_kernel(p_ref, gq_ref, gk_ref, tab_ref, qT_ref, k_ref, vT_ref):
    gq, gk = gq_ref[...], gk_ref[...]
    for h in range(N_HEADS):
        sl = slice(h * LANES, (h + 1) * LANES)
        qT_ref[h] = _head_norm_rope(p_ref[:, sl], gq, tab_ref, LOG2E * HEAD_DIM ** -0.5).T.astype(BF16)
        ksl = slice(BRANCH_WIDTH + h * LANES, BRANCH_WIDTH + (h + 1) * LANES)
        k_ref[:, sl] = _head_norm_rope(p_ref[:, ksl], gk, tab_ref, 1.0).astype(BF16)
        vT_ref[h] = p_ref[:, 2 * BRANCH_WIDTH + h * LANES:2 * BRANCH_WIDTH + (h + 1) * LANES].T.astype(BF16)


def prep_dil(p, g_q, g_k, tab):
    s = p.shape[0]
    t = ATT_T
    return pl.pallas_call(
        _prep_dil_kernel, grid=(s // t,),
        in_specs=[_row_spec(t, 3 * BRANCH_WIDTH), _const_spec(LANES), _const_spec(LANES), _tab_spec(t)],
        out_specs=[_qT_spec(HEAD_DIM, t), _row_spec(t, BRANCH_WIDTH), _vT_spec(t)],
        out_shape=_qkv_shapes(s, HEAD_DIM, BRANCH_WIDTH, t),
        compiler_params=_cparams(1), name="prep_dil",
    )(p, g_q.reshape(1, -1), g_k.reshape(1, -1), tab)


def _prep_dsa_kernel(p_ref, gq_ref, gk_ref, tab_ref, qT_ref, k_ref, vT_ref):
    gq, gk = gq_ref[...], gk_ref[...]
    for h in range(N_HEADS):
        sl = slice(h * LANES, (h + 1) * LANES)
        qT = _head_norm_rope(p_ref[:, sl], gq, tab_ref, LOG2E * HEAD_DIM ** -0.5).T.astype(BF16)
        for b in range(p_ref.shape[0] // DSA_TQ):
            qT_ref[b, :, sl] = qT[:, b * DSA_TQ:(b + 1) * DSA_TQ]
    k_ref[...] = _head_norm_rope(p_ref[:, BRANCH_WIDTH:BRANCH_WIDTH + LANES], gk, tab_ref, 1.0).astype(BF16)
    vT_ref[...] = p_ref[:, BRANCH_WIDTH + LANES:].T.astype(BF16)


def prep_dsa(p, g_q, g_k, tab):
    s = p.shape[0]
    t = ATT_T
    nqb = t // DSA_TQ
    return pl.pallas_call(
        _prep_dsa_kernel, grid=(s // t,),
        in_specs=[_row_spec(t, BRANCH_WIDTH + 2 * LANES), _const_spec(LANES), _const_spec(LANES), _tab_spec(t)],
        out_specs=[pl.BlockSpec((nqb, HEAD_DIM, BRANCH_WIDTH), lambda i: (i, 0, 0)), _row_spec(t, LANES),
                   pl.BlockSpec((None, HEAD_DIM, t), lambda i: (i, 0, 0))],
        out_shape=[jax.ShapeDtypeStruct((s // DSA_TQ, HEAD_DIM, BRANCH_WIDTH), BF16),
                   jax.ShapeDtypeStruct((s, LANES), BF16),
                   jax.ShapeDtypeStruct((s // t, HEAD_DIM, t), BF16)],
        compiler_params=_cparams(1), name="prep_dsa",
    )(p, g_q.reshape(1, -1), g_k.reshape(1, -1), tab)


def _prep_idx_kernel(p_ref, tab_ref, qiT_ref, ki_ref, wiT_ref):
    half = IDX_ROT // 2
    t = p_ref.shape[0]
    nqb = t // DSA_TQ
    lane = lax.broadcasted_iota(I32, (t, LANES), 1)
    first = lane < IDX_DIM
    zero = jnp.zeros((t, LANES), F32)
    for b in range(IDX_HEADS // 2):
        x = _rope(p_ref[:, b * LANES:(b + 1) * LANES], tab_ref, half) * (IDX_DIM ** -0.5)
        hi = x.astype(BF16).astype(F32)
        lo = x - hi
        rhi = pltpu.roll(hi, IDX_DIM, 1)
        for hh, (a0, a1) in enumerate(((jnp.where(first, hi, pltpu.roll(lo, IDX_DIM, 1)), jnp.where(first, hi, zero)),
                                       (jnp.where(first, rhi, lo), jnp.where(first, rhi, zero)))):
            h = 2 * b + hh
            a0T, a1T = a0.T.astype(BF16), a1.T.astype(BF16)
            for qb in range(nqb):
                qs = slice(qb * DSA_TQ, (qb + 1) * DSA_TQ)
                qiT_ref[qb, :LANES, h * DSA_TQ:(h + 1) * DSA_TQ] = a0T[:, qs]
                qiT_ref[qb, LANES:, h * DSA_TQ:(h + 1) * DSA_TQ] = a1T[:, qs]
    kx = _rope(p_ref[:, IDX_HEADS * IDX_DIM:IDX_HEADS * IDX_DIM + LANES], tab_ref, half)
    hi = kx.astype(BF16).astype(F32)
    lo = kx - hi
    ki_ref[:, :LANES] = jnp.where(first, hi, pltpu.roll(hi, IDX_DIM, 1)).astype(BF16)
    ki_ref[:, LANES:] = jnp.where(first, lo, zero).astype(BF16)
    wT = (p_ref[:, IDX_HEADS * IDX_DIM + LANES:] * (IDX_HEADS ** -0.5)).T
    for qb in range(nqb):
        wiT_ref[qb] = wT[:IDX_HEADS, qb * DSA_TQ:(qb + 1) * DSA_TQ]


def prep_idx(p, tab):
    s = p.shape[0]
    t = ATT_T
    nqb = t // DSA_TQ
    return pl.pallas_call(
        _prep_idx_kernel, grid=(s // t,),
        in_specs=[_row_spec(t, IDX_HEADS * IDX_DIM + 2 * LANES), _tab_spec(t)],
        out_specs=[pl.BlockSpec((nqb, 2 * LANES, IDX_HEADS * DSA_TQ), lambda i: (i, 0, 0)),
                   _row_spec(t, 2 * LANES),
                   pl.BlockSpec((nqb, IDX_HEADS, DSA_TQ), lambda i: (i, 0, 0))],
        out_shape=[jax.ShapeDtypeStruct((s // DSA_TQ, 2 * LANES, IDX_HEADS * DSA_TQ), BF16),
                   jax.ShapeDtypeStruct((s, 2 * LANES), BF16),
                   jax.ShapeDtypeStruct((s // DSA_TQ, IDX_HEADS, DSA_TQ), F32)],
        compiler_params=_cparams(1), name="prep_idx",
    )(p, tab)


def _softmax_step(sT, vT, m_sc, l_sc, acc_sc):
    m_prev = m_sc[...]
    m_new = jnp.maximum(m_prev, jnp.max(sT, axis=0, keepdims=True))
    alpha = jnp.exp2(m_prev - m_new)
    p = jnp.exp2(sT - m_new)
    l_sc[...] = alpha * l_sc[...] + jnp.sum(p, axis=0, keepdims=True)
    acc_sc[...] = alpha * acc_sc[...] + jnp.dot(vT, p.astype(BF16), preferred_element_type=F32)
    m_sc[...] = m_new


def _softmax_init(m_sc, l_sc, acc_sc):
    m_sc[...] = jnp.full(m_sc.shape, NEG, F32)
    l_sc[...] = jnp.zeros(l_sc.shape, F32)
    acc_sc[...] = jnp.zeros(acc_sc.shape, F32)


def _pipelined_blocks(n_full, scores, consume, consume_last, s_sc):
    s_sc[0] = scores(0)

    def pair(jj, c):
        s_sc[1] = scores(2 * jj + 1)
        consume(s_sc[0], 2 * jj)
        s_sc[0] = scores(2 * jj + 2)
        consume(s_sc[1], 2 * jj + 1)
        return c

    lax.fori_loop(0, n_full // 2, pair, 0)

    @pl.when(n_full % 2 == 1)
    def _():
        s_sc[1] = scores(n_full)
        consume(s_sc[0], n_full - 1)
        consume_last(s_sc[1], n_full)

    @pl.when(n_full % 2 == 0)
    def _():
        consume_last(s_sc[0], n_full)


def _key_le_query(tk, tq):
    return lax.broadcasted_iota(I32, (tk, tq), 0) <= lax.broadcasted_iota(I32, (tk, tq), 1)


def _mla_kernel(qT_ref, k_ref, vT_ref, o_ref, s_sc, m_sc, l_sc, acc_sc):
    i = pl.program_id(1)
    t = qT_ref.shape[1]
    qT = qT_ref[...]
    _softmax_init(m_sc, l_sc, acc_sc)
    scores = lambda j: jnp.dot(k_ref[j], qT, preferred_element_type=F32)
    step = lambda sT, j: _softmax_step(sT, vT_ref[j], m_sc, l_sc, acc_sc)
    last = lambda sT, j: _softmax_step(jnp.where(_key_le_query(t, t), sT, NEG), vT_ref[j], m_sc, l_sc, acc_sc)
    _pipelined_blocks(i, scores, step, last, s_sc)
    o_ref[...] = (acc_sc[...] / l_sc[...]).T.astype(o_ref.dtype)


def _attn_scratch(tk, tq):
    return [pltpu.VMEM((2, tk, tq), F32), pltpu.VMEM((1, tq), F32), pltpu.VMEM((1, tq), F32),
            pltpu.VMEM((HEAD_DIM, tq), F32)]


def _head_qT_spec(d, t):
    return pl.BlockSpec((None, d, t), lambda h, i: (h, 0, i))


def _head_k_spec(nb, t, w):
    return pl.BlockSpec((nb, t, w), lambda h, i: (0, 0, h))


def _head_vT_spec(nb, t):
    return pl.BlockSpec((None, nb, HEAD_DIM, t), lambda h, i: (h, 0, 0, 0))


def _head_out_spec(t):
    return pl.BlockSpec((t, HEAD_DIM), lambda h, i: (i, h))


def mla_attention(qT, k, vT):
    s = k.shape[0]
    t = ATT_T
    nb = s // t
    return pl.pallas_call(
        _mla_kernel, grid=(N_HEADS, nb),
        in_specs=[_head_qT_spec(MLA_PAD, t), _head_k_spec(nb, t, MLA_PAD), _head_vT_spec(nb, t)],
        out_specs=_head_out_spec(t),
        out_shape=jax.ShapeDtypeStruct((s, BRANCH_WIDTH), BF16),
        scratch_shapes=_attn_scratch(t, t),
        compiler_params=_cparams(2), name="mla_attention",
    )(qT, k.reshape(nb, t, -1), vT)


SB_SUB = 128


def _sb_kernel(qT_ref, k_ref, vT_ref, u_ref, kn_ref, o_ref, carry_sc, acc_sc):
    i = pl.program_id(1)
    t = qT_ref.shape[1]
    qT = qT_ref[...]
    u2 = u_ref[...]
    qf = qT.astype(F32)
    zbound = jnp.sqrt(jnp.sum(qf * qf, axis=0, keepdims=True) * jnp.max(kn_ref[...]))
    carry_sc[...] = jnp.zeros(carry_sc.shape, F32)
    acc_sc[...] = jnp.zeros(acc_sc.shape, F32)
    key = lax.broadcasted_iota(I32, (SB_SUB, t), 0)
    qry = lax.broadcasted_iota(I32, (SB_SUB, t), 1)

    def block(j, diag):
        zT = jnp.dot(k_ref[j], qT, preferred_element_type=F32)
        carry = carry_sc[...]
        parts = [None] * (t // SB_SUB)
        for c in reversed(range(t // SB_SUB)):
            zc = zT[c * SB_SUB:(c + 1) * SB_SUB]
            ls = jnp.minimum(-zc, 0.0) - jnp.log(1.0 + jnp.exp(-jnp.abs(zc)))
            if diag:
                past = key + c * SB_SUB < qry
                ls = jnp.where(past, ls, 0.0)
            hi = ls.astype(BF16)
            lo = (ls - hi.astype(F32)).astype(BF16)
            rev = jnp.dot(u2, jnp.concatenate([hi, lo], axis=0), preferred_element_type=F32) + carry
            a = jnp.exp(jnp.minimum(zc + rev, 0.0))
            if diag:
                a = jnp.where(past, a, 0.0)
            parts[c] = a.astype(BF16)
            carry = rev[0:1, :]
        carry_sc[...] = carry
        acc_sc[...] += jnp.dot(vT_ref[j], jnp.concatenate(parts, axis=0), preferred_element_type=F32)

    def all_underflow():
        return (jnp.max(carry_sc[...] + zbound) < SB_EXIT).astype(I32)

    block(i, True)

    def back(state):
        jj, _ = state
        block(i - 1 - jj, False)
        return jj + 1, all_underflow()

    lax.while_loop(lambda st: jnp.logical_and(st[0] < i, st[1] == 0), back, (jnp.int32(0), all_underflow()))
    o_ref[...] = acc_sc[...].T.astype(o_ref.dtype)


def sb_attention(qT, k, vT, kn):
    s = k.shape[0]
    t = ATT_T
    nb = s // t
    tri = (np.arange(SB_SUB)[None, :] >= np.arange(SB_SUB)[:, None]).astype(np.float32)
    u2 = jnp.asarray(np.concatenate([tri, tri], axis=1), dtype=BF16)
    return pl.pallas_call(
        _sb_kernel, grid=(N_HEADS, nb),
        in_specs=[_head_qT_spec(HEAD_DIM, t), _head_k_spec(nb, t, HEAD_DIM), _head_vT_spec(nb, t),
                  pl.BlockSpec((SB_SUB, 2 * SB_SUB), lambda h, i: (0, 0)),
                  pl.BlockSpec((None, nb, 8, LANES), lambda h, i: (h, 0, 0, 0))],
        out_specs=_head_out_spec(t),
        out_shape=jax.ShapeDtypeStruct((s, BRANCH_WIDTH), BF16),
        scratch_shapes=[pltpu.VMEM((1, t), F32), pltpu.VMEM((HEAD_DIM, t), F32)],
        compiler_params=_cparams(2), name="sb_attention",
    )(qT, k.reshape(nb, t, -1), vT, u2, kn)


def _dil_log_weights(t):
    span = max(w for w, _ in DIL_PATTERNS)
    nback = -(-span // t)
    d = np.arange(nback + 1)[:, None, None] * t + np.arange(t)[None, None, :] - np.arange(t)[None, :, None]
    mult = np.zeros(d.shape, np.float64)
    for window, dil in DIL_PATTERNS:
        mult += ((d >= 0) & (d <= window) & (d % dil == 0))
    return np.where(mult > 0, np.log2(np.maximum(mult, 1.0)), NEG).astype(np.float32)


def _dil_kernel(qT_ref, k_ref, vT_ref, w_ref, o_ref, s_sc, m_sc, l_sc, acc_sc):
    i = pl.program_id(1)
    nback = w_ref.shape[0] - 1
    qT = qT_ref[...]
    _softmax_init(m_sc, l_sc, acc_sc)
    scores = lambda d: jnp.dot(k_ref[i - d], qT, preferred_element_type=F32)
    step = lambda sT, d: _softmax_step(sT + w_ref[d], vT_ref[i - d], m_sc, l_sc, acc_sc)
    _pipelined_blocks(jnp.minimum(i, nback), scores, step, step, s_sc)
    o_ref[...] = (acc_sc[...] / l_sc[...]).T.astype(o_ref.dtype)


def dil_attention(qT, k, vT):
    s = k.shape[0]
    t = ATT_T
    nb = s // t
    logw = jnp.asarray(_dil_log_weights(t))
    return pl.pallas_call(
        _dil_kernel, grid=(N_HEADS, nb),
        in_specs=[_head_qT_spec(HEAD_DIM, t), _head_k_spec(nb, t, HEAD_DIM), _head_vT_spec(nb, t),
                  pl.BlockSpec(logw.shape, lambda h, i: (0, 0, 0))],
        out_specs=_head_out_spec(t),
        out_shape=jax.ShapeDtypeStruct((s, BRANCH_WIDTH), BF16),
        scratch_shapes=_attn_scratch(t, t),
        compiler_params=_cparams(2), name="dil_attention",
    )(qT, k.reshape(nb, t, -1), vT, logw)


def _dsa_kernel(qsT_ref, qiT_ref, wiT_ref, ki_ref, k_ref, vT_ref, o_ref,
                key_sc, q1_sc, s_sc, m_sc, l_sc, acc_sc, *, top_k, pos_bits):
    i = pl.program_id(0)
    tq, tk = DSA_TQ, ATT_T
    nkb = (i * tq + tq + tk - 1) // tk
    qiT = qiT_ref[...]
    wiT = wiT_ref[...]
    row = lax.broadcasted_iota(I32, (tk, tq), 1) + i * tq
    col = lax.broadcasted_iota(I32, (tk, tq), 0)

    def idx_block(aT, j):
        idx = jnp.zeros((tk, tq), F32)
        for h in range(IDX_HEADS):
            idx = idx + jnp.maximum(aT[:, h * tq:(h + 1) * tq], 0.0) * wiT[h:h + 1, :]
        bits = pltpu.bitcast(idx + 0.0, I32)
        key = bits ^ ((bits >> 31) & 0x7FFFFFFF)
        key_sc[j] = jnp.where(col + j * tk <= row, key, INT_MIN)

    _pipelined_blocks(nkb - 1, lambda j: jnp.dot(ki_ref[j], qiT, preferred_element_type=F32),
                      idx_block, idx_block, s_sc)

    def count(hits):
        def body(j, acc):
            hit = hits(key_sc[j], j)
            return acc + jnp.sum(hit.reshape(tk // COUNT_ROWS, COUNT_ROWS, tq), axis=0)
        acc = lax.fori_loop(0, nkb, body, jnp.zeros((COUNT_ROWS, tq), F32))
        return jnp.sum(acc, axis=0, keepdims=True)

    def bit_body(b, carry):
        thr, cge = carry
        cand = thr + jnp.left_shift(jnp.int32(1), 31 - b)
        c = count(lambda kb, j: jnp.where(kb >= cand, 1.0, 0.0))
        ok = c >= top_k
        return jnp.where(ok, cand, thr), jnp.where(ok, c, cge)

    thr0 = jnp.full((1, tq), INT_MIN, I32)
    cge0 = jnp.zeros((1, tq), F32) + (nkb * tk).astype(F32)
    thr, cge = lax.fori_loop(0, 32, bit_body, (thr0, cge0))

    q1_sc[...] = jnp.full((1, tq), 2 ** 30, I32)

    @pl.when(jnp.max(cge) > top_k)
    def _():
        need = top_k - count(lambda kb, j: jnp.where(kb > thr, 1.0, 0.0))

        def pos_body(b, qpos):
            cand = qpos + jnp.left_shift(jnp.int32(1), pos_bits - 1 - b)
            g = count(lambda kb, j: jnp.where(kb == thr, jnp.where(col + j * tk < cand, 1.0, 0.0), 0.0))
            return jnp.where(g < need, cand, qpos)

        q1_sc[...] = lax.fori_loop(0, pos_bits, pos_body, jnp.zeros((1, tq), I32)) + 1

    q1 = q1_sc[...]
    qsT = qsT_ref[...]
    _softmax_init(m_sc, l_sc, acc_sc)
    def att_block(sT, j):
        kb = key_sc[j]
        pos = col + j * tk
        bias = jnp.where(kb > thr, 0.0, jnp.where(kb == thr, jnp.where(pos < q1, 0.0, NEG), NEG))
        bias = jnp.where(pos <= row, bias, NEG)
        _softmax_step(sT + jnp.concatenate([bias] * N_HEADS, axis=1), vT_ref[j], m_sc, l_sc, acc_sc)

    _pipelined_blocks(nkb - 1, lambda j: jnp.dot(k_ref[j], qsT, preferred_element_type=F32),
                      att_block, att_block, s_sc)
    o = acc_sc[...] / l_sc[...]
    for h in range(N_HEADS):
        sl = slice(h * HEAD_DIM, (h + 1) * HEAD_DIM)
        o_ref[:, sl] = o[:, h * tq:(h + 1) * tq].T.astype(o_ref.dtype)


def dsa_attention(qsT, qiT, wiT, ki, k, vT):
    s = k.shape[0]
    tq, tk = DSA_TQ, ATT_T
    nkb = s // tk
    top_k = min(DSA_TOPK, s // 4)
    wq = N_HEADS * tq
    full3 = lambda a, b: pl.BlockSpec((nkb, a, b), lambda i: (0, 0, 0))
    return pl.pallas_call(
        functools.partial(_dsa_kernel, top_k=top_k, pos_bits=(s - 1).bit_length()),
        grid=(s // tq,),
        in_specs=[pl.BlockSpec((None, HEAD_DIM, wq), lambda i: (i, 0, 0)),
                  pl.BlockSpec((None, 2 * LANES, wq), lambda i: (i, 0, 0)),
                  pl.BlockSpec((None, IDX_HEADS, tq), lambda i: (i, 0, 0)),
                  full3(tk, 2 * LANES), full3(tk, HEAD_DIM), full3(HEAD_DIM, tk)],
        out_specs=pl.BlockSpec((tq, BRANCH_WIDTH), lambda i: (i, 0)),
        out_shape=jax.ShapeDtypeStruct((s, BRANCH_WIDTH), BF16),
        scratch_shapes=[pltpu.VMEM((nkb, tk, tq), I32), pltpu.VMEM((1, tq), I32)] + _attn_scratch(tk, wq),
        compiler_params=_cparams(1), name="dsa_attention",
    )(qsT, qiT, wiT, ki.reshape(nkb, tk, -1), k.reshape(nkb, tk, -1), vT)


def _merge_kernel(g_ref, o0_ref, o1_ref, o2_ref, o3_ref, wgb_ref, b_ref, wbr_ref, out_ref):
    g = g_ref[...]
    acc = None
    for n, o_ref in enumerate((o0_ref, o1_ref, o2_ref, o3_ref)):
        gate = jax.nn.sigmoid(jnp.dot(g, wgb_ref[n], preferred_element_type=F32) + b_ref[n])
        val = gate * jnp.dot(o_ref[...], wbr_ref[n], preferred_element_type=F32)
        acc = val if acc is None else acc + val
    out_ref[...] = acc.astype(out_ref.dtype)


def gated_merge(g_lat, outs, w_gate_b, b_gate, w_branch):
    s = g_lat.shape[0]
    d = w_gate_b.shape[-1]
    tm = _pick(s, (1024, 512, 256, 128))
    tn = _pick(d, (512, 256, 128))
    o_spec = pl.BlockSpec((tm, BRANCH_WIDTH), lambda i, j: (i, 0))
    return pl.pallas_call(
        _merge_kernel, grid=(s // tm, d // tn),
        in_specs=[pl.BlockSpec((tm, GATE_RANK), lambda i, j: (i, 0)), o_spec, o_spec, o_spec, o_spec,
                  pl.BlockSpec((N_BRANCHES, GATE_RANK, tn), lambda i, j: (0, 0, j)),
                  pl.BlockSpec((N_BRANCHES, 1, tn), lambda i, j: (0, 0, j)),
                  pl.BlockSpec((N_BRANCHES, BRANCH_WIDTH, tn), lambda i, j: (0, 0, j))],
        out_specs=pl.BlockSpec((tm, tn), lambda i, j: (i, j)),
        out_shape=jax.ShapeDtypeStruct((s, d), BF16),
        compiler_params=_cparams(2), name="gated_merge",
    )(g_lat, *outs, w_gate_b, b_gate.reshape(N_BRANCHES, 1, d), w_branch)


def _cross_kernel(x_ref, g_ref, wq_ref, gq_ref, kraw_ref, gk_ref, v_ref, wo_ref, o_ref):
    x = x_ref[...]
    ms = jnp.mean(x * x, axis=-1, keepdims=True)
    h = (x * lax.rsqrt(ms + RMS_EPS) * g_ref[...]).astype(BF16)
    q = jnp.dot(h, wq_ref[...], preferred_element_type=F32)
    outs = []
    for hd in range(MEM_HEADS):
        sl = slice(hd * MEM_HEAD_DIM, (hd + 1) * MEM_HEAD_DIM)
        qh = q[:, sl]
        qh = qh * (lax.rsqrt(jnp.mean(qh * qh, axis=-1, keepdims=True) + RMS_EPS) * MEM_HEAD_DIM ** -0.5) * gq_ref[...]
        kh = kraw_ref[:, sl]
        kh = kh * lax.rsqrt(jnp.mean(kh * kh, axis=-1, keepdims=True) + RMS_EPS) * gk_ref[...]
        s = lax.dot_general(qh.astype(BF16), kh.astype(BF16), NT_DIMS, preferred_element_type=F32)
        p = jnp.exp(s - jnp.max(s, axis=1, keepdims=True))
        p = p / jnp.sum(p, axis=1, keepdims=True)
        outs.append(jnp.dot(p.astype(BF16), v_ref[:, sl], preferred_element_type=F32).astype(BF16))
    o = jnp.concatenate(outs, axis=1)
    o_ref[...] = x + jnp.dot(o, wo_ref[...], preferred_element_type=F32)


def cross_attention(x, g, w_xq, g_q, k_raw, g_k, v, w_xo):
    s, d = x.shape
    m = k_raw.shape[0]
    tm = _pick(s, (256, 128))
    c2 = lambda shape: pl.BlockSpec(shape, lambda i: (0, 0))
    return pl.pallas_call(
        _cross_kernel, grid=(s // tm,),
        in_specs=[_row_spec(tm, d), c2((1, d)), c2((d, MEM_WIDTH)), c2((1, MEM_HEAD_DIM)),
                  c2((m, MEM_WIDTH)), c2((1, MEM_HEAD_DIM)), c2((m, MEM_WIDTH)), c2((MEM_WIDTH, d))],
        out_specs=_row_spec(tm, d),
        out_shape=jax.ShapeDtypeStruct((s, d), F32),
        compiler_params=_cparams(1), name="cross_attention",
    )(x, g.reshape(1, d), w_xq, g_q.reshape(1, -1), k_raw, g_k.reshape(1, -1), v, w_xo)


def _split_w_in(w_in):
    cuts = np.cumsum([0, MLA_Q_LORA, MLA_KV_LORA, MLA_ROPE, BRANCH_WIDTH, BRANCH_WIDTH, BRANCH_WIDTH,
                      BRANCH_WIDTH, BRANCH_WIDTH, BRANCH_WIDTH, BRANCH_WIDTH, HEAD_DIM, HEAD_DIM,
                      IDX_HEADS * IDX_DIM, IDX_DIM, IDX_HEADS])
    seg = lambda a, b: w_in[:, cuts[a]:cuts[b]]
    zeros = lambda n: jnp.zeros((w_in.shape[0], n), w_in.dtype)
    w_lat = jnp.concatenate([seg(0, 3), zeros(LANES - MLA_ROPE)], axis=1)
    w_sb = seg(3, 6)
    w_dil = seg(6, 9)
    w_dsa = seg(9, 12)
    w_idx = jnp.concatenate([seg(12, 14), zeros(LANES - IDX_DIM), seg(14, 15), zeros(LANES - IDX_HEADS)], axis=1)
    return [w.astype(BF16) for w in (w_lat, w_sb, w_dil, w_dsa, w_idx)]


def _pad_mla_up(w_uq, w_ukv):
    r = w_uq.shape[0]
    wq = w_uq.reshape(r, N_HEADS, MLA_QK)
    wq = jnp.concatenate([wq, jnp.zeros((r, N_HEADS, MLA_PAD - MLA_QK), wq.dtype)], axis=2)
    wkv = w_ukv.reshape(w_ukv.shape[0], N_HEADS, 2 * HEAD_DIM)
    wkv = jnp.concatenate([wkv[:, :, :MLA_NOPE].reshape(-1, BRANCH_WIDTH),
                           wkv[:, :, MLA_NOPE:].reshape(-1, BRANCH_WIDTH)], axis=1)
    return wq.reshape(r, N_HEADS * MLA_PAD).astype(BF16), wkv.astype(BF16)


def _branches(h, p, tabs):
    tab_p, tab_i, tab_m = tabs
    w_lat, w_sb, w_dil, w_dsa, w_idx = _split_w_in(p['w_in'])
    w_uq, w_ukv = _pad_mla_up(p['w_uq'], p['w_ukv'])

    cq, ckv, kpe = prep_latent(matmul(h, w_lat, out_dtype=F32), p['g_cq'], p['g_ckv'])
    qT, k, vT = prep_mla(matmul(cq, w_uq, out_dtype=F32), matmul(ckv, w_ukv, out_dtype=F32), kpe,
                         p['g_q_mla'], p['g_k_mla'], tab_m)
    o_mla = mla_attention(qT, k, vT)
    o_sb = sb_attention(*prep_sb(matmul(h, w_sb, out_dtype=F32)))
    o_dil = dil_attention(*prep_dil(matmul(h, w_dil, out_dtype=F32), p['g_q_dil'], p['g_k_dil'], tab_p))
    qsT, ks, vsT = prep_dsa(matmul(h, w_dsa, out_dtype=F32), p['g_q_dsa'], p['g_k_dsa'], tab_p)
    qiT, ki, wiT = prep_idx(matmul(h, w_idx, out_dtype=F32), tab_i)
    o_dsa = dsa_attention(qsT, qiT, wiT, ki, ks, vsT)
    return o_mla, o_sb, o_dil, o_dsa


def _token_mixer(x, h, p, tabs):
    o_mla, o_sb, o_dil, o_dsa = _branches(h, p, tabs)
    g_lat = matmul(h, p['w_gate_a'].astype(BF16), out_dtype=BF16)
    merged = gated_merge(g_lat, (o_mla, o_sb, o_dil, o_dsa), p['w_gate_b'].astype(BF16),
                         p['b_gate'].astype(F32), p['w_branch'].astype(BF16))
    return matmul(merged, p['w_out'].astype(BF16), out_dtype=F32, res=x)


def _cross_block(x, mem, p):
    m_n = rmsnorm_rows(mem, p['ln_mem'])
    k_raw = matmul(m_n, p['w_xk'].astype(BF16), out_dtype=F32)
    v = matmul(m_n, p['w_xv'].astype(BF16), out_dtype=BF16)
    return cross_attention(x, p['ln_xattn'], p['w_xq'].astype(BF16), p['g_q_x'], k_raw, p['g_k_x'], v,
                           p['w_xo'].astype(BF16))


def _ffn_block(x, g, wg, wu, wd, w_router=None):
    h = rmsnorm_rows(x, g)
    gates = None if w_router is None else router_gates(x, g, w_router)
    act = glu_up(h, wg.astype(BF16), wu.astype(BF16), gates)
    return matmul(act, wd.reshape(-1, wd.shape[-1]).astype(BF16), out_dtype=F32, res=x)


def kernel(x, mem, ln_mix, w_in, g_cq, g_ckv, w_uq, w_ukv, g_q_mla, g_k_mla, g_q_dil, g_k_dil, g_q_dsa, g_k_dsa, w_gate_a, w_gate_b, b_gate, w_branch, w_out, ln_xattn, ln_mem, w_xq, w_xk, w_xv, g_q_x, g_k_x, w_xo, ln_ffn, w_ff_gate, w_ff_up, w_ff_down, w_router, w_e_gate, w_e_up, w_e_down):
    b, s, d = x.shape
    per_layer = dict(w_in=w_in, g_cq=g_cq, g_ckv=g_ckv, w_uq=w_uq, w_ukv=w_ukv, g_q_mla=g_q_mla,
                     g_k_mla=g_k_mla, g_q_dil=g_q_dil, g_k_dil=g_k_dil, g_q_dsa=g_q_dsa, g_k_dsa=g_k_dsa,
                     w_gate_a=w_gate_a, w_gate_b=w_gate_b, b_gate=b_gate, w_branch=w_branch, w_out=w_out,
                     ln_xattn=ln_xattn, ln_mem=ln_mem, w_xq=w_xq, w_xk=w_xk, w_xv=w_xv, g_q_x=g_q_x,
                     g_k_x=g_k_x, w_xo=w_xo)
    tabs = (_rope_tables(s, ROT_DIM, HEAD_DIM), _rope_tables(s, IDX_ROT, IDX_DIM),
            _rope_tables(s, MLA_ROPE, HEAD_DIM))
    depth = ln_mix.shape[0]
    outs = []
    for bi in range(b):
        xb, mb = x[bi], mem[bi]
        for i in range(depth):
            p = {name: val[i] for name, val in per_layer.items()}
            xb = _token_mixer(xb, rmsnorm_rows(xb, ln_mix[i]), p, tabs)
            xb = _cross_block(xb, mb, p)
            j = i // 2
            if i % 2 == 0:
                xb = _ffn_block(xb, ln_ffn[i], w_ff_gate[j:j + 1], w_ff_up[j:j + 1], w_ff_down[j:j + 1])
            else:
                xb = _ffn_block(xb, ln_ffn[i], w_e_gate[j], w_e_up[j], w_e_down[j], w_router[j])
        outs.append(xb)
    return jnp.stack(outs, axis=0)
```

```python
import functools
import math

import numpy as np
import jax
import jax.numpy as jnp
from jax import lax
from jax.experimental import pallas as pl
from jax.experimental.pallas import tpu as pltpu

F32 = jnp.float32
BF16 = jnp.bfloat16
I32 = jnp.int32

N_BRANCHES = 4
HEAD_DIM = 128
N_HEADS = 8
BRANCH_WIDTH = N_HEADS * HEAD_DIM
ROT_DIM = HEAD_DIM // 4
ROPE_THETA = 500000.0
RMS_EPS = 1e-6
NEG = -1e30
GATE_RANK = 256
MLA_Q_LORA = 896
MLA_KV_LORA = 256
MLA_NOPE = 128
MLA_ROPE = 64
MLA_QK = MLA_NOPE + MLA_ROPE
MLA_PAD = 256
DIL_PATTERNS = ((128, 1), (512, 4), (2048, 16))
DSA_TOPK = 256
IDX_HEADS = 8
IDX_DIM = 64
IDX_ROT = IDX_DIM // 4
MEM_HEADS = 4
MEM_HEAD_DIM = 128
MEM_WIDTH = MEM_HEADS * MEM_HEAD_DIM
N_EXPERTS = 8
MOE_TOP_K = 2
INT_MIN = -2 ** 31
LOG2E = math.log2(math.e)

ATT_T = 512
DSA_TQ = 128
COUNT_ROWS = 64
SB_EXIT = -105.0

V7X_VMEM_LIMIT_BYTES = 56 * 1024 * 1024
LANES = 128

NT_DIMS = (((1,), (1,)), ((), ()))


def _cparams(n_axes):
    return pltpu.CompilerParams(dimension_semantics=("arbitrary",) * n_axes,
                                vmem_limit_bytes=V7X_VMEM_LIMIT_BYTES)


def _pick(n, candidates):
    for c in candidates:
        if n % c == 0:
            return c
    return n


def _rmsnorm_kernel(x_ref, g_ref, o_ref):
    x = x_ref[...].astype(F32)
    ms = jnp.mean(x * x, axis=-1, keepdims=True)
    o_ref[...] = (x * lax.rsqrt(ms + RMS_EPS) * g_ref[...]).astype(o_ref.dtype)


def rmsnorm_rows(x, g, out_dtype=BF16):
    m, d = x.shape
    tm = _pick(m, (512, 256, 128))
    return pl.pallas_call(
        _rmsnorm_kernel, grid=(m // tm,),
        in_specs=[pl.BlockSpec((tm, d), lambda i: (i, 0)), pl.BlockSpec((1, d), lambda i: (0, 0))],
        out_specs=pl.BlockSpec((tm, d), lambda i: (i, 0)),
        out_shape=jax.ShapeDtypeStruct((m, d), out_dtype),
        compiler_params=_cparams(1), name="rmsnorm_rows",
    )(x, g.reshape(1, d).astype(F32))


def _mm_kernel(*refs, nk, has_res):
    a_ref, b_ref = refs[0], refs[1]
    r_ref = refs[2] if has_res else None
    o_ref = refs[2 + has_res]
    part = jnp.dot(a_ref[...], b_ref[...], preferred_element_type=F32)
    if nk == 1:
        if has_res:
            part = r_ref[...] + part
        o_ref[...] = part.astype(o_ref.dtype)
        return
    acc_ref = refs[3 + has_res]
    k = pl.program_id(2)

    @pl.when(k == 0)
    def _():
        acc_ref[...] = part

    @pl.when(k > 0)
    def _():
        acc_ref[...] += part

    @pl.when(k == nk - 1)
    def _():
        res = acc_ref[...]
        if has_res:
            res = r_ref[...] + res
        o_ref[...] = res.astype(o_ref.dtype)


def matmul(a, b, *, out_dtype, res=None, tm=None, tn=None, tk=None):
    m, kdim = a.shape
    n = b.shape[1]
    tm = tm or _pick(m, (1024, 512, 256, 128))
    tn = tn or _pick(n, (512, 256, 128))
    tk = tk or (kdim if kdim <= 4096 else _pick(kdim, (2048, 1024, 512)))
    nk = kdim // tk
    in_specs = [pl.BlockSpec((tm, tk), lambda i, j, k: (i, k)),
                pl.BlockSpec((tk, tn), lambda i, j, k: (k, j))]
    args = [a, b]
    if res is not None:
        in_specs.append(pl.BlockSpec((tm, tn), lambda i, j, k: (i, j)))
        args.append(res)
    scratch = [pltpu.VMEM((tm, tn), F32)] if nk > 1 else []
    return pl.pallas_call(
        functools.partial(_mm_kernel, nk=nk, has_res=res is not None),
        grid=(m // tm, n // tn, nk),
        in_specs=in_specs,
        out_specs=pl.BlockSpec((tm, tn), lambda i, j, k: (i, j)),
        out_shape=jax.ShapeDtypeStruct((m, n), out_dtype),
        scratch_shapes=scratch,
        compiler_params=_cparams(3), name="matmul",
    )(*args)


def _glu_kernel(a_ref, wg_ref, wu_ref, o_ref):
    a = a_ref[...]
    g = jnp.dot(a, wg_ref[...], preferred_element_type=F32)
    u = jnp.dot(a, wu_ref[...], preferred_element_type=F32)
    o_ref[...] = (g * jax.nn.sigmoid(g) * u).astype(o_ref.dtype)


def glu_up(a, wg, wu):
    m, kdim = a.shape
    f = wg.shape[1]
    tm = _pick(m, (1024, 512, 256, 128))
    tn = _pick(f, (512, 256, 128))
    w_spec = pl.BlockSpec((kdim, tn), lambda i, j: (0, j))
    return pl.pallas_call(
        _glu_kernel, grid=(m // tm, f // tn),
        in_specs=[pl.BlockSpec((tm, kdim), lambda i, j: (i, 0)), w_spec, w_spec],
        out_specs=pl.BlockSpec((tm, tn), lambda i, j: (i, j)),
        out_shape=jax.ShapeDtypeStruct((m, f), BF16),
        compiler_params=_cparams(2), name="glu_up",
    )(a, wg, wu)


MOE_TM = 512


def _glu_grouped_kernel(te_ref, a_ref, wg_ref, wu_ref, o_ref):
    _glu_kernel(a_ref, wg_ref, wu_ref, o_ref)


def _mm_grouped_kernel(te_ref, a_ref, w_ref, o_ref):
    o_ref[...] = jnp.dot(a_ref[...], w_ref[...], preferred_element_type=F32).astype(o_ref.dtype)


def _grouped_call(kernel, name, a, weights, tile_expert, tn, out_dtype):
    r, kdim = a.shape
    n = weights[0].shape[2]
    w_spec = pl.BlockSpec((None, kdim, tn), lambda j, i, te: (te[i], 0, j))
    grid_spec = pltpu.PrefetchScalarGridSpec(
        num_scalar_prefetch=1, grid=(n // tn, r // MOE_TM),
        in_specs=[pl.BlockSpec((MOE_TM, kdim), lambda j, i, te: (i, 0))] + [w_spec] * len(weights),
        out_specs=pl.BlockSpec((MOE_TM, tn), lambda j, i, te: (i, j)))
    return pl.pallas_call(
        kernel, grid_spec=grid_spec, out_shape=jax.ShapeDtypeStruct((r, n), out_dtype),
        compiler_params=_cparams(2), name=name,
    )(tile_expert, a, *weights)


def _row_gather(idx_ref, base, src_hbm, dst, sem, n, *, wait):
    def body(r, c):
        copy = pltpu.make_async_copy(src_hbm.at[pl.ds(idx_ref[base + r], 1)], dst.at[pl.ds(r, 1)], sem)
        if wait:
            copy.wait()
        else:
            copy.start()
        return c

    lax.fori_loop(0, n, body, 0)


def _gather_norm_kernel(src_ref, x_hbm, g_ref, o_ref, buf, sem):
    tm = o_ref.shape[0]
    base = pl.program_id(0) * tm
    _row_gather(src_ref, base, x_hbm, buf, sem, tm, wait=False)
    _row_gather(src_ref, base, x_hbm, buf, sem, tm, wait=True)
    x = buf[...]
    ms = jnp.mean(x * x, axis=-1, keepdims=True)
    o_ref[...] = (x * lax.rsqrt(ms + RMS_EPS) * g_ref[...]).astype(o_ref.dtype)


def gather_norm(x, g, src):
    d = x.shape[1]
    r = src.shape[0]
    grid_spec = pltpu.PrefetchScalarGridSpec(
        num_scalar_prefetch=1, grid=(r // MOE_TM,),
        in_specs=[pl.BlockSpec(memory_space=pl.ANY), pl.BlockSpec((1, d), lambda i, s: (0, 0))],
        out_specs=pl.BlockSpec((MOE_TM, d), lambda i, s: (i, 0)),
        scratch_shapes=[pltpu.VMEM((MOE_TM, d), F32), pltpu.SemaphoreType.DMA(())])
    return pl.pallas_call(
        _gather_norm_kernel, grid_spec=grid_spec, out_shape=jax.ShapeDtypeStruct((r, d), BF16),
        compiler_params=_cparams(1), name="gather_norm",
    )(src, x, g.reshape(1, d).astype(F32))


def _moe_combine_kernel(pos_ref, x_ref, sel_ref, y_hbm, o_ref, buf0, buf1, sems):
    tm = o_ref.shape[0]
    base0 = pl.program_id(0) * tm
    base1 = pos_ref.shape[0] // 2 + base0
    for wait in (False, True):
        _row_gather(pos_ref, base0, y_hbm, buf0, sems.at[0], tm, wait=wait)
        _row_gather(pos_ref, base1, y_hbm, buf1, sems.at[1], tm, wait=wait)
    sel = sel_ref[...]
    o_ref[...] = x_ref[...] + sel[:, 2:3] * buf0[...] + sel[:, 3:4] * buf1[...]


def moe_combine(x, sel, y, pos):
    s, d = x.shape
    tm = _pick(s, (256, 128))
    grid_spec = pltpu.PrefetchScalarGridSpec(
        num_scalar_prefetch=1, grid=(s // tm,),
        in_specs=[pl.BlockSpec((tm, d), lambda i, p: (i, 0)), pl.BlockSpec((tm, LANES), lambda i, p: (i, 0)),
                  pl.BlockSpec(memory_space=pl.ANY)],
        out_specs=pl.BlockSpec((tm, d), lambda i, p: (i, 0)),
        scratch_shapes=[pltpu.VMEM((tm, d), F32), pltpu.VMEM((tm, d), F32), pltpu.SemaphoreType.DMA((2,))])
    return pl.pallas_call(
        _moe_combine_kernel, grid_spec=grid_spec, out_shape=jax.ShapeDtypeStruct((s, d), F32),
        compiler_params=_cparams(1), name="moe_combine",
    )(pos.reshape(-1), x, sel, y)


def _moe_plan(sel):
    s = sel.shape[0]
    e = sel[:, :MOE_TOP_K].astype(I32).T.reshape(-1)
    onehot = (e[:, None] == jnp.arange(N_EXPERTS, dtype=I32)[None, :]).astype(I32)
    rank = jnp.take_along_axis(jnp.cumsum(onehot, axis=0), e[:, None], axis=1)[:, 0] - 1
    padded = (jnp.sum(onehot, axis=0) + MOE_TM - 1) // MOE_TM * MOE_TM
    ends = jnp.cumsum(padded)
    pos = (ends - padded)[e] + rank
    n_rows = MOE_TOP_K * s + N_EXPERTS * MOE_TM
    token = jnp.tile(jnp.arange(s, dtype=I32), MOE_TOP_K)
    src = jnp.zeros((n_rows,), I32).at[pos].set(token)
    tile_start = jnp.arange(n_rows // MOE_TM, dtype=I32) * MOE_TM
    tile_expert = jnp.minimum(jnp.searchsorted(ends, tile_start, side='right'), N_EXPERTS - 1).astype(I32)
    return src, pos.reshape(MOE_TOP_K, s).astype(I32), tile_expert


def _router_kernel(x_ref, g_ref, w_ref, o_ref):
    x = x_ref[...]
    ms = jnp.mean(x * x, axis=-1, keepdims=True)
    h = x * lax.rsqrt(ms + RMS_EPS) * g_ref[...]
    logits = jnp.dot(h, w_ref[...], preferred_element_type=F32, precision=lax.Precision.HIGHEST)
    lane = lax.broadcasted_iota(I32, logits.shape, 1).astype(F32)
    logits = jnp.where(lane < N_EXPERTS, logits, -jnp.inf)
    m1 = jnp.max(logits, axis=1, keepdims=True)
    i1 = jnp.min(jnp.where(logits == m1, lane, float(LANES)), axis=1, keepdims=True)
    rest = jnp.where(lane == i1, -jnp.inf, logits)
    m2 = jnp.max(rest, axis=1, keepdims=True)
    i2 = jnp.min(jnp.where(rest == m2, lane, float(LANES)), axis=1, keepdims=True)
    e2 = jnp.exp(m2 - m1)
    w1 = 1.0 / (1.0 + e2)
    w2 = e2 / (1.0 + e2)
    o_ref[...] = (jnp.where(lane == 0.0, i1, 0.0) + jnp.where(lane == 1.0, i2, 0.0)
                  + jnp.where(lane == 2.0, w1, 0.0) + jnp.where(lane == 3.0, w2, 0.0))


def router_top2(x, g, w_router):
    m, d = x.shape
    tm = _pick(m, (256, 128))
    w = jnp.zeros((d, LANES), F32).at[:, :N_EXPERTS].set(w_router.astype(F32))
    return pl.pallas_call(
        _router_kernel, grid=(m // tm,),
        in_specs=[pl.BlockSpec((tm, d), lambda i: (i, 0)), pl.BlockSpec((1, d), lambda i: (0, 0)),
                  pl.BlockSpec((d, LANES), lambda i: (0, 0))],
        out_specs=pl.BlockSpec((tm, LANES), lambda i: (i, 0)),
        out_shape=jax.ShapeDtypeStruct((m, LANES), F32),
        compiler_params=_cparams(1), name="router_top2",
    )(x, g.reshape(1, d).astype(F32), w)


def _rope(x, tab_ref, half):
    w = x.shape[-1]
    return (x * tab_ref[0] + pltpu.roll(x, w - half, 1) * tab_ref[1]
            + pltpu.roll(x, half, 1) * tab_ref[2])


def _rope_tables(seq, rot_dim, period):
    half = rot_dim // 2
    inv_freq = ROPE_THETA ** (-jnp.arange(0, rot_dim, 2, dtype=F32) / rot_dim)
    ang = jnp.arange(seq, dtype=F32)[:, None] * inv_freq[None, :]
    cos, sin = jnp.cos(ang), jnp.sin(ang)
    ones = jnp.ones((seq, period - rot_dim), F32)
    zeros = jnp.zeros((seq, period - rot_dim), F32)
    zh = jnp.zeros((seq, half), F32)
    c = jnp.concatenate([cos, cos, ones], axis=1)
    sa = jnp.concatenate([-sin, zh, zeros], axis=1)
    sb = jnp.concatenate([zh, sin, zeros], axis=1)
    rep = LANES // period
    return jnp.stack([jnp.tile(c, (1, rep)), jnp.tile(sa, (1, rep)), jnp.tile(sb, (1, rep))], axis=0)


def _tab_spec(tm):
    return pl.BlockSpec((3, tm, LANES), lambda i: (0, i, 0))


def _row_spec(tm, w):
    return pl.BlockSpec((tm, w), lambda i: (i, 0))


def _const_spec(w):
    return pl.BlockSpec((1, w), lambda i: (0, 0))


def _prep_latent_kernel(p_ref, gq_ref, gkv_ref, cq_ref, ckv_ref, kpe_ref):
    cq = p_ref[:, :MLA_Q_LORA]
    ms = jnp.mean(cq * cq, axis=-1, keepdims=True)
    cq_ref[...] = (cq * lax.rsqrt(ms + RMS_EPS) * gq_ref[...]).astype(cq_ref.dtype)
    ckv = p_ref[:, MLA_Q_LORA:MLA_Q_LORA + MLA_KV_LORA]
    ms = jnp.mean(ckv * ckv, axis=-1, keepdims=True)
    ckv_ref[...] = (ckv * lax.rsqrt(ms + RMS_EPS) * gkv_ref[...]).astype(ckv_ref.dtype)
    kpe_ref[...] = p_ref[:, MLA_Q_LORA + MLA_KV_LORA:]


def prep_latent(p, g_cq, g_ckv):
    s = p.shape[0]
    tm = _pick(s, (512, 256, 128))
    return pl.pallas_call(
        _prep_latent_kernel, grid=(s // tm,),
        in_specs=[_row_spec(tm, p.shape[1]), _const_spec(MLA_Q_LORA), _const_spec(MLA_KV_LORA)],
        out_specs=[_row_spec(tm, MLA_Q_LORA), _row_spec(tm, MLA_KV_LORA), _row_spec(tm, LANES)],
        out_shape=[jax.ShapeDtypeStruct((s, MLA_Q_LORA), BF16),
                   jax.ShapeDtypeStruct((s, MLA_KV_LORA), BF16),
                   jax.ShapeDtypeStruct((s, LANES), F32)],
        compiler_params=_cparams(1), name="prep_latent",
    )(p, g_cq.reshape(1, -1), g_ckv.reshape(1, -1))


def _prep_mla_kernel(qup_ref, kvup_ref, kpe_ref, gq_ref, gk_ref, tab_ref, qT_ref, k_ref, vT_ref):
    half = MLA_ROPE // 2
    gqn, gqr = gq_ref[:, :LANES], gq_ref[:, LANES:]
    gkn, gkr = gk_ref[:, :LANES], gk_ref[:, LANES:]
    kpe = kpe_ref[...]
    kpe_ss = jnp.sum(kpe * kpe, axis=-1, keepdims=True)
    kr_base = _rope(kpe * gkr, tab_ref, half)
    scale = LOG2E * MLA_QK ** -0.5
    for h in range(N_HEADS):
        qn = qup_ref[:, h * MLA_PAD:h * MLA_PAD + LANES]
        qr = qup_ref[:, h * MLA_PAD + LANES:(h + 1) * MLA_PAD]
        ss = jnp.sum(qn * qn, axis=-1, keepdims=True) + jnp.sum(qr * qr, axis=-1, keepdims=True)
        r = lax.rsqrt(ss * (1.0 / MLA_QK) + RMS_EPS) * scale
        qT_ref[h, :LANES, :] = (qn * r * gqn).T.astype(BF16)
        qT_ref[h, LANES:, :] = _rope(qr * r * gqr, tab_ref, half).T.astype(BF16)
        kn = kvup_ref[:, h * LANES:(h + 1) * LANES]
        ss = jnp.sum(kn * kn, axis=-1, keepdims=True) + kpe_ss
        r = lax.rsqrt(ss * (1.0 / MLA_QK) + RMS_EPS)
        k_ref[:, h * MLA_PAD:h * MLA_PAD + LANES] = (kn * r * gkn).astype(BF16)
        k_ref[:, h * MLA_PAD + LANES:(h + 1) * MLA_PAD] = (kr_base * r).astype(BF16)
        vT_ref[h] = kvup_ref[:, BRANCH_WIDTH + h * LANES:BRANCH_WIDTH + (h + 1) * LANES].T.astype(BF16)


def _qT_spec(d, t):
    return pl.BlockSpec((N_HEADS, d, t), lambda i: (0, 0, i))


def _vT_spec(t):
    return pl.BlockSpec((N_HEADS, None, HEAD_DIM, t), lambda i: (0, i, 0, 0))


def _qkv_shapes(s, dq, wk, t):
    return [jax.ShapeDtypeStruct((N_HEADS, dq, s), BF16), jax.ShapeDtypeStruct((s, wk), BF16),
            jax.ShapeDtypeStruct((N_HEADS, s // t, HEAD_DIM, t), BF16)]


def prep_mla(qup, kvup, kpe, g_q, g_k, tab):
    s = qup.shape[0]
    t = ATT_T
    pad = lambda g: jnp.zeros((1, MLA_PAD), F32).at[0, :MLA_QK].set(g)
    wq = N_HEADS * MLA_PAD
    return pl.pallas_call(
        _prep_mla_kernel, grid=(s // t,),
        in_specs=[_row_spec(t, wq), _row_spec(t, 2 * BRANCH_WIDTH), _row_spec(t, LANES),
                  _const_spec(MLA_PAD), _const_spec(MLA_PAD), _tab_spec(t)],
        out_specs=[_qT_spec(MLA_PAD, t), _row_spec(t, wq), _vT_spec(t)],
        out_shape=_qkv_shapes(s, MLA_PAD, wq, t),
        compiler_params=_cparams(1), name="prep_mla",
    )(qup, kvup, kpe, pad(g_q), pad(g_k), tab)


def _head_norm_rope(x, g, tab_ref, scale):
    ms = jnp.mean(x * x, axis=-1, keepdims=True)
    y = x * (lax.rsqrt(ms + RMS_EPS) * scale) * g
    return _rope(y, tab_ref, ROT_DIM // 2)


def _prep_sb_kernel(p_ref, qT_ref, k_ref, vT_ref, kn_ref):
    for h in range(N_HEADS):
        sl = slice(h * LANES, (h + 1) * LANES)
        qT_ref[h] = (p_ref[:, sl] * HEAD_DIM ** -0.5).T.astype(BF16)
        vT_ref[h] = p_ref[:, 2 * BRANCH_WIDTH + h * LANES:2 * BRANCH_WIDTH + (h + 1) * LANES].T.astype(BF16)
        kb = p_ref[:, BRANCH_WIDTH + h * LANES:BRANCH_WIDTH + (h + 1) * LANES].astype(BF16)
        k_ref[:, sl] = kb
        kf = kb.astype(F32)
        kn_ref[h] = jnp.zeros((8, LANES), F32) + jnp.max(jnp.sum(kf * kf, axis=1, keepdims=True))


def prep_sb(p):
    s = p.shape[0]
    t = ATT_T
    return pl.pallas_call(
        _prep_sb_kernel, grid=(s // t,),
        in_specs=[_row_spec(t, 3 * BRANCH_WIDTH)],
        out_specs=[_qT_spec(HEAD_DIM, t), _row_spec(t, BRANCH_WIDTH), _vT_spec(t),
                   pl.BlockSpec((N_HEADS, None, 8, LANES), lambda i: (0, i, 0, 0))],
        out_shape=_qkv_shapes(s, HEAD_DIM, BRANCH_WIDTH, t)
        + [jax.ShapeDtypeStruct((N_HEADS, s // t, 8, LANES), F32)],
        compiler_params=_cparams(1), name="prep_sb",
    )(p)


def _prep_dil_kernel(p_ref, gq_ref, gk_ref, tab_ref, qT_ref, k_ref, vT_ref):
    gq, gk = gq_ref[...], gk_ref[...]
    for h in range(N_HEADS):
        sl = slice(h * LANES, (h + 1) * LANES)
        qT_ref[h] = _head_norm_rope(p_ref[:, sl], gq, tab_ref, LOG2E * HEAD_DIM ** -0.5).T.astype(BF16)
        ksl = slice(BRANCH_WIDTH + h * LANES, BRANCH_WIDTH + (h + 1) * LANES)
        k_ref[:, sl] = _head_norm_rope(p_ref[:, ksl], gk, tab_ref, 1.0).astype(BF16)
        vT_ref[h] = p_ref[:, 2 * BRANCH_WIDTH + h * LANES:2 * BRANCH_WIDTH + (h + 1) * LANES].T.astype(BF16)


def prep_dil(p, g_q, g_k, tab):
    s = p.shape[0]
    t = ATT_T
    return pl.pallas_call(
        _prep_dil_kernel, grid=(s // t,),
        in_specs=[_row_spec(t, 3 * BRANCH_WIDTH), _const_spec(LANES), _const_spec(LANES), _tab_spec(t)],
        out_specs=[_qT_spec(HEAD_DIM, t), _row_spec(t, BRANCH_WIDTH), _vT_spec(t)],
        out_shape=_qkv_shapes(s, HEAD_DIM, BRANCH_WIDTH, t),
        compiler_params=_cparams(1), name="prep_dil",
    )(p, g_q.reshape(1, -1), g_k.reshape(1, -1), tab)


def _prep_dsa_kernel(p_ref, gq_ref, gk_ref, tab_ref, qT_ref, k_ref, vT_ref):
    gq, gk = gq_ref[...], gk_ref[...]
    for h in range(N_HEADS):
        sl = slice(h * LANES, (h + 1) * LANES)
        qT = _head_norm_rope(p_ref[:, sl], gq, tab_ref, LOG2E * HEAD_DIM ** -0.5).T.astype(BF16)
        for b in range(p_ref.shape[0] // DSA_TQ):
            qT_ref[b, :, sl] = qT[:, b * DSA_TQ:(b + 1) * DSA_TQ]
    k_ref[...] = _head_norm_rope(p_ref[:, BRANCH_WIDTH:BRANCH_WIDTH + LANES], gk, tab_ref, 1.0).astype(BF16)
    vT_ref[...] = p_ref[:, BRANCH_WIDTH + LANES:].T.astype(BF16)


def prep_dsa(p, g_q, g_k, tab):
    s = p.shape[0]
    t = ATT_T
    nqb = t // DSA_TQ
    return pl.pallas_call(
        _prep_dsa_kernel, grid=(s // t,),
        in_specs=[_row_spec(t, BRANCH_WIDTH + 2 * LANES), _const_spec(LANES), _const_spec(LANES), _tab_spec(t)],
        out_specs=[pl.BlockSpec((nqb, HEAD_DIM, BRANCH_WIDTH), lambda i: (i, 0, 0)), _row_spec(t, LANES),
                   pl.BlockSpec((None, HEAD_DIM, t), lambda i: (i, 0, 0))],
        out_shape=[jax.ShapeDtypeStruct((s // DSA_TQ, HEAD_DIM, BRANCH_WIDTH), BF16),
                   jax.ShapeDtypeStruct((s, LANES), BF16),
                   jax.ShapeDtypeStruct((s // t, HEAD_DIM, t), BF16)],
        compiler_params=_cparams(1), name="prep_dsa",
    )(p, g_q.reshape(1, -1), g_k.reshape(1, -1), tab)


def _prep_idx_kernel(p_ref, tab_ref, qiT_ref, ki_ref, wiT_ref):
    half = IDX_ROT // 2
    t = p_ref.shape[0]
    nqb = t // DSA_TQ
    lane = lax.broadcasted_iota(I32, (t, LANES), 1)
    first = lane < IDX_DIM
    zero = jnp.zeros((t, LANES), F32)
    for b in range(IDX_HEADS // 2):
        x = _rope(p_ref[:, b * LANES:(b + 1) * LANES], tab_ref, half) * (IDX_DIM ** -0.5)
        hi = x.astype(BF16).astype(F32)
        lo = x - hi
        rhi = pltpu.roll(hi, IDX_DIM, 1)
        for hh, (a0, a1) in enumerate(((jnp.where(first, hi, pltpu.roll(lo, IDX_DIM, 1)), jnp.where(first, hi, zero)),
                                       (jnp.where(first, rhi, lo), jnp.where(first, rhi, zero)))):
            h = 2 * b + hh
            a0T, a1T = a0.T.astype(BF16), a1.T.astype(BF16)
            for qb in range(nqb):
                qs = slice(qb * DSA_TQ, (qb + 1) * DSA_TQ)
                qiT_ref[qb, :LANES, h * DSA_TQ:(h + 1) * DSA_TQ] = a0T[:, qs]
                qiT_ref[qb, LANES:, h * DSA_TQ:(h + 1) * DSA_TQ] = a1T[:, qs]
    kx = _rope(p_ref[:, IDX_HEADS * IDX_DIM:IDX_HEADS * IDX_DIM + LANES], tab_ref, half)
    hi = kx.astype(BF16).astype(F32)
    lo = kx - hi
    ki_ref[:, :LANES] = jnp.where(first, hi, pltpu.roll(hi, IDX_DIM, 1)).astype(BF16)
    ki_ref[:, LANES:] = jnp.where(first, lo, zero).astype(BF16)
    wT = (p_ref[:, IDX_HEADS * IDX_DIM + LANES:] * (IDX_HEADS ** -0.5)).T
    for qb in range(nqb):
        wiT_ref[qb] = wT[:IDX_HEADS, qb * DSA_TQ:(qb + 1) * DSA_TQ]


def prep_idx(p, tab):
    s = p.shape[0]
    t = ATT_T
    nqb = t // DSA_TQ
    return pl.pallas_call(
        _prep_idx_kernel, grid=(s // t,),
        in_specs=[_row_spec(t, IDX_HEADS * IDX_DIM + 2 * LANES), _tab_spec(t)],
        out_specs=[pl.BlockSpec((nqb, 2 * LANES, IDX_HEADS * DSA_TQ), lambda i: (i, 0, 0)),
                   _row_spec(t, 2 * LANES),
                   pl.BlockSpec((nqb, IDX_HEADS, DSA_TQ), lambda i: (i, 0, 0))],
        out_shape=[jax.ShapeDtypeStruct((s // DSA_TQ, 2 * LANES, IDX_HEADS * DSA_TQ), BF16),
                   jax.ShapeDtypeStruct((s, 2 * LANES), BF16),
                   jax.ShapeDtypeStruct((s // DSA_TQ, IDX_HEADS, DSA_TQ), F32)],
        compiler_params=_cparams(1), name="prep_idx",
    )(p, tab)


def _softmax_step(sT, vT, m_sc, l_sc, acc_sc):
    m_prev = m_sc[...]
    m_new = jnp.maximum(m_prev, jnp.max(sT, axis=0, keepdims=True))
    alpha = jnp.exp2(m_prev - m_new)
    p = jnp.exp2(sT - m_new)
    l_sc[...] = alpha * l_sc[...] + jnp.sum(p, axis=0, keepdims=True)
    acc_sc[...] = alpha * acc_sc[...] + jnp.dot(vT, p.astype(BF16), preferred_element_type=F32)
    m_sc[...] = m_new


def _softmax_init(m_sc, l_sc, acc_sc):
    m_sc[...] = jnp.full(m_sc.shape, NEG, F32)
    l_sc[...] = jnp.zeros(l_sc.shape, F32)
    acc_sc[...] = jnp.zeros(acc_sc.shape, F32)


def _pipelined_blocks(n_full, scores, consume, consume_last, s_sc):
    s_sc[0] = scores(0)

    def pair(jj, c):
        s_sc[1] = scores(2 * jj + 1)
        consume(s_sc[0], 2 * jj)
        s_sc[0] = scores(2 * jj + 2)
        consume(s_sc[1], 2 * jj + 1)
        return c

    lax.fori_loop(0, n_full // 2, pair, 0)

    @pl.when(n_full % 2 == 1)
    def _():
        s_sc[1] = scores(n_full)
        consume(s_sc[0], n_full - 1)
        consume_last(s_sc[1], n_full)

    @pl.when(n_full % 2 == 0)
    def _():
        consume_last(s_sc[0], n_full)


def _key_le_query(tk, tq):
    return lax.broadcasted_iota(I32, (tk, tq), 0) <= lax.broadcasted_iota(I32, (tk, tq), 1)


def _mla_kernel(qT_ref, k_ref, vT_ref, o_ref, s_sc, m_sc, l_sc, acc_sc):
    i = pl.program_id(1)
    t = qT_ref.shape[1]
    qT = qT_ref[...]
    _softmax_init(m_sc, l_sc, acc_sc)
    scores = lambda j: jnp.dot(k_ref[j], qT, preferred_element_type=F32)
    step = lambda sT, j: _softmax_step(sT, vT_ref[j], m_sc, l_sc, acc_sc)
    last = lambda sT, j: _softmax_step(jnp.where(_key_le_query(t, t), sT, NEG), vT_ref[j], m_sc, l_sc, acc_sc)
    _pipelined_blocks(i, scores, step, last, s_sc)
    o_ref[...] = (acc_sc[...] / l_sc[...]).T.astype(o_ref.dtype)


def _attn_scratch(tk, tq):
    return [pltpu.VMEM((2, tk, tq), F32), pltpu.VMEM((1, tq), F32), pltpu.VMEM((1, tq), F32),
            pltpu.VMEM((HEAD_DIM, tq), F32)]


def _head_qT_spec(d, t):
    return pl.BlockSpec((None, d, t), lambda h, i: (h, 0, i))


def _head_k_spec(nb, t, w):
    return pl.BlockSpec((nb, t, w), lambda h, i: (0, 0, h))


def _head_vT_spec(nb, t):
    return pl.BlockSpec((None, nb, HEAD_DIM, t), lambda h, i: (h, 0, 0, 0))


def _head_out_spec(t):
    return pl.BlockSpec((t, HEAD_DIM), lambda h, i: (i, h))


def mla_attention(qT, k, vT):
    s = k.shape[0]
    t = ATT_T
    nb = s // t
    return pl.pallas_call(
        _mla_kernel, grid=(N_HEADS, nb),
        in_specs=[_head_qT_spec(MLA_PAD, t), _head_k_spec(nb, t, MLA_PAD), _head_vT_spec(nb, t)],
        out_specs=_head_out_spec(t),
        out_shape=jax.ShapeDtypeStruct((s, BRANCH_WIDTH), BF16),
        scratch_shapes=_attn_scratch(t, t),
        compiler_params=_cparams(2), name="mla_attention",
    )(qT, k.reshape(nb, t, -1), vT)


SB_SUB = 128


def _sb_kernel(qT_ref, k_ref, vT_ref, u_ref, kn_ref, o_ref, carry_sc, acc_sc):
    i = pl.program_id(1)
    t = qT_ref.shape[1]
    qT = qT_ref[...]
    u2 = u_ref[...]
    qf = qT.astype(F32)
    zbound = jnp.sqrt(jnp.sum(qf * qf, axis=0, keepdims=True) * jnp.max(kn_ref[...]))
    carry_sc[...] = jnp.zeros(carry_sc.shape, F32)
    acc_sc[...] = jnp.zeros(acc_sc.shape, F32)
    key = lax.broadcasted_iota(I32, (SB_SUB, t), 0)
    qry = lax.broadcasted_iota(I32, (SB_SUB, t), 1)

    def block(j, diag):
        zT = jnp.dot(k_ref[j], qT, preferred_element_type=F32)
        carry = carry_sc[...]
        parts = [None] * (t // SB_SUB)
        for c in reversed(range(t // SB_SUB)):
            zc = zT[c * SB_SUB:(c + 1) * SB_SUB]
            ls = jnp.minimum(-zc, 0.0) - jnp.log(1.0 + jnp.exp(-jnp.abs(zc)))
            if diag:
                past = key + c * SB_SUB < qry
                ls = jnp.where(past, ls, 0.0)
            hi = ls.astype(BF16)
            lo = (ls - hi.astype(F32)).astype(BF16)
            rev = jnp.dot(u2, jnp.concatenate([hi, lo], axis=0), preferred_element_type=F32) + carry
            a = jnp.exp(jnp.minimum(zc + rev, 0.0))
            if diag:
                a = jnp.where(past, a, 0.0)
            parts[c] = a.astype(BF16)
            carry = rev[0:1, :]
        carry_sc[...] = carry
        acc_sc[...] += jnp.dot(vT_ref[j], jnp.concatenate(parts, axis=0), preferred_element_type=F32)

    def all_underflow():
        return (jnp.max(carry_sc[...] + zbound) < SB_EXIT).astype(I32)

    block(i, True)

    def back(state):
        jj, _ = state
        block(i - 1 - jj, False)
        return jj + 1, all_underflow()

    lax.while_loop(lambda st: jnp.logical_and(st[0] < i, st[1] == 0), back, (jnp.int32(0), all_underflow()))
    o_ref[...] = acc_sc[...].T.astype(o_ref.dtype)


def sb_attention(qT, k, vT, kn):
    s = k.shape[0]
    t = ATT_T
    nb = s // t
    tri = (np.arange(SB_SUB)[None, :] >= np.arange(SB_SUB)[:, None]).astype(np.float32)
    u2 = jnp.asarray(np.concatenate([tri, tri], axis=1), dtype=BF16)
    return pl.pallas_call(
        _sb_kernel, grid=(N_HEADS, nb),
        in_specs=[_head_qT_spec(HEAD_DIM, t), _head_k_spec(nb, t, HEAD_DIM), _head_vT_spec(nb, t),
                  pl.BlockSpec((SB_SUB, 2 * SB_SUB), lambda h, i: (0, 0)),
                  pl.BlockSpec((None, nb, 8, LANES), lambda h, i: (h, 0, 0, 0))],
        out_specs=_head_out_spec(t),
        out_shape=jax.ShapeDtypeStruct((s, BRANCH_WIDTH), BF16),
        scratch_shapes=[pltpu.VMEM((1, t), F32), pltpu.VMEM((HEAD_DIM, t), F32)],
        compiler_params=_cparams(2), name="sb_attention",
    )(qT, k.reshape(nb, t, -1), vT, u2, kn)


def _dil_log_weights(t):
    span = max(w for w, _ in DIL_PATTERNS)
    nback = -(-span // t)
    d = np.arange(nback + 1)[:, None, None] * t + np.arange(t)[None, None, :] - np.arange(t)[None, :, None]
    mult = np.zeros(d.shape, np.float64)
    for window, dil in DIL_PATTERNS:
        mult += ((d >= 0) & (d <= window) & (d % dil == 0))
    return np.where(mult > 0, np.log2(np.maximum(mult, 1.0)), NEG).astype(np.float32)


def _dil_kernel(qT_ref, k_ref, vT_ref, w_ref, o_ref, s_sc, m_sc, l_sc, acc_sc):
    i = pl.program_id(1)
    nback = w_ref.shape[0] - 1
    qT = qT_ref[...]
    _softmax_init(m_sc, l_sc, acc_sc)
    scores = lambda d: jnp.dot(k_ref[i - d], qT, preferred_element_type=F32)
    step = lambda sT, d: _softmax_step(sT + w_ref[d], vT_ref[i - d], m_sc, l_sc, acc_sc)
    _pipelined_blocks(jnp.minimum(i, nback), scores, step, step, s_sc)
    o_ref[...] = (acc_sc[...] / l_sc[...]).T.astype(o_ref.dtype)


def dil_attention(qT, k, vT):
    s = k.shape[0]
    t = ATT_T
    nb = s // t
    logw = jnp.asarray(_dil_log_weights(t))
    return pl.pallas_call(
        _dil_kernel, grid=(N_HEADS, nb),
        in_specs=[_head_qT_spec(HEAD_DIM, t), _head_k_spec(nb, t, HEAD_DIM), _head_vT_spec(nb, t),
                  pl.BlockSpec(logw.shape, lambda h, i: (0, 0, 0))],
        out_specs=_head_out_spec(t),
        out_shape=jax.ShapeDtypeStruct((s, BRANCH_WIDTH), BF16),
        scratch_shapes=_attn_scratch(t, t),
        compiler_params=_cparams(2), name="dil_attention",
    )(qT, k.reshape(nb, t, -1), vT, logw)


def _dsa_kernel(qsT_ref, qiT_ref, wiT_ref, ki_ref, k_ref, vT_ref, o_ref,
                key_sc, q1_sc, s_sc, m_sc, l_sc, acc_sc, *, top_k, pos_bits):
    i = pl.program_id(0)
    tq, tk = DSA_TQ, ATT_T
    nkb = (i * tq + tq + tk - 1) // tk
    qiT = qiT_ref[...]
    wiT = wiT_ref[...]
    row = lax.broadcasted_iota(I32, (tk, tq), 1) + i * tq
    col = lax.broadcasted_iota(I32, (tk, tq), 0)

    def idx_block(aT, j):
        idx = jnp.zeros((tk, tq), F32)
        for h in range(IDX_HEADS):
            idx = idx + jnp.maximum(aT[:, h * tq:(h + 1) * tq], 0.0) * wiT[h:h + 1, :]
        bits = pltpu.bitcast(idx + 0.0, I32)
        key = bits ^ ((bits >> 31) & 0x7FFFFFFF)
        key_sc[j] = jnp.where(col + j * tk <= row, key, INT_MIN)

    _pipelined_blocks(nkb - 1, lambda j: jnp.dot(ki_ref[j], qiT, preferred_element_type=F32),
                      idx_block, idx_block, s_sc)

    def count(hits):
        def body(j, acc):
            hit = hits(key_sc[j], j)
            return acc + jnp.sum(hit.reshape(tk // COUNT_ROWS, COUNT_ROWS, tq), axis=0)
        acc = lax.fori_loop(0, nkb, body, jnp.zeros((COUNT_ROWS, tq), F32))
        return jnp.sum(acc, axis=0, keepdims=True)

    def bit_body(b, carry):
        thr, cge = carry
        cand = thr + jnp.left_shift(jnp.int32(1), 31 - b)
        c = count(lambda kb, j: jnp.where(kb >= cand, 1.0, 0.0))
        ok = c >= top_k
        return jnp.where(ok, cand, thr), jnp.where(ok, c, cge)

    thr0 = jnp.full((1, tq), INT_MIN, I32)
    cge0 = jnp.zeros((1, tq), F32) + (nkb * tk).astype(F32)
    thr, cge = lax.fori_loop(0, 32, bit_body, (thr0, cge0))

    q1_sc[...] = jnp.full((1, tq), 2 ** 30, I32)

    @pl.when(jnp.max(cge) > top_k)
    def _():
        need = top_k - count(lambda kb, j: jnp.where(kb > thr, 1.0, 0.0))

        def pos_body(b, qpos):
            cand = qpos + jnp.left_shift(jnp.int32(1), pos_bits - 1 - b)
            g = count(lambda kb, j: jnp.where(kb == thr, jnp.where(col + j * tk < cand, 1.0, 0.0), 0.0))
            return jnp.where(g < need, cand, qpos)

        q1_sc[...] = lax.fori_loop(0, pos_bits, pos_body, jnp.zeros((1, tq), I32)) + 1

    q1 = q1_sc[...]
    qsT = qsT_ref[...]
    _softmax_init(m_sc, l_sc, acc_sc)
    def att_block(sT, j):
        kb = key_sc[j]
        pos = col + j * tk
        bias = jnp.where(kb > thr, 0.0, jnp.where(kb == thr, jnp.where(pos < q1, 0.0, NEG), NEG))
        bias = jnp.where(pos <= row, bias, NEG)
        _softmax_step(sT + jnp.concatenate([bias] * N_HEADS, axis=1), vT_ref[j], m_sc, l_sc, acc_sc)

    _pipelined_blocks(nkb - 1, lambda j: jnp.dot(k_ref[j], qsT, preferred_element_type=F32),
                      att_block, att_block, s_sc)
    o = acc_sc[...] / l_sc[...]
    for h in range(N_HEADS):
        sl = slice(h * HEAD_DIM, (h + 1) * HEAD_DIM)
        o_ref[:, sl] = o[:, h * tq:(h + 1) * tq].T.astype(o_ref.dtype)


def dsa_attention(qsT, qiT, wiT, ki, k, vT):
    s = k.shape[0]
    tq, tk = DSA_TQ, ATT_T
    nkb = s // tk
    top_k = min(DSA_TOPK, s // 4)
    wq = N_HEADS * tq
    full3 = lambda a, b: pl.BlockSpec((nkb, a, b), lambda i: (0, 0, 0))
    return pl.pallas_call(
        functools.partial(_dsa_kernel, top_k=top_k, pos_bits=(s - 1).bit_length()),
        grid=(s // tq,),
        in_specs=[pl.BlockSpec((None, HEAD_DIM, wq), lambda i: (i, 0, 0)),
                  pl.BlockSpec((None, 2 * LANES, wq), lambda i: (i, 0, 0)),
                  pl.BlockSpec((None, IDX_HEADS, tq), lambda i: (i, 0, 0)),
                  full3(tk, 2 * LANES), full3(tk, HEAD_DIM), full3(HEAD_DIM, tk)],
        out_specs=pl.BlockSpec((tq, BRANCH_WIDTH), lambda i: (i, 0)),
        out_shape=jax.ShapeDtypeStruct((s, BRANCH_WIDTH), BF16),
        scratch_shapes=[pltpu.VMEM((nkb, tk, tq), I32), pltpu.VMEM((1, tq), I32)] + _attn_scratch(tk, wq),
        compiler_params=_cparams(1), name="dsa_attention",
    )(qsT, qiT, wiT, ki.reshape(nkb, tk, -1), k.reshape(nkb, tk, -1), vT)


def _merge_kernel(g_ref, o0_ref, o1_ref, o2_ref, o3_ref, wgb_ref, b_ref, wbr_ref, out_ref):
    g = g_ref[...]
    acc = None
    for n, o_ref in enumerate((o0_ref, o1_ref, o2_ref, o3_ref)):
        gate = jax.nn.sigmoid(jnp.dot(g, wgb_ref[n], preferred_element_type=F32) + b_ref[n])
        val = gate * jnp.dot(o_ref[...], wbr_ref[n], preferred_element_type=F32)
        acc = val if acc is None else acc + val
    out_ref[...] = acc.astype(out_ref.dtype)


def gated_merge(g_lat, outs, w_gate_b, b_gate, w_branch):
    s = g_lat.shape[0]
    d = w_gate_b.shape[-1]
    tm = _pick(s, (1024, 512, 256, 128))
    tn = _pick(d, (512, 256, 128))
    o_spec = pl.BlockSpec((tm, BRANCH_WIDTH), lambda i, j: (i, 0))
    return pl.pallas_call(
        _merge_kernel, grid=(s // tm, d // tn),
        in_specs=[pl.BlockSpec((tm, GATE_RANK), lambda i, j: (i, 0)), o_spec, o_spec, o_spec, o_spec,
                  pl.BlockSpec((N_BRANCHES, GATE_RANK, tn), lambda i, j: (0, 0, j)),
                  pl.BlockSpec((N_BRANCHES, 1, tn), lambda i, j: (0, 0, j)),
                  pl.BlockSpec((N_BRANCHES, BRANCH_WIDTH, tn), lambda i, j: (0, 0, j))],
        out_specs=pl.BlockSpec((tm, tn), lambda i, j: (i, j)),
        out_shape=jax.ShapeDtypeStruct((s, d), BF16),
        compiler_params=_cparams(2), name="gated_merge",
    )(g_lat, *outs, w_gate_b, b_gate.reshape(N_BRANCHES, 1, d), w_branch)


def _cross_kernel(x_ref, g_ref, wq_ref, gq_ref, kraw_ref, gk_ref, v_ref, wo_ref, o_ref):
    x = x_ref[...]
    ms = jnp.mean(x * x, axis=-1, keepdims=True)
    h = (x * lax.rsqrt(ms + RMS_EPS) * g_ref[...]).astype(BF16)
    q = jnp.dot(h, wq_ref[...], preferred_element_type=F32)
    outs = []
    for hd in range(MEM_HEADS):
        sl = slice(hd * MEM_HEAD_DIM, (hd + 1) * MEM_HEAD_DIM)
        qh = q[:, sl]
        qh = qh * (lax.rsqrt(jnp.mean(qh * qh, axis=-1, keepdims=True) + RMS_EPS) * MEM_HEAD_DIM ** -0.5) * gq_ref[...]
        kh = kraw_ref[:, sl]
        kh = kh * lax.rsqrt(jnp.mean(kh * kh, axis=-1, keepdims=True) + RMS_EPS) * gk_ref[...]
        s = lax.dot_general(qh.astype(BF16), kh.astype(BF16), NT_DIMS, preferred_element_type=F32)
        p = jnp.exp(s - jnp.max(s, axis=1, keepdims=True))
        p = p / jnp.sum(p, axis=1, keepdims=True)
        outs.append(jnp.dot(p.astype(BF16), v_ref[:, sl], preferred_element_type=F32).astype(BF16))
    o = jnp.concatenate(outs, axis=1)
    o_ref[...] = x + jnp.dot(o, wo_ref[...], preferred_element_type=F32)


def cross_attention(x, g, w_xq, g_q, k_raw, g_k, v, w_xo):
    s, d = x.shape
    m = k_raw.shape[0]
    tm = _pick(s, (256, 128))
    c2 = lambda shape: pl.BlockSpec(shape, lambda i: (0, 0))
    return pl.pallas_call(
        _cross_kernel, grid=(s // tm,),
        in_specs=[_row_spec(tm, d), c2((1, d)), c2((d, MEM_WIDTH)), c2((1, MEM_HEAD_DIM)),
                  c2((m, MEM_WIDTH)), c2((1, MEM_HEAD_DIM)), c2((m, MEM_WIDTH)), c2((MEM_WIDTH, d))],
        out_specs=_row_spec(tm, d),
        out_shape=jax.ShapeDtypeStruct((s, d), F32),
        compiler_params=_cparams(1), name="cross_attention",
    )(x, g.reshape(1, d), w_xq, g_q.reshape(1, -1), k_raw, g_k.reshape(1, -1), v, w_xo)


def _split_w_in(w_in):
    cuts = np.cumsum([0, MLA_Q_LORA, MLA_KV_LORA, MLA_ROPE, BRANCH_WIDTH, BRANCH_WIDTH, BRANCH_WIDTH,
                      BRANCH_WIDTH, BRANCH_WIDTH, BRANCH_WIDTH, BRANCH_WIDTH, HEAD_DIM, HEAD_DIM,
                      IDX_HEADS * IDX_DIM, IDX_DIM, IDX_HEADS])
    seg = lambda a, b: w_in[:, cuts[a]:cuts[b]]
    zeros = lambda n: jnp.zeros((w_in.shape[0], n), w_in.dtype)
    w_lat = jnp.concatenate([seg(0, 3), zeros(LANES - MLA_ROPE)], axis=1)
    w_sb = seg(3, 6)
    w_dil = seg(6, 9)
    w_dsa = seg(9, 12)
    w_idx = jnp.concatenate([seg(12, 14), zeros(LANES - IDX_DIM), seg(14, 15), zeros(LANES - IDX_HEADS)], axis=1)
    return [w.astype(BF16) for w in (w_lat, w_sb, w_dil, w_dsa, w_idx)]


def _pad_mla_up(w_uq, w_ukv):
    r = w_uq.shape[0]
    wq = w_uq.reshape(r, N_HEADS, MLA_QK)
    wq = jnp.concatenate([wq, jnp.zeros((r, N_HEADS, MLA_PAD - MLA_QK), wq.dtype)], axis=2)
    wkv = w_ukv.reshape(w_ukv.shape[0], N_HEADS, 2 * HEAD_DIM)
    wkv = jnp.concatenate([wkv[:, :, :MLA_NOPE].reshape(-1, BRANCH_WIDTH),
                           wkv[:, :, MLA_NOPE:].reshape(-1, BRANCH_WIDTH)], axis=1)
    return wq.reshape(r, N_HEADS * MLA_PAD).astype(BF16), wkv.astype(BF16)


def _branches(h, p, tabs):
    tab_p, tab_i, tab_m = tabs
    w_lat, w_sb, w_dil, w_dsa, w_idx = _split_w_in(p['w_in'])
    w_uq, w_ukv = _pad_mla_up(p['w_uq'], p['w_ukv'])

    cq, ckv, kpe = prep_latent(matmul(h, w_lat, out_dtype=F32), p['g_cq'], p['g_ckv'])
    qT, k, vT = prep_mla(matmul(cq, w_uq, out_dtype=F32), matmul(ckv, w_ukv, out_dtype=F32), kpe,
                         p['g_q_mla'], p['g_k_mla'], tab_m)
    o_mla = mla_attention(qT, k, vT)
    o_sb = sb_attention(*prep_sb(matmul(h, w_sb, out_dtype=F32)))
    o_dil = dil_attention(*prep_dil(matmul(h, w_dil, out_dtype=F32), p['g_q_dil'], p['g_k_dil'], tab_p))
    qsT, ks, vsT = prep_dsa(matmul(h, w_dsa, out_dtype=F32), p['g_q_dsa'], p['g_k_dsa'], tab_p)
    qiT, ki, wiT = prep_idx(matmul(h, w_idx, out_dtype=F32), tab_i)
    o_dsa = dsa_attention(qsT, qiT, wiT, ki, ks, vsT)
    return o_mla, o_sb, o_dil, o_dsa


def _token_mixer(x, h, p, tabs):
    o_mla, o_sb, o_dil, o_dsa = _branches(h, p, tabs)
    g_lat = matmul(h, p['w_gate_a'].astype(BF16), out_dtype=BF16)
    merged = gated_merge(g_lat, (o_mla, o_sb, o_dil, o_dsa), p['w_gate_b'].astype(BF16),
                         p['b_gate'].astype(F32), p['w_branch'].astype(BF16))
    return matmul(merged, p['w_out'].astype(BF16), out_dtype=F32, res=x)


def _cross_block(x, mem, p):
    m_n = rmsnorm_rows(mem, p['ln_mem'])
    k_raw = matmul(m_n, p['w_xk'].astype(BF16), out_dtype=F32)
    v = matmul(m_n, p['w_xv'].astype(BF16), out_dtype=BF16)
    return cross_attention(x, p['ln_xattn'], p['w_xq'].astype(BF16), p['g_q_x'], k_raw, p['g_k_x'], v,
                           p['w_xo'].astype(BF16))


def _ffn_block(x, g, wg, wu, wd):
    act = glu_up(rmsnorm_rows(x, g), wg.astype(BF16), wu.astype(BF16))
    return matmul(act, wd.astype(BF16), out_dtype=F32, res=x, tn=1024, tk=1024)


def _moe_block(x, g, w_router, wg, wu, wd):
    sel = router_top2(x, g, w_router)
    src, pos, tile_expert = _moe_plan(sel)
    hs = gather_norm(x, g, src)
    act = _grouped_call(_glu_grouped_kernel, "moe_glu_up", hs, [wg.astype(BF16), wu.astype(BF16)],
                        tile_expert, _pick(wg.shape[2], (256, 128)), BF16)
    y = _grouped_call(_mm_grouped_kernel, "moe_down", act, [wd.astype(BF16)], tile_expert,
                      _pick(wd.shape[2], (1024, 512, 256, 128)), F32)
    return moe_combine(x, sel, y, pos)


def kernel(x, mem, ln_mix, w_in, g_cq, g_ckv, w_uq, w_ukv, g_q_mla, g_k_mla, g_q_dil, g_k_dil, g_q_dsa, g_k_dsa, w_gate_a, w_gate_b, b_gate, w_branch, w_out, ln_xattn, ln_mem, w_xq, w_xk, w_xv, g_q_x, g_k_x, w_xo, ln_ffn, w_ff_gate, w_ff_up, w_ff_down, w_router, w_e_gate, w_e_up, w_e_down):
    b, s, d = x.shape
    per_layer = dict(w_in=w_in, g_cq=g_cq, g_ckv=g_ckv, w_uq=w_uq, w_ukv=w_ukv, g_q_mla=g_q_mla,
                     g_k_mla=g_k_mla, g_q_dil=g_q_dil, g_k_dil=g_k_dil, g_q_dsa=g_q_dsa, g_k_dsa=g_k_dsa,
                     w_gate_a=w_gate_a, w_gate_b=w_gate_b, b_gate=b_gate, w_branch=w_branch, w_out=w_out,
                     ln_xattn=ln_xattn, ln_mem=ln_mem, w_xq=w_xq, w_xk=w_xk, w_xv=w_xv, g_q_x=g_q_x,
                     g_k_x=g_k_x, w_xo=w_xo)
    tabs = (_rope_tables(s, ROT_DIM, HEAD_DIM), _rope_tables(s, IDX_ROT, IDX_DIM),
            _rope_tables(s, MLA_ROPE, HEAD_DIM))
    depth = ln_mix.shape[0]
    outs = []
    for bi in range(b):
        xb, mb = x[bi], mem[bi]
        for i in range(depth):
            p = {name: val[i] for name, val in per_layer.items()}
            xb = _token_mixer(xb, rmsnorm_rows(xb, ln_mix[i]), p, tabs)
            xb = _cross_block(xb, mb, p)
            j = i // 2
            if i % 2 == 0:
                xb = _ffn_block(xb, ln_ffn[i], w_ff_gate[j], w_ff_up[j], w_ff_down[j])
            else:
                xb = _moe_block(xb, ln_ffn[i], w_router[j], w_e_gate[j], w_e_up[j], w_e_down[j])
        outs.append(xb)
    return jnp.stack(outs, axis=0)
```

```python
import functools
import math

import numpy as np
import jax
import jax.numpy as jnp
from jax import lax
from jax.experimental import pallas as pl
from jax.experimental.pallas import tpu as pltpu

F32 = jnp.float32
BF16 = jnp.bfloat16
I32 = jnp.int32
I16 = jnp.int16

N_BRANCHES = 4
HEAD_DIM = 128
N_HEADS = 8
BRANCH_WIDTH = N_HEADS * HEAD_DIM
ROT_DIM = HEAD_DIM // 4
ROPE_THETA = 500000.0
RMS_EPS = 1e-6
NEG = -1e30
GATE_RANK = 256
MLA_Q_LORA = 896
MLA_KV_LORA = 256
MLA_NOPE = 128
MLA_ROPE = 64
MLA_QK = MLA_NOPE + MLA_ROPE
MLA_PAD = 256
DIL_PATTERNS = ((128, 1), (512, 4), (2048, 16))
DSA_TOPK = 256
IDX_HEADS = 8
IDX_DIM = 64
IDX_ROT = IDX_DIM // 4
MEM_HEADS = 4
MEM_HEAD_DIM = 128
MEM_WIDTH = MEM_HEADS * MEM_HEAD_DIM
N_EXPERTS = 8
MOE_TOP_K = 2
INT_MIN = -2 ** 31
LOG2E = math.log2(math.e)

ATT_T = 512
DSA_TQ = 128
COUNT_ROWS = 64
VT_ROWS = HEAD_DIM
HALF = 2 ** 15
PACKED_ROWS = 128
SB_EXIT = -105.0

V7X_VMEM_LIMIT_BYTES = 56 * 1024 * 1024
LANES = 128

NT_DIMS = (((1,), (1,)), ((), ()))


def _cparams(n_axes):
    return pltpu.CompilerParams(dimension_semantics=("arbitrary",) * n_axes,
                                vmem_limit_bytes=V7X_VMEM_LIMIT_BYTES)


def _pick(n, candidates):
    for c in candidates:
        if n % c == 0:
            return c
    return n


def _rmsnorm_kernel(x_ref, g_ref, o_ref):
    x = x_ref[...].astype(F32)
    ms = jnp.mean(x * x, axis=-1, keepdims=True)
    o_ref[...] = (x * lax.rsqrt(ms + RMS_EPS) * g_ref[...]).astype(o_ref.dtype)


def rmsnorm_rows(x, g, out_dtype=BF16):
    m, d = x.shape
    tm = _pick(m, (512, 256, 128))
    return pl.pallas_call(
        _rmsnorm_kernel, grid=(m // tm,),
        in_specs=[pl.BlockSpec((tm, d), lambda i: (i, 0)), pl.BlockSpec((1, d), lambda i: (0, 0))],
        out_specs=pl.BlockSpec((tm, d), lambda i: (i, 0)),
        out_shape=jax.ShapeDtypeStruct((m, d), out_dtype),
        compiler_params=_cparams(1), name="rmsnorm_rows",
    )(x, g.reshape(1, d).astype(F32))


def _mm_kernel(*refs, nk, has_res):
    a_ref, b_ref = refs[0], refs[1]
    r_ref = refs[2] if has_res else None
    o_ref = refs[2 + has_res]
    if nk == 1:
        part = jnp.dot(a_ref[...], b_ref[...], preferred_element_type=F32)
        if has_res:
            part = r_ref[...] + part
        o_ref[...] = part.astype(o_ref.dtype)
        return
    acc_ref = refs[3 + has_res]
    k = pl.program_id(2)

    @pl.when(k == 0)
    def _():
        acc_ref[...] = jnp.zeros(acc_ref.shape, F32)

    acc_ref[...] += jnp.dot(a_ref[...], b_ref[...], preferred_element_type=F32)

    @pl.when(k == nk - 1)
    def _():
        res = acc_ref[...]
        if has_res:
            res = r_ref[...] + res
        o_ref[...] = res.astype(o_ref.dtype)


def matmul(a, b, *, out_dtype, res=None, tm=None, tn=None, tk=None):
    m, kdim = a.shape
    n = b.shape[1]
    tm = tm or _pick(m, (1024, 512, 256, 128))
    tn = tn or _pick(n, (512, 256, 128))
    tk = tk or (kdim if kdim <= 4096 else _pick(kdim, (2048, 1024, 512)))
    nk = kdim // tk
    in_specs = [pl.BlockSpec((tm, tk), lambda i, j, k: (i, k)),
                pl.BlockSpec((tk, tn), lambda i, j, k: (k, j))]
    args = [a, b]
    if res is not None:
        in_specs.append(pl.BlockSpec((tm, tn), lambda i, j, k: (i, j)))
        args.append(res)
    scratch = [pltpu.VMEM((tm, tn), F32)] if nk > 1 else []
    return pl.pallas_call(
        functools.partial(_mm_kernel, nk=nk, has_res=res is not None),
        grid=(m // tm, n // tn, nk),
        in_specs=in_specs,
        out_specs=pl.BlockSpec((tm, tn), lambda i, j, k: (i, j)),
        out_shape=jax.ShapeDtypeStruct((m, n), out_dtype),
        scratch_shapes=scratch,
        compiler_params=_cparams(3), name="matmul",
    )(*args)


def _glu_kernel(a_ref, wg_ref, wu_ref, o_ref):
    a = a_ref[...]
    g = jnp.dot(a, wg_ref[...], preferred_element_type=F32)
    u = jnp.dot(a, wu_ref[...], preferred_element_type=F32)
    o_ref[...] = (g * jax.nn.sigmoid(g) * u).astype(o_ref.dtype)


def glu_up(a, wg, wu):
    m, kdim = a.shape
    f = wg.shape[1]
    tm = _pick(m, (1024, 512, 256, 128))
    tn = _pick(f, (512, 256, 128))
    w_spec = pl.BlockSpec((kdim, tn), lambda i, j: (0, j))
    return pl.pallas_call(
        _glu_kernel, grid=(m // tm, f // tn),
        in_specs=[pl.BlockSpec((tm, kdim), lambda i, j: (i, 0)), w_spec, w_spec],
        out_specs=pl.BlockSpec((tm, tn), lambda i, j: (i, j)),
        out_shape=jax.ShapeDtypeStruct((m, f), BF16),
        compiler_params=_cparams(2), name="glu_up",
    )(a, wg, wu)


MOE_TM = 512


def _glu_grouped_kernel(te_ref, a_ref, wg_ref, wu_ref, o_ref):
    _glu_kernel(a_ref, wg_ref, wu_ref, o_ref)


def _mm_grouped_kernel(te_ref, a_ref, w_ref, o_ref):
    o_ref[...] = jnp.dot(a_ref[...], w_ref[...], preferred_element_type=F32).astype(o_ref.dtype)


def _grouped_call(kernel, name, a, weights, tile_expert, tn, out_dtype):
    r, kdim = a.shape
    n = weights[0].shape[2]
    w_spec = pl.BlockSpec((None, kdim, tn), lambda j, i, te: (te[i], 0, j))
    grid_spec = pltpu.PrefetchScalarGridSpec(
        num_scalar_prefetch=1, grid=(n // tn, r // MOE_TM),
        in_specs=[pl.BlockSpec((MOE_TM, kdim), lambda j, i, te: (i, 0))] + [w_spec] * len(weights),
        out_specs=pl.BlockSpec((MOE_TM, tn), lambda j, i, te: (i, j)))
    return pl.pallas_call(
        kernel, grid_spec=grid_spec, out_shape=jax.ShapeDtypeStruct((r, n), out_dtype),
        compiler_params=_cparams(2), name=name,
    )(tile_expert, a, *weights)


def _row_gather(idx_ref, base, src_hbm, dst, sem, n, *, wait):
    def body(r, c):
        copy = pltpu.make_async_copy(src_hbm.at[pl.ds(idx_ref[base + r], 1)], dst.at[pl.ds(r, 1)], sem)
        if wait:
            copy.wait()
        else:
            copy.start()
        return c

    lax.fori_loop(0, n, body, 0, unroll=8)


def _gather_norm_kernel(src_ref, x_hbm, g_ref, o_ref, buf, sem):
    tm = o_ref.shape[0]
    base = pl.program_id(0) * tm
    _row_gather(src_ref, base, x_hbm, buf, sem, tm, wait=False)
    _row_gather(src_ref, base, x_hbm, buf, sem, tm, wait=True)
    x = buf[...]
    ms = jnp.mean(x * x, axis=-1, keepdims=True)
    o_ref[...] = (x * lax.rsqrt(ms + RMS_EPS) * g_ref[...]).astype(o_ref.dtype)


def gather_norm(x, g, src):
    d = x.shape[1]
    r = src.shape[0]
    grid_spec = pltpu.PrefetchScalarGridSpec(
        num_scalar_prefetch=1, grid=(r // MOE_TM,),
        in_specs=[pl.BlockSpec(memory_space=pl.ANY), pl.BlockSpec((1, d), lambda i, s: (0, 0))],
        out_specs=pl.BlockSpec((MOE_TM, d), lambda i, s: (i, 0)),
        scratch_shapes=[pltpu.VMEM((MOE_TM, d), F32), pltpu.SemaphoreType.DMA(())])
    return pl.pallas_call(
        _gather_norm_kernel, grid_spec=grid_spec, out_shape=jax.ShapeDtypeStruct((r, d), BF16),
        compiler_params=_cparams(1), name="gather_norm",
    )(src, x, g.reshape(1, d).astype(F32))


def _moe_combine_kernel(pos_ref, x_ref, sel_ref, y_hbm, o_ref, buf0, buf1, sems):
    tm = o_ref.shape[0]
    base0 = pl.program_id(0) * tm
    base1 = pos_ref.shape[0] // 2 + base0
    for wait in (False, True):
        _row_gather(pos_ref, base0, y_hbm, buf0, sems.at[0], tm, wait=wait)
        _row_gather(pos_ref, base1, y_hbm, buf1, sems.at[1], tm, wait=wait)
    sel = sel_ref[...]
    o_ref[...] = x_ref[...] + sel[:, 2:3] * buf0[...] + sel[:, 3:4] * buf1[...]


def moe_combine(x, sel, y, pos):
    s, d = x.shape
    tm = _pick(s, (256, 128))
    grid_spec = pltpu.PrefetchScalarGridSpec(
        num_scalar_prefetch=1, grid=(s // tm,),
        in_specs=[pl.BlockSpec((tm, d), lambda i, p: (i, 0)), pl.BlockSpec((tm, LANES), lambda i, p: (i, 0)),
                  pl.BlockSpec(memory_space=pl.ANY)],
        out_specs=pl.BlockSpec((tm, d), lambda i, p: (i, 0)),
        scratch_shapes=[pltpu.VMEM((tm, d), F32), pltpu.VMEM((tm, d), F32), pltpu.SemaphoreType.DMA((2,))])
    return pl.pallas_call(
        _moe_combine_kernel, grid_spec=grid_spec, out_shape=jax.ShapeDtypeStruct((s, d), F32),
        compiler_params=_cparams(1), name="moe_combine",
    )(pos.reshape(-1), x, sel, y)


def _moe_plan(sel):
    s = sel.shape[0]
    e = sel[:, :MOE_TOP_K].astype(I32).T.reshape(-1)
    onehot = (e[:, None] == jnp.arange(N_EXPERTS, dtype=I32)[None, :]).astype(I32)
    rank = jnp.take_along_axis(jnp.cumsum(onehot, axis=0), e[:, None], axis=1)[:, 0] - 1
    padded = (jnp.sum(onehot, axis=0) + MOE_TM - 1) // MOE_TM * MOE_TM
    ends = jnp.cumsum(padded)
    pos = (ends - padded)[e] + rank
    n_rows = MOE_TOP_K * s + N_EXPERTS * MOE_TM
    token = jnp.tile(jnp.arange(s, dtype=I32), MOE_TOP_K)
    src = jnp.zeros((n_rows,), I32).at[pos].set(token)
    tile_start = jnp.arange(n_rows // MOE_TM, dtype=I32) * MOE_TM
    tile_expert = jnp.minimum(jnp.searchsorted(ends, tile_start, side='right'), N_EXPERTS - 1).astype(I32)
    return src, pos.reshape(MOE_TOP_K, s).astype(I32), tile_expert


def _router_kernel(x_ref, g_ref, w_ref, o_ref):
    x = x_ref[...]
    ms = jnp.mean(x * x, axis=-1, keepdims=True)
    h = x * lax.rsqrt(ms + RMS_EPS) * g_ref[...]
    logits = jnp.dot(h, w_ref[...], preferred_element_type=F32, precision=lax.Precision.HIGHEST)
    lane = lax.broadcasted_iota(I32, logits.shape, 1).astype(F32)
    logits = jnp.where(lane < N_EXPERTS, logits, -jnp.inf)
    m1 = jnp.max(logits, axis=1, keepdims=True)
    i1 = jnp.min(jnp.where(logits == m1, lane, float(LANES)), axis=1, keepdims=True)
    rest = jnp.where(lane == i1, -jnp.inf, logits)
    m2 = jnp.max(rest, axis=1, keepdims=True)
    i2 = jnp.min(jnp.where(rest == m2, lane, float(LANES)), axis=1, keepdims=True)
    e2 = jnp.exp(m2 - m1)
    w1 = 1.0 / (1.0 + e2)
    w2 = e2 / (1.0 + e2)
    o_ref[...] = (jnp.where(lane == 0.0, i1, 0.0) + jnp.where(lane == 1.0, i2, 0.0)
                  + jnp.where(lane == 2.0, w1, 0.0) + jnp.where(lane == 3.0, w2, 0.0))


def router_top2(x, g, w_router):
    m, d = x.shape
    tm = _pick(m, (256, 128))
    w = jnp.zeros((d, LANES), F32).at[:, :N_EXPERTS].set(w_router.astype(F32))
    return pl.pallas_call(
        _router_kernel, grid=(m // tm,),
        in_specs=[pl.BlockSpec((tm, d), lambda i: (i, 0)), pl.BlockSpec((1, d), lambda i: (0, 0)),
                  pl.BlockSpec((d, LANES), lambda i: (0, 0))],
        out_specs=pl.BlockSpec((tm, LANES), lambda i: (i, 0)),
        out_shape=jax.ShapeDtypeStruct((m, LANES), F32),
        compiler_params=_cparams(1), name="router_top2",
    )(x, g.reshape(1, d).astype(F32), w)


def _rope(x, tab_ref, half):
    w = x.shape[-1]
    return (x * tab_ref[0] + pltpu.roll(x, w - half, 1) * tab_ref[1]
            + pltpu.roll(x, half, 1) * tab_ref[2])


def _rope_tables(seq, rot_dim, period):
    half = rot_dim // 2
    inv_freq = ROPE_THETA ** (-jnp.arange(0, rot_dim, 2, dtype=F32) / rot_dim)
    ang = jnp.arange(seq, dtype=F32)[:, None] * inv_freq[None, :]
    cos, sin = jnp.cos(ang), jnp.sin(ang)
    ones = jnp.ones((seq, period - rot_dim), F32)
    zeros = jnp.zeros((seq, period - rot_dim), F32)
    zh = jnp.zeros((seq, half), F32)
    c = jnp.concatenate([cos, cos, ones], axis=1)
    sa = jnp.concatenate([-sin, zh, zeros], axis=1)
    sb = jnp.concatenate([zh, sin, zeros], axis=1)
    rep = LANES // period
    return jnp.stack([jnp.tile(c, (1, rep)), jnp.tile(sa, (1, rep)), jnp.tile(sb, (1, rep))], axis=0)


def _tab_spec(tm):
    return pl.BlockSpec((3, tm, LANES), lambda i: (0, i, 0))


def _row_spec(tm, w):
    return pl.BlockSpec((tm, w), lambda i: (i, 0))


def _const_spec(w):
    return pl.BlockSpec((1, w), lambda i: (0, 0))


def _prep_latent_kernel(p_ref, gq_ref, gkv_ref, cq_ref, ckv_ref, kpe_ref):
    cq = p_ref[:, :MLA_Q_LORA]
    ms = jnp.mean(cq * cq, axis=-1, keepdims=True)
    cq_ref[...] = (cq * lax.rsqrt(ms + RMS_EPS) * gq_ref[...]).astype(cq_ref.dtype)
    ckv = p_ref[:, MLA_Q_LORA:MLA_Q_LORA + MLA_KV_LORA]
    ms = jnp.mean(ckv * ckv, axis=-1, keepdims=True)
    ckv_ref[...] = (ckv * lax.rsqrt(ms + RMS_EPS) * gkv_ref[...]).astype(ckv_ref.dtype)
    kpe_ref[...] = p_ref[:, MLA_Q_LORA + MLA_KV_LORA:]


def prep_latent(p, g_cq, g_ckv):
    s = p.shape[0]
    tm = _pick(s, (512, 256, 128))
    return pl.pallas_call(
        _prep_latent_kernel, grid=(s // tm,),
        in_specs=[_row_spec(tm, p.shape[1]), _const_spec(MLA_Q_LORA), _const_spec(MLA_KV_LORA)],
        out_specs=[_row_spec(tm, MLA_Q_LORA), _row_spec(tm, MLA_KV_LORA), _row_spec(tm, LANES)],
        out_shape=[jax.ShapeDtypeStruct((s, MLA_Q_LORA), BF16),
                   jax.ShapeDtypeStruct((s, MLA_KV_LORA), BF16),
                   jax.ShapeDtypeStruct((s, LANES), F32)],
        compiler_params=_cparams(1), name="prep_latent",
    )(p, g_cq.reshape(1, -1), g_ckv.reshape(1, -1))


def _prep_mla_kernel(qup_ref, kvup_ref, kpe_ref, gq_ref, gk_ref, tab_ref, qT_ref, k_ref, vT_ref):
    half = MLA_ROPE // 2
    gqn, gqr = gq_ref[:, :LANES], gq_ref[:, LANES:]
    gkn, gkr = gk_ref[:, :LANES], gk_ref[:, LANES:]
    kpe = kpe_ref[...]
    kpe_ss = jnp.sum(kpe * kpe, axis=-1, keepdims=True)
    kr_base = _rope(kpe * gkr, tab_ref, half)
    scale = LOG2E * MLA_QK ** -0.5
    for h in range(N_HEADS):
        qn = qup_ref[:, h * MLA_PAD:h * MLA_PAD + LANES]
        qr = qup_ref[:, h * MLA_PAD + LANES:(h + 1) * MLA_PAD]
        ss = jnp.sum(qn * qn, axis=-1, keepdims=True) + jnp.sum(qr * qr, axis=-1, keepdims=True)
        r = lax.rsqrt(ss * (1.0 / MLA_QK) + RMS_EPS) * scale
        qT_ref[h, :LANES, :] = (qn * r * gqn).T.astype(BF16)
        qT_ref[h, LANES:, :] = _rope(qr * r * gqr, tab_ref, half).T.astype(BF16)
        kn = kvup_ref[:, h * LANES:(h + 1) * LANES]
        ss = jnp.sum(kn * kn, axis=-1, keepdims=True) + kpe_ss
        r = lax.rsqrt(ss * (1.0 / MLA_QK) + RMS_EPS)
        k_ref[:, h * MLA_PAD:h * MLA_PAD + LANES] = (kn * r * gkn).astype(BF16)
        k_ref[:, h * MLA_PAD + LANES:(h + 1) * MLA_PAD] = (kr_base * r).astype(BF16)
        vT_ref[h] = _vT_bf16(kvup_ref[:, BRANCH_WIDTH + h * LANES:BRANCH_WIDTH + (h + 1) * LANES])


def _qT_spec(d, t):
    return pl.BlockSpec((N_HEADS, d, t), lambda i: (0, 0, i))


def _vT_spec(t, rows):
    return pl.BlockSpec((N_HEADS, None, rows, t), lambda i: (0, i, 0, 0))


def _qkv_shapes(s, dq, wk, t, rows):
    return [jax.ShapeDtypeStruct((N_HEADS, dq, s), BF16), jax.ShapeDtypeStruct((s, wk), BF16),
            jax.ShapeDtypeStruct((N_HEADS, s // t, rows, t), BF16)]


def _vT_bf16(v):
    return v.T.astype(BF16)


def prep_mla(qup, kvup, kpe, g_q, g_k, tab):
    s = qup.shape[0]
    t = ATT_T
    pad = lambda g: jnp.zeros((1, MLA_PAD), F32).at[0, :MLA_QK].set(g)
    wq = N_HEADS * MLA_PAD
    return pl.pallas_call(
        _prep_mla_kernel, grid=(s // t,),
        in_specs=[_row_spec(t, wq), _row_spec(t, 2 * BRANCH_WIDTH), _row_spec(t, LANES),
                  _const_spec(MLA_PAD), _const_spec(MLA_PAD), _tab_spec(t)],
        out_specs=[_qT_spec(MLA_PAD, t), _row_spec(t, wq), _vT_spec(t, VT_ROWS)],
        out_shape=_qkv_shapes(s, MLA_PAD, wq, t, VT_ROWS),
        compiler_params=_cparams(1), name="prep_mla",
    )(qup, kvup, kpe, pad(g_q), pad(g_k), tab)


def _head_norm_rope(x, g, tab_ref, scale):
    ms = jnp.mean(x * x, axis=-1, keepdims=True)
    y = x * (lax.rsqrt(ms + RMS_EPS) * scale) * g
    return _rope(y, tab_ref, ROT_DIM // 2)


def _prep_sb_kernel(p_ref, qT_ref, k_ref, vT_ref, kn_ref):
    for h in range(N_HEADS):
        sl = slice(h * LANES, (h + 1) * LANES)
        qT_ref[h] = (p_ref[:, sl] * HEAD_DIM ** -0.5).T.astype(BF16)
        vT_ref[h] = p_ref[:, 2 * BRANCH_WIDTH + h * LANES:2 * BRANCH_WIDTH + (h + 1) * LANES].T.astype(BF16)
        kb = p_ref[:, BRANCH_WIDTH + h * LANES:BRANCH_WIDTH + (h + 1) * LANES].astype(BF16)
        k_ref[:, sl] = kb
        kf = kb.astype(F32)
        kn_ref[h] = jnp.zeros((8, LANES), F32) + jnp.max(jnp.sum(kf * kf, axis=1, keepdims=True))


def prep_sb(p):
    s = p.shape[0]
    t = ATT_T
    return pl.pallas_call(
        _prep_sb_kernel, grid=(s // t,),
        in_specs=[_row_spec(t, 3 * BRANCH_WIDTH)],
        out_specs=[_qT_spec(HEAD_DIM, t), _row_spec(t, BRANCH_WIDTH), _vT_spec(t, HEAD_DIM),
                   pl.BlockSpec((N_HEADS, None, 8, LANES), lambda i: (0, i, 0, 0))],
        out_shape=_qkv_shapes(s, HEAD_DIM, BRANCH_WIDTH, t, HEAD_DIM)
        + [jax.ShapeDtypeStruct((N_HEADS, s // t, 8, LANES), F32)],
        compiler_params=_cparams(1), name="prep_sb",
    )(p)


def _prep_dil_kernel(p_ref, gq_ref, gk_ref, tab_ref, qT_ref, k_ref, vT_ref):
    gq, gk = gq_ref[...], gk_ref[...]
    for h in range(N_HEADS):
        sl = slice(h * LANES, (h + 1) * LANES)
        qT_ref[h] = _head_norm_rope(p_ref[:, sl], gq, tab_ref, LOG2E * HEAD_DIM ** -0.5).T.astype(BF16)
        ksl = slice(BRANCH_WIDTH + h * LANES, BRANCH_WIDTH + (h + 1) * LANES)
        k_ref[:, sl] = _head_norm_rope(p_ref[:, ksl], gk, tab_ref, 1.0).astype(BF16)
        vT_ref[h] = _vT_bf16(p_ref[:, 2 * BRANCH_WIDTH + h * LANES:2 * BRANCH_WIDTH + (h + 1) * LANES])


def prep_dil(p, g_q, g_k, tab):
    s = p.shape[0]
    t = ATT_T
    return pl.pallas_call(
        _prep_dil_kernel, grid=(s // t,),
        in_specs=[_row_spec(t, 3 * BRANCH_WIDTH), _const_spec(LANES), _const_spec(LANES), _tab_spec(t)],
        out_specs=[_qT_spec(HEAD_DIM, t), _row_spec(t, BRANCH_WIDTH), _vT_spec(t, VT_ROWS)],
        out_shape=_qkv_shapes(s, HEAD_DIM, BRANCH_WIDTH, t, VT_ROWS),
        compiler_params=_cparams(1), name="prep_dil",
    )(p, g_q.reshape(1, -1), g_k.reshape(1, -1), tab)


def _prep_dsa_kernel(p_ref, gq_ref, gk_ref, tab_ref, qT_ref, k_ref, vT_ref):
    gq, gk = gq_ref[...], gk_ref[...]
    for h in range(N_HEADS):
        sl = slice(h * LANES, (h + 1) * LANES)
        qT = _head_norm_rope(p_ref[:, sl], gq, tab_ref, LOG2E * HEAD_DIM ** -0.5).T.astype(BF16)
        for b in range(p_ref.shape[0] // DSA_TQ):
            qT_ref[b, :, sl] = qT[:, b * DSA_TQ:(b + 1) * DSA_TQ]
    k_ref[...] = _head_norm_rope(p_ref[:, BRANCH_WIDTH:BRANCH_WIDTH + LANES], gk, tab_ref, 1.0).astype(BF16)
    vT_ref[...] = _vT_bf16(p_ref[:, BRANCH_WIDTH + LANES:])


def prep_dsa(p, g_q, g_k, tab):
    s = p.shape[0]
    t = ATT_T
    nqb = t // DSA_TQ
    return pl.pallas_call(
        _prep_dsa_kernel, grid=(s // t,),
        in_specs=[_row_spec(t, BRANCH_WIDTH + 2 * LANES), _const_spec(LANES), _const_spec(LANES), _tab_spec(t)],
        out_specs=[pl.BlockSpec((nqb, HEAD_DIM, BRANCH_WIDTH), lambda i: (i, 0, 0)), _row_spec(t, LANES),
                   pl.BlockSpec((None, VT_ROWS, t), lambda i: (i, 0, 0))],
        out_shape=[jax.ShapeDtypeStruct((s // DSA_TQ, HEAD_DIM, BRANCH_WIDTH), BF16),
                   jax.ShapeDtypeStruct((s, LANES), BF16),
                   jax.ShapeDtypeStruct((s // t, VT_ROWS, t), BF16)],
        compiler_params=_cparams(1), name="prep_dsa",
    )(p, g_q.reshape(1, -1), g_k.reshape(1, -1), tab)


def _prep_idx_kernel(p_ref, tab_ref, qiT_ref, ki_ref, wiT_ref):
    half = IDX_ROT // 2
    t = p_ref.shape[0]
    nqb = t // DSA_TQ
    lane = lax.broadcasted_iota(I32, (t, LANES), 1)
    first = lane < IDX_DIM
    zero = jnp.zeros((t, LANES), F32)
    for b in range(IDX_HEADS // 2):
        x = _rope(p_ref[:, b * LANES:(b + 1) * LANES], tab_ref, half) * (IDX_DIM ** -0.5)
        hi = x.astype(BF16).astype(F32)
        lo = x - hi
        rhi = pltpu.roll(hi, IDX_DIM, 1)
        for hh, (a0, a1) in enumerate(((jnp.where(first, hi, pltpu.roll(lo, IDX_DIM, 1)), jnp.where(first, hi, zero)),
                                       (jnp.where(first, rhi, lo), jnp.where(first, rhi, zero)))):
            h = 2 * b + hh
            a0T, a1T = a0.T.astype(BF16), a1.T.astype(BF16)
            for qb in range(nqb):
                qs = slice(qb * DSA_TQ, (qb + 1) * DSA_TQ)
                qiT_ref[qb, :LANES, h * DSA_TQ:(h + 1) * DSA_TQ] = a0T[:, qs]
                qiT_ref[qb, LANES:, h * DSA_TQ:(h + 1) * DSA_TQ] = a1T[:, qs]
    kx = _rope(p_ref[:, IDX_HEADS * IDX_DIM:IDX_HEADS * IDX_DIM + LANES], tab_ref, half)
    hi = kx.astype(BF16).astype(F32)
    lo = kx - hi
    ki_ref[:, :LANES] = jnp.where(first, hi, pltpu.roll(hi, IDX_DIM, 1)).astype(BF16)
    ki_ref[:, LANES:] = jnp.where(first, lo, zero).astype(BF16)
    wT = (p_ref[:, IDX_HEADS * IDX_DIM + LANES:] * (IDX_HEADS ** -0.5)).T
    for qb in range(nqb):
        wiT_ref[qb] = wT[:IDX_HEADS, qb * DSA_TQ:(qb + 1) * DSA_TQ]


def prep_idx(p, tab):
    s = p.shape[0]
    t = ATT_T
    nqb = t // DSA_TQ
    return pl.pallas_call(
        _prep_idx_kernel, grid=(s // t,),
        in_specs=[_row_spec(t, IDX_HEADS * IDX_DIM + 2 * LANES), _tab_spec(t)],
        out_specs=[pl.BlockSpec((nqb, 2 * LANES, IDX_HEADS * DSA_TQ), lambda i: (i, 0, 0)),
                   _row_spec(t, 2 * LANES),
                   pl.BlockSpec((nqb, IDX_HEADS, DSA_TQ), lambda i: (i, 0, 0))],
        out_shape=[jax.ShapeDtypeStruct((s // DSA_TQ, 2 * LANES, IDX_HEADS * DSA_TQ), BF16),
                   jax.ShapeDtypeStruct((s, 2 * LANES), BF16),
                   jax.ShapeDtypeStruct((s // DSA_TQ, IDX_HEADS, DSA_TQ), F32)],
        compiler_params=_cparams(1), name="prep_idx",
    )(p, tab)


def _softmax_step(sT, vT, m_sc, acc_sc):
    l_sc, o_sc = acc_sc
    m_prev = m_sc[...]
    m_new = jnp.maximum(m_prev, jnp.max(sT, axis=0, keepdims=True))
    alpha = jnp.exp2(m_prev - m_new)
    p = jnp.exp2(sT - m_new)
    l_sc[...] = alpha * l_sc[...] + jnp.sum(p, axis=0, keepdims=True)
    o_sc[...] = alpha * o_sc[...] + jnp.dot(vT, p.astype(BF16), preferred_element_type=F32)
    m_sc[...] = m_new


def _softmax_init(m_sc, acc_sc):
    m_sc[...] = jnp.full(m_sc.shape, NEG, F32)
    for ref in acc_sc:
        ref[...] = jnp.zeros(ref.shape, F32)


def _softmax_result(acc_sc):
    l_sc, o_sc = acc_sc
    return o_sc[...] / l_sc[...]


def _pipelined_blocks(n_full, scores, consume, consume_last, s_sc):
    s_sc[0] = scores(0)

    def pair(jj, c):
        s_sc[1] = scores(2 * jj + 1)
        consume(s_sc[0], 2 * jj)
        s_sc[0] = scores(2 * jj + 2)
        consume(s_sc[1], 2 * jj + 1)
        return c

    lax.fori_loop(0, n_full // 2, pair, 0)

    @pl.when(n_full % 2 == 1)
    def _():
        s_sc[1] = scores(n_full)
        consume(s_sc[0], n_full - 1)
        consume_last(s_sc[1], n_full)

    @pl.when(n_full % 2 == 0)
    def _():
        consume_last(s_sc[0], n_full)


def _key_le_query(tk, tq):
    return lax.broadcasted_iota(I32, (tk, tq), 0) <= lax.broadcasted_iota(I32, (tk, tq), 1)


def _mla_kernel(qT_ref, k_ref, vT_ref, o_ref, s_sc, m_sc, l_sc, o_sc):
    acc_sc = (l_sc, o_sc)
    i = pl.program_id(1)
    t = qT_ref.shape[1]
    qT = qT_ref[...]
    _softmax_init(m_sc, acc_sc)
    scores = lambda j: jnp.dot(k_ref[j], qT, preferred_element_type=F32)
    step = lambda sT, j: _softmax_step(sT, vT_ref[j], m_sc, acc_sc)
    last = lambda sT, j: _softmax_step(jnp.where(_key_le_query(t, t), sT, NEG), vT_ref[j], m_sc, acc_sc)
    _pipelined_blocks(i, scores, step, last, s_sc)
    o_ref[...] = _softmax_result(acc_sc).T.astype(o_ref.dtype)


def _attn_scratch(tk, tq):
    return [pltpu.VMEM((2, tk, tq), F32), pltpu.VMEM((1, tq), F32), pltpu.VMEM((1, tq), F32),
            pltpu.VMEM((HEAD_DIM, tq), F32)]


def _head_qT_spec(d, t):
    return pl.BlockSpec((None, d, t), lambda h, i: (h, 0, i))


def _head_k_spec(nb, t, w):
    return pl.BlockSpec((nb, t, w), lambda h, i: (0, 0, h))


def _head_vT_spec(nb, t, rows=VT_ROWS):
    return pl.BlockSpec((None, nb, rows, t), lambda h, i: (h, 0, 0, 0))


def _head_out_spec(t):
    return pl.BlockSpec((t, HEAD_DIM), lambda h, i: (i, h))


def mla_attention(qT, k, vT):
    s = k.shape[0]
    t = ATT_T
    nb = s // t
    return pl.pallas_call(
        _mla_kernel, grid=(N_HEADS, nb),
        in_specs=[_head_qT_spec(MLA_PAD, t), _head_k_spec(nb, t, MLA_PAD), _head_vT_spec(nb, t)],
        out_specs=_head_out_spec(t),
        out_shape=jax.ShapeDtypeStruct((s, BRANCH_WIDTH), BF16),
        scratch_shapes=_attn_scratch(t, t),
        compiler_params=_cparams(2), name="mla_attention",
    )(qT, k.reshape(nb, t, -1), vT)


SB_SUB = 128


def _sb_kernel(qT_ref, k_ref, vT_ref, u_ref, kn_ref, o_ref, carry_sc, acc_sc):
    i = pl.program_id(1)
    t = qT_ref.shape[1]
    qT = qT_ref[...]
    u2 = u_ref[...]
    qf = qT.astype(F32)
    zbound = jnp.sqrt(jnp.sum(qf * qf, axis=0, keepdims=True) * jnp.max(kn_ref[...]))
    carry_sc[...] = jnp.zeros(carry_sc.shape, F32)
    acc_sc[...] = jnp.zeros(acc_sc.shape, F32)
    key = lax.broadcasted_iota(I32, (SB_SUB, t), 0)
    qry = lax.broadcasted_iota(I32, (SB_SUB, t), 1)

    def block(j, diag):
        zT = jnp.dot(k_ref[j], qT, preferred_element_type=F32)
        carry = carry_sc[...]
        parts = [None] * (t // SB_SUB)
        for c in reversed(range(t // SB_SUB)):
            zc = zT[c * SB_SUB:(c + 1) * SB_SUB]
            ls = jnp.minimum(-zc, 0.0) - jnp.log(1.0 + jnp.exp(-jnp.abs(zc)))
            if diag:
                past = key + c * SB_SUB < qry
                ls = jnp.where(past, ls, 0.0)
            hi = ls.astype(BF16)
            lo = (ls - hi.astype(F32)).astype(BF16)
            rev = jnp.dot(u2, jnp.concatenate([hi, lo], axis=0), preferred_element_type=F32) + carry
            a = jnp.exp(jnp.minimum(zc + rev, 0.0))
            if diag:
                a = jnp.where(past, a, 0.0)
            parts[c] = a.astype(BF16)
            carry = rev[0:1, :]
        carry_sc[...] = carry
        acc_sc[...] += jnp.dot(vT_ref[j], jnp.concatenate(parts, axis=0), preferred_element_type=F32)

    def all_underflow():
        return (jnp.max(carry_sc[...] + zbound) < SB_EXIT).astype(I32)

    block(i, True)

    def back(state):
        jj, _ = state
        block(i - 1 - jj, False)
        return jj + 1, all_underflow()

    lax.while_loop(lambda st: jnp.logical_and(st[0] < i, st[1] == 0), back, (jnp.int32(0), all_underflow()))
    o_ref[...] = acc_sc[...].T.astype(o_ref.dtype)


def sb_attention(qT, k, vT, kn):
    s = k.shape[0]
    t = ATT_T
    nb = s // t
    tri = (np.arange(SB_SUB)[None, :] >= np.arange(SB_SUB)[:, None]).astype(np.float32)
    u2 = jnp.asarray(np.concatenate([tri, tri], axis=1), dtype=BF16)
    return pl.pallas_call(
        _sb_kernel, grid=(N_HEADS, nb),
        in_specs=[_head_qT_spec(HEAD_DIM, t), _head_k_spec(nb, t, HEAD_DIM), _head_vT_spec(nb, t, HEAD_DIM),
                  pl.BlockSpec((SB_SUB, 2 * SB_SUB), lambda h, i: (0, 0)),
                  pl.BlockSpec((None, nb, 8, LANES), lambda h, i: (h, 0, 0, 0))],
        out_specs=_head_out_spec(t),
        out_shape=jax.ShapeDtypeStruct((s, BRANCH_WIDTH), BF16),
        scratch_shapes=[pltpu.VMEM((1, t), F32), pltpu.VMEM((HEAD_DIM, t), F32)],
        compiler_params=_cparams(2), name="sb_attention",
    )(qT, k.reshape(nb, t, -1), vT, u2, kn)


def _dil_log_weights(t):
    span = max(w for w, _ in DIL_PATTERNS)
    nback = -(-span // t)
    d = np.arange(nback + 1)[:, None, None] * t + np.arange(t)[None, None, :] - np.arange(t)[None, :, None]
    mult = np.zeros(d.shape, np.float64)
    for window, dil in DIL_PATTERNS:
        mult += ((d >= 0) & (d <= window) & (d % dil == 0))
    return np.where(mult > 0, np.log2(np.maximum(mult, 1.0)), NEG).astype(np.float32)


def _dil_kernel(qT_ref, k_ref, vT_ref, w_ref, o_ref, s_sc, m_sc, l_sc, o_sc):
    acc_sc = (l_sc, o_sc)
    i = pl.program_id(1)
    nback = w_ref.shape[0] - 1
    qT = qT_ref[...]
    _softmax_init(m_sc, acc_sc)
    scores = lambda d: jnp.dot(k_ref[i - d], qT, preferred_element_type=F32)
    step = lambda sT, d: _softmax_step(sT + w_ref[d], vT_ref[i - d], m_sc, acc_sc)
    _pipelined_blocks(jnp.minimum(i, nback), scores, step, step, s_sc)
    o_ref[...] = _softmax_result(acc_sc).T.astype(o_ref.dtype)


def dil_attention(qT, k, vT):
    s = k.shape[0]
    t = ATT_T
    nb = s // t
    logw = jnp.asarray(_dil_log_weights(t))
    return pl.pallas_call(
        _dil_kernel, grid=(N_HEADS, nb),
        in_specs=[_head_qT_spec(HEAD_DIM, t), _head_k_spec(nb, t, HEAD_DIM), _head_vT_spec(nb, t),
                  pl.BlockSpec(logw.shape, lambda h, i: (0, 0, 0))],
        out_specs=_head_out_spec(t),
        out_shape=jax.ShapeDtypeStruct((s, BRANCH_WIDTH), BF16),
        scratch_shapes=_attn_scratch(t, t),
        compiler_params=_cparams(2), name="dil_attention",
    )(qT, k.reshape(nb, t, -1), vT, logw)


def _dsa_kernel(qsT_ref, qiT_ref, wiT_ref, ki_ref, k_ref, vT_ref, o_ref,
                key_sc, hi_sc, lo_sc, q1_sc, s_sc, m_sc, l_sc, o_sc, *, top_k, pos_bits):
    acc_sc = (l_sc, o_sc)
    i = pl.program_id(0)
    tq, tk = DSA_TQ, ATT_T
    nkb = (i * tq + tq + tk - 1) // tk
    qiT = qiT_ref[...]
    wiT = wiT_ref[...]
    row = lax.broadcasted_iota(I32, (tk, tq), 1) + i * tq
    col = lax.broadcasted_iota(I32, (tk, tq), 0)

    def idx_block(aT, j):
        idx = jnp.zeros((tk, tq), F32)
        for h in range(IDX_HEADS):
            idx = idx + jnp.maximum(aT[:, h * tq:(h + 1) * tq], 0.0) * wiT[h:h + 1, :]
        bits = pltpu.bitcast(idx + 0.0, I32)
        key = bits ^ ((bits >> 31) & 0x7FFFFFFF)
        key = jnp.where(col + j * tk <= row, key, INT_MIN)
        key_sc[j] = key
        hi_sc[j] = (key >> 16).astype(I16)
        lo_sc[j] = ((key & 0xFFFF) - HALF).astype(I16)

    _pipelined_blocks(nkb - 1, lambda j: jnp.dot(ki_ref[j], qiT, preferred_element_type=F32),
                      idx_block, idx_block, s_sc)

    def count(hits):
        def body(j, acc):
            hit = hits(key_sc[j], j)
            return acc + jnp.sum(hit.reshape(tk // COUNT_ROWS, COUNT_ROWS, tq), axis=0)
        acc = lax.fori_loop(0, nkb, body, jnp.zeros((COUNT_ROWS, tq), F32))
        return jnp.sum(acc, axis=0, keepdims=True)

    def count16(ref, cand):
        c16 = cand.astype(I16)
        one, zero = jnp.ones((), BF16), jnp.zeros((), BF16)

        def body(j, acc):
            hit = jnp.where(ref[j] >= c16, one, zero)
            parts = [hit[r * PACKED_ROWS:(r + 1) * PACKED_ROWS] for r in range(tk // PACKED_ROWS)]
            while len(parts) > 1:
                parts = [a + b for a, b in zip(parts[::2], parts[1::2])]
            return acc + parts[0]

        acc = lax.fori_loop(0, nkb, body, jnp.zeros((PACKED_ROWS, tq), BF16))
        return jnp.sum(acc.astype(F32), axis=0, keepdims=True)

    def bisect16(ref, need, total):
        def body(b, carry):
            t, cnt = carry
            cand = t + jnp.left_shift(jnp.int32(1), 15 - b)
            c = count16(ref, cand)
            ok = c >= need
            return jnp.where(ok, cand, t), jnp.where(ok, c, cnt)

        return lax.fori_loop(0, 16, body, (jnp.full((1, tq), -HALF, I32), total))

    total = jnp.zeros((1, tq), F32) + (nkb * tk).astype(F32)
    thr_hi, cge_hi = bisect16(hi_sc, float(top_k), total)
    above = jnp.where(thr_hi < HALF - 1, count16(hi_sc, jnp.minimum(thr_hi + 1, HALF - 1)), 0.0)
    t16 = thr_hi.astype(I16)

    def keep_equal(j, c):
        lo_sc[j] = jnp.where(hi_sc[j] == t16, lo_sc[j], jnp.int16(-HALF))
        return c

    lax.fori_loop(0, nkb, keep_equal, 0)
    thr_lo, cge_lo = bisect16(lo_sc, top_k - above, cge_hi - above)
    thr = thr_hi * (2 * HALF) + (thr_lo + HALF)
    cge = above + cge_lo

    q1_sc[...] = jnp.full((1, tq), 2 ** 30, I32)

    @pl.when(jnp.max(cge) > top_k)
    def _():
        need = top_k - count(lambda kb, j: jnp.where(kb > thr, 1.0, 0.0))

        def pos_body(b, qpos):
            cand = qpos + jnp.left_shift(jnp.int32(1), pos_bits - 1 - b)
            g = count(lambda kb, j: jnp.where(kb == thr, jnp.where(col + j * tk < cand, 1.0, 0.0), 0.0))
            return jnp.where(g < need, cand, qpos)

        q1_sc[...] = lax.fori_loop(0, pos_bits, pos_body, jnp.zeros((1, tq), I32)) + 1

    q1 = q1_sc[...]
    qsT = qsT_ref[...]
    _softmax_init(m_sc, acc_sc)
    def att_block(sT, j):
        kb = key_sc[j]
        pos = col + j * tk
        bias = jnp.where(kb > thr, 0.0, jnp.where(kb == thr, jnp.where(pos < q1, 0.0, NEG), NEG))
        bias = jnp.where(pos <= row, bias, NEG)
        _softmax_step(sT + jnp.concatenate([bias] * N_HEADS, axis=1), vT_ref[j], m_sc, acc_sc)

    _pipelined_blocks(nkb - 1, lambda j: jnp.dot(k_ref[j], qsT, preferred_element_type=F32),
                      att_block, att_block, s_sc)
    o = _softmax_result(acc_sc)
    for h in range(N_HEADS):
        sl = slice(h * HEAD_DIM, (h + 1) * HEAD_DIM)
        o_ref[:, sl] = o[:, h * tq:(h + 1) * tq].T.astype(o_ref.dtype)


def dsa_attention(qsT, qiT, wiT, ki, k, vT):
    s = k.shape[0]
    tq, tk = DSA_TQ, ATT_T
    nkb = s // tk
    top_k = min(DSA_TOPK, s // 4)
    wq = N_HEADS * tq
    full3 = lambda a, b: pl.BlockSpec((nkb, a, b), lambda i: (0, 0, 0))
    return pl.pallas_call(
        functools.partial(_dsa_kernel, top_k=top_k, pos_bits=(s - 1).bit_length()),
        grid=(s // tq,),
        in_specs=[pl.BlockSpec((None, HEAD_DIM, wq), lambda i: (i, 0, 0)),
                  pl.BlockSpec((None, 2 * LANES, wq), lambda i: (i, 0, 0)),
                  pl.BlockSpec((None, IDX_HEADS, tq), lambda i: (i, 0, 0)),
                  full3(tk, 2 * LANES), full3(tk, HEAD_DIM), full3(VT_ROWS, tk)],
        out_specs=pl.BlockSpec((tq, BRANCH_WIDTH), lambda i: (i, 0)),
        out_shape=jax.ShapeDtypeStruct((s, BRANCH_WIDTH), BF16),
        scratch_shapes=[pltpu.VMEM((nkb, tk, tq), I32), pltpu.VMEM((nkb, tk, tq), I16),
                        pltpu.VMEM((nkb, tk, tq), I16), pltpu.VMEM((1, tq), I32)] + _attn_scratch(tk, wq),
        compiler_params=_cparams(1), name="dsa_attention",
    )(qsT, qiT, wiT, ki.reshape(nkb, tk, -1), k.reshape(nkb, tk, -1), vT)


def _merge_kernel(g_ref, o0_ref, o1_ref, o2_ref, o3_ref, wgb_ref, b_ref, wbr_ref, out_ref):
    g = g_ref[...]
    acc = None
    for n, o_ref in enumerate((o0_ref, o1_ref, o2_ref, o3_ref)):
        gate = jax.nn.sigmoid(jnp.dot(g, wgb_ref[n], preferred_element_type=F32) + b_ref[n])
        val = gate * jnp.dot(o_ref[...], wbr_ref[n], preferred_element_type=F32)
        acc = val if acc is None else acc + val
    out_ref[...] = acc.astype(out_ref.dtype)


def gated_merge(g_lat, outs, w_gate_b, b_gate, w_branch):
    s = g_lat.shape[0]
    d = w_gate_b.shape[-1]
    tm = _pick(s, (1024, 512, 256, 128))
    tn = _pick(d, (512, 256, 128))
    o_spec = pl.BlockSpec((tm, BRANCH_WIDTH), lambda i, j: (i, 0))
    return pl.pallas_call(
        _merge_kernel, grid=(s // tm, d // tn),
        in_specs=[pl.BlockSpec((tm, GATE_RANK), lambda i, j: (i, 0)), o_spec, o_spec, o_spec, o_spec,
                  pl.BlockSpec((N_BRANCHES, GATE_RANK, tn), lambda i, j: (0, 0, j)),
                  pl.BlockSpec((N_BRANCHES, 1, tn), lambda i, j: (0, 0, j)),
                  pl.BlockSpec((N_BRANCHES, BRANCH_WIDTH, tn), lambda i, j: (0, 0, j))],
        out_specs=pl.BlockSpec((tm, tn), lambda i, j: (i, j)),
        out_shape=jax.ShapeDtypeStruct((s, d), BF16),
        compiler_params=_cparams(2), name="gated_merge",
    )(g_lat, *outs, w_gate_b, b_gate.reshape(N_BRANCHES, 1, d), w_branch)


def _cross_kernel(x_ref, g_ref, wq_ref, gq_ref, kraw_ref, gk_ref, v_ref, wo_ref, o_ref):
    x = x_ref[...]
    ms = jnp.mean(x * x, axis=-1, keepdims=True)
    h = (x * lax.rsqrt(ms + RMS_EPS) * g_ref[...]).astype(BF16)
    q = jnp.dot(h, wq_ref[...], preferred_element_type=F32)
    outs = []
    for hd in range(MEM_HEADS):
        sl = slice(hd * MEM_HEAD_DIM, (hd + 1) * MEM_HEAD_DIM)
        qh = q[:, sl]
        qh = qh * (lax.rsqrt(jnp.mean(qh * qh, axis=-1, keepdims=True) + RMS_EPS) * MEM_HEAD_DIM ** -0.5) * gq_ref[...]
        kh = kraw_ref[:, sl]
        kh = kh * lax.rsqrt(jnp.mean(kh * kh, axis=-1, keepdims=True) + RMS_EPS) * gk_ref[...]
        s = lax.dot_general(qh.astype(BF16), kh.astype(BF16), NT_DIMS, preferred_element_type=F32)
        p = jnp.exp(s - jnp.max(s, axis=1, keepdims=True))
        p = p / jnp.sum(p, axis=1, keepdims=True)
        outs.append(jnp.dot(p.astype(BF16), v_ref[:, sl], preferred_element_type=F32).astype(BF16))
    o = jnp.concatenate(outs, axis=1)
    o_ref[...] = x + jnp.dot(o, wo_ref[...], preferred_element_type=F32)


def cross_attention(x, g, w_xq, g_q, k_raw, g_k, v, w_xo):
    s, d = x.shape
    m = k_raw.shape[0]
    tm = _pick(s, (256, 128))
    c2 = lambda shape: pl.BlockSpec(shape, lambda i: (0, 0))
    return pl.pallas_call(
        _cross_kernel, grid=(s // tm,),
        in_specs=[_row_spec(tm, d), c2((1, d)), c2((d, MEM_WIDTH)), c2((1, MEM_HEAD_DIM)),
                  c2((m, MEM_WIDTH)), c2((1, MEM_HEAD_DIM)), c2((m, MEM_WIDTH)), c2((MEM_WIDTH, d))],
        out_specs=_row_spec(tm, d),
        out_shape=jax.ShapeDtypeStruct((s, d), F32),
        compiler_params=_cparams(1), name="cross_attention",
    )(x, g.reshape(1, d), w_xq, g_q.reshape(1, -1), k_raw, g_k.reshape(1, -1), v, w_xo)


def _split_w_in(w_in):
    cuts = np.cumsum([0, MLA_Q_LORA, MLA_KV_LORA, MLA_ROPE, BRANCH_WIDTH, BRANCH_WIDTH, BRANCH_WIDTH,
                      BRANCH_WIDTH, BRANCH_WIDTH, BRANCH_WIDTH, BRANCH_WIDTH, HEAD_DIM, HEAD_DIM,
                      IDX_HEADS * IDX_DIM, IDX_DIM, IDX_HEADS])
    seg = lambda a, b: w_in[:, cuts[a]:cuts[b]]
    zeros = lambda n: jnp.zeros((w_in.shape[0], n), w_in.dtype)
    w_lat = jnp.concatenate([seg(0, 3), zeros(LANES - MLA_ROPE)], axis=1)
    w_sb = seg(3, 6)
    w_dil = seg(6, 9)
    w_dsa = seg(9, 12)
    w_idx = jnp.concatenate([seg(12, 14), zeros(LANES - IDX_DIM), seg(14, 15), zeros(LANES - IDX_HEADS)], axis=1)
    return [w.astype(BF16) for w in (w_lat, w_sb, w_dil, w_dsa, w_idx)]


def _pad_mla_up(w_uq, w_ukv):
    r = w_uq.shape[0]
    wq = w_uq.reshape(r, N_HEADS, MLA_QK)
    wq = jnp.concatenate([wq, jnp.zeros((r, N_HEADS, MLA_PAD - MLA_QK), wq.dtype)], axis=2)
    wkv = w_ukv.reshape(w_ukv.shape[0], N_HEADS, 2 * HEAD_DIM)
    wkv = jnp.concatenate([wkv[:, :, :MLA_NOPE].reshape(-1, BRANCH_WIDTH),
                           wkv[:, :, MLA_NOPE:].reshape(-1, BRANCH_WIDTH)], axis=1)
    return wq.reshape(r, N_HEADS * MLA_PAD).astype(BF16), wkv.astype(BF16)


def _branches(h, p, tabs):
    tab_p, tab_i, tab_m = tabs
    w_lat, w_sb, w_dil, w_dsa, w_idx = _split_w_in(p['w_in'])
    w_uq, w_ukv = _pad_mla_up(p['w_uq'], p['w_ukv'])

    cq, ckv, kpe = prep_latent(matmul(h, w_lat, out_dtype=F32), p['g_cq'], p['g_ckv'])
    qT, k, vT = prep_mla(matmul(cq, w_uq, out_dtype=F32), matmul(ckv, w_ukv, out_dtype=F32), kpe,
                         p['g_q_mla'], p['g_k_mla'], tab_m)
    o_mla = mla_attention(qT, k, vT)
    o_sb = sb_attention(*prep_sb(matmul(h, w_sb, out_dtype=F32)))
    o_dil = dil_attention(*prep_dil(matmul(h, w_dil, out_dtype=F32), p['g_q_dil'], p['g_k_dil'], tab_p))
    qsT, ks, vsT = prep_dsa(matmul(h, w_dsa, out_dtype=F32), p['g_q_dsa'], p['g_k_dsa'], tab_p)
    qiT, ki, wiT = prep_idx(matmul(h, w_idx, out_dtype=F32), tab_i)
    o_dsa = dsa_attention(qsT, qiT, wiT, ki, ks, vsT)
    return o_mla, o_sb, o_dil, o_dsa


def _token_mixer(x, h, p, tabs):
    o_mla, o_sb, o_dil, o_dsa = _branches(h, p, tabs)
    g_lat = matmul(h, p['w_gate_a'].astype(BF16), out_dtype=BF16)
    merged = gated_merge(g_lat, (o_mla, o_sb, o_dil, o_dsa), p['w_gate_b'].astype(BF16),
                         p['b_gate'].astype(F32), p['w_branch'].astype(BF16))
    return matmul(merged, p['w_out'].astype(BF16), out_dtype=F32, res=x)


def _cross_block(x, mem, p):
    m_n = rmsnorm_rows(mem, p['ln_mem'])
    k_raw = matmul(m_n, p['w_xk'].astype(BF16), out_dtype=F32)
    v = matmul(m_n, p['w_xv'].astype(BF16), out_dtype=BF16)
    return cross_attention(x, p['ln_xattn'], p['w_xq'].astype(BF16), p['g_q_x'], k_raw, p['g_k_x'], v,
                           p['w_xo'].astype(BF16))


def _ffn_block(x, g, wg, wu, wd):
    act = glu_up(rmsnorm_rows(x, g), wg.astype(BF16), wu.astype(BF16))
    f = wd.shape[0]
    tk = max(c for c in range(256, 2049, 256) if f % c == 0)
    return matmul(act, wd.astype(BF16), out_dtype=F32, res=x, tn=1024, tk=tk)


def _moe_block(x, g, w_router, wg, wu, wd):
    sel = router_top2(x, g, w_router)
    src, pos, tile_expert = _moe_plan(sel)
    hs = gather_norm(x, g, src)
    act = _grouped_call(_glu_grouped_kernel, "moe_glu_up", hs, [wg.astype(BF16), wu.astype(BF16)],
                        tile_expert, _pick(wg.shape[2], (256, 128)), BF16)
    y = _grouped_call(_mm_grouped_kernel, "moe_down", act, [wd.astype(BF16)], tile_expert,
                      _pick(wd.shape[2], (1024, 512, 256, 128)), F32)
    return moe_combine(x, sel, y, pos)


def kernel(x, mem, ln_mix, w_in, g_cq, g_ckv, w_uq, w_ukv, g_q_mla, g_k_mla, g_q_dil, g_k_dil, g_q_dsa, g_k_dsa, w_gate_a, w_gate_b, b_gate, w_branch, w_out, ln_xattn, ln_mem, w_xq, w_xk, w_xv, g_q_x, g_k_x, w_xo, ln_ffn, w_ff_gate, w_ff_up, w_ff_down, w_router, w_e_gate, w_e_up, w_e_down):
    b, s, d = x.shape
    per_layer = dict(w_in=w_in, g_cq=g_cq, g_ckv=g_ckv, w_uq=w_uq, w_ukv=w_ukv, g_q_mla=g_q_mla,
                     g_k_mla=g_k_mla, g_q_dil=g_q_dil, g_k_dil=g_k_dil, g_q_dsa=g_q_dsa, g_k_dsa=g_k_dsa,
                     w_gate_a=w_gate_a, w_gate_b=w_gate_b, b_gate=b_gate, w_branch=w_branch, w_out=w_out,
                     ln_xattn=ln_xattn, ln_mem=ln_mem, w_xq=w_xq, w_xk=w_xk, w_xv=w_xv, g_q_x=g_q_x,
                     g_k_x=g_k_x, w_xo=w_xo)
    tabs = (_rope_tables(s, ROT_DIM, HEAD_DIM), _rope_tables(s, IDX_ROT, IDX_DIM),
            _rope_tables(s, MLA_ROPE, HEAD_DIM))
    depth = ln_mix.shape[0]
    outs = []
    for bi in range(b):
        xb, mb = x[bi], mem[bi]
        for i in range(depth):
            p = {name: val[i] for name, val in per_layer.items()}
            xb = _token_mixer(xb, rmsnorm_rows(xb, ln_mix[i]), p, tabs)
            xb = _cross_block(xb, mb, p)
            j = i // 2
            if i % 2 == 0:
                xb = _ffn_block(xb, ln_ffn[i], w_ff_gate[j], w_ff_up[j], w_ff_down[j])
            else:
                xb = _moe_block(xb, ln_ffn[i], w_router[j], w_e_gate[j], w_e_up[j], w_e_down[j])
        outs.append(xb)
    return outs[0][None] if b == 1 else jnp.stack(outs, axis=0)
```

```python
import functools
import math

import numpy as np
import jax
import jax.numpy as jnp
from jax import lax
from jax.experimental import pallas as pl
from jax.experimental.pallas import tpu as pltpu

F32 = jnp.float32
BF16 = jnp.bfloat16
I32 = jnp.int32

N_BRANCHES = 4
HEAD_DIM = 128
N_HEADS = 8
BRANCH_WIDTH = N_HEADS * HEAD_DIM
ROT_DIM = HEAD_DIM // 4
ROPE_THETA = 500000.0
RMS_EPS = 1e-6
NEG = -1e30
GATE_RANK = 256
MLA_Q_LORA = 896
MLA_KV_LORA = 256
MLA_NOPE = 128
MLA_ROPE = 64
MLA_QK = MLA_NOPE + MLA_ROPE
MLA_PAD = 256
DIL_PATTERNS = ((128, 1), (512, 4), (2048, 16))
DSA_TOPK = 256
IDX_HEADS = 8
IDX_DIM = 64
IDX_ROT = IDX_DIM // 4
MEM_HEADS = 4
MEM_HEAD_DIM = 128
MEM_WIDTH = MEM_HEADS * MEM_HEAD_DIM
N_EXPERTS = 8
MOE_TOP_K = 2
INT_MIN = -2 ** 31
LOG2E = math.log2(math.e)

ATT_T = 512
SB_T = 256
DSA_TQ = 128
COUNT_ROWS = 64
VT_ROWS = HEAD_DIM
SB_EXIT = -105.0

V7X_VMEM_LIMIT_BYTES = 56 * 1024 * 1024
LANES = 128

NT_DIMS = (((1,), (1,)), ((), ()))


def _cparams(n_axes):
    return pltpu.CompilerParams(dimension_semantics=("arbitrary",) * n_axes,
                                vmem_limit_bytes=V7X_VMEM_LIMIT_BYTES)


def _pick(n, candidates):
    for c in candidates:
        if n % c == 0:
            return c
    return n


def _rmsnorm_kernel(x_ref, g_ref, o_ref):
    x = x_ref[...].astype(F32)
    ms = jnp.mean(x * x, axis=-1, keepdims=True)
    o_ref[...] = (x * lax.rsqrt(ms + RMS_EPS) * g_ref[...]).astype(o_ref.dtype)


def rmsnorm_rows(x, g, out_dtype=BF16):
    m, d = x.shape
    tm = _pick(m, (512, 256, 128))
    return pl.pallas_call(
        _rmsnorm_kernel, grid=(m // tm,),
        in_specs=[pl.BlockSpec((tm, d), lambda i: (i, 0)), pl.BlockSpec((1, d), lambda i: (0, 0))],
        out_specs=pl.BlockSpec((tm, d), lambda i: (i, 0)),
        out_shape=jax.ShapeDtypeStruct((m, d), out_dtype),
        compiler_params=_cparams(1), name="rmsnorm_rows",
    )(x, g.reshape(1, d).astype(F32))


def _mm_kernel(*refs, nk, has_res):
    a_ref, b_ref = refs[0], refs[1]
    r_ref = refs[2] if has_res else None
    o_ref = refs[2 + has_res]
    if nk == 1:
        part = jnp.dot(a_ref[...], b_ref[...], preferred_element_type=F32)
        if has_res:
            part = r_ref[...] + part
        o_ref[...] = part.astype(o_ref.dtype)
        return
    acc_ref = refs[3 + has_res]
    k = pl.program_id(2)

    @pl.when(k == 0)
    def _():
        acc_ref[...] = jnp.zeros(acc_ref.shape, F32)

    acc_ref[...] += jnp.dot(a_ref[...], b_ref[...], preferred_element_type=F32)

    @pl.when(k == nk - 1)
    def _():
        res = acc_ref[...]
        if has_res:
            res = r_ref[...] + res
        o_ref[...] = res.astype(o_ref.dtype)


def matmul(a, b, *, out_dtype, res=None, tm=None, tn=None, tk=None):
    m, kdim = a.shape
    n = b.shape[1]
    tm = tm or _pick(m, (1024, 512, 256, 128))
    tn = tn or _pick(n, (512, 256, 128))
    tk = tk or (kdim if kdim <= 4096 else _pick(kdim, (2048, 1024, 512)))
    nk = kdim // tk
    in_specs = [pl.BlockSpec((tm, tk), lambda i, j, k: (i, k)),
                pl.BlockSpec((tk, tn), lambda i, j, k: (k, j))]
    args = [a, b]
    if res is not None:
        in_specs.append(pl.BlockSpec((tm, tn), lambda i, j, k: (i, j)))
        args.append(res)
    scratch = [pltpu.VMEM((tm, tn), F32)] if nk > 1 else []
    return pl.pallas_call(
        functools.partial(_mm_kernel, nk=nk, has_res=res is not None),
        grid=(m // tm, n // tn, nk),
        in_specs=in_specs,
        out_specs=pl.BlockSpec((tm, tn), lambda i, j, k: (i, j)),
        out_shape=jax.ShapeDtypeStruct((m, n), out_dtype),
        scratch_shapes=scratch,
        compiler_params=_cparams(3), name="matmul",
    )(*args)


def _glu_kernel(a_ref, wg_ref, wu_ref, o_ref):
    a = a_ref[...]
    g = jnp.dot(a, wg_ref[...], preferred_element_type=F32)
    u = jnp.dot(a, wu_ref[...], preferred_element_type=F32)
    o_ref[...] = (g * jax.nn.sigmoid(g) * u).astype(o_ref.dtype)


def glu_up(a, wg, wu):
    m, kdim = a.shape
    f = wg.shape[1]
    tm = _pick(m, (1024, 512, 256, 128))
    tn = _pick(f, (512, 256, 128))
    w_spec = pl.BlockSpec((kdim, tn), lambda i, j: (0, j))
    return pl.pallas_call(
        _glu_kernel, grid=(m // tm, f // tn),
        in_specs=[pl.BlockSpec((tm, kdim), lambda i, j: (i, 0)), w_spec, w_spec],
        out_specs=pl.BlockSpec((tm, tn), lambda i, j: (i, j)),
        out_shape=jax.ShapeDtypeStruct((m, f), BF16),
        compiler_params=_cparams(2), name="glu_up",
    )(a, wg, wu)


MOE_TM = 512


def _glu_grouped_kernel(te_ref, a_ref, wg_ref, wu_ref, o_ref):
    _glu_kernel(a_ref, wg_ref, wu_ref, o_ref)


def _mm_grouped_kernel(te_ref, a_ref, w_ref, o_ref):
    o_ref[...] = jnp.dot(a_ref[...], w_ref[...], preferred_element_type=F32).astype(o_ref.dtype)


def _grouped_call(kernel, name, a, weights, tile_expert, tn, out_dtype):
    r, kdim = a.shape
    n = weights[0].shape[2]
    w_spec = pl.BlockSpec((None, kdim, tn), lambda j, i, te: (te[i], 0, j))
    grid_spec = pltpu.PrefetchScalarGridSpec(
        num_scalar_prefetch=1, grid=(n // tn, r // MOE_TM),
        in_specs=[pl.BlockSpec((MOE_TM, kdim), lambda j, i, te: (i, 0))] + [w_spec] * len(weights),
        out_specs=pl.BlockSpec((MOE_TM, tn), lambda j, i, te: (i, j)))
    return pl.pallas_call(
        kernel, grid_spec=grid_spec, out_shape=jax.ShapeDtypeStruct((r, n), out_dtype),
        compiler_params=_cparams(2), name=name,
    )(tile_expert, a, *weights)


def _row_gather(idx_ref, base, src_hbm, dst, sem, n, *, wait):
    def body(r, c):
        copy = pltpu.make_async_copy(src_hbm.at[pl.ds(idx_ref[base + r], 1)], dst.at[pl.ds(r, 1)], sem)
        if wait:
            copy.wait()
        else:
            copy.start()
        return c

    lax.fori_loop(0, n, body, 0, unroll=8)


def _gather_norm_kernel(src_ref, x_hbm, g_ref, o_ref, buf, sem):
    tm = o_ref.shape[0]
    base = pl.program_id(0) * tm
    _row_gather(src_ref, base, x_hbm, buf, sem, tm, wait=False)
    _row_gather(src_ref, base, x_hbm, buf, sem, tm, wait=True)
    x = buf[...]
    ms = jnp.mean(x * x, axis=-1, keepdims=True)
    o_ref[...] = (x * lax.rsqrt(ms + RMS_EPS) * g_ref[...]).astype(o_ref.dtype)


def gather_norm(x, g, src):
    d = x.shape[1]
    r = src.shape[0]
    grid_spec = pltpu.PrefetchScalarGridSpec(
        num_scalar_prefetch=1, grid=(r // MOE_TM,),
        in_specs=[pl.BlockSpec(memory_space=pl.ANY), pl.BlockSpec((1, d), lambda i, s: (0, 0))],
        out_specs=pl.BlockSpec((MOE_TM, d), lambda i, s: (i, 0)),
        scratch_shapes=[pltpu.VMEM((MOE_TM, d), F32), pltpu.SemaphoreType.DMA(())])
    return pl.pallas_call(
        _gather_norm_kernel, grid_spec=grid_spec, out_shape=jax.ShapeDtypeStruct((r, d), BF16),
        compiler_params=_cparams(1), name="gather_norm",
    )(src, x, g.reshape(1, d).astype(F32))


def _moe_combine_kernel(pos_ref, x_ref, sel_ref, y_hbm, o_ref, buf0, buf1, sems):
    tm = o_ref.shape[0]
    base0 = pl.program_id(0) * tm
    base1 = pos_ref.shape[0] // 2 + base0
    for wait in (False, True):
        _row_gather(pos_ref, base0, y_hbm, buf0, sems.at[0], tm, wait=wait)
        _row_gather(pos_ref, base1, y_hbm, buf1, sems.at[1], tm, wait=wait)
    sel = sel_ref[...]
    o_ref[...] = x_ref[...] + sel[:, 2:3] * buf0[...] + sel[:, 3:4] * buf1[...]


def moe_combine(x, sel, y, pos):
    s, d = x.shape
    tm = _pick(s, (256, 128))
    grid_spec = pltpu.PrefetchScalarGridSpec(
        num_scalar_prefetch=1, grid=(s // tm,),
        in_specs=[pl.BlockSpec((tm, d), lambda i, p: (i, 0)), pl.BlockSpec((tm, LANES), lambda i, p: (i, 0)),
                  pl.BlockSpec(memory_space=pl.ANY)],
        out_specs=pl.BlockSpec((tm, d), lambda i, p: (i, 0)),
        scratch_shapes=[pltpu.VMEM((tm, d), F32), pltpu.VMEM((tm, d), F32), pltpu.SemaphoreType.DMA((2,))])
    return pl.pallas_call(
        _moe_combine_kernel, grid_spec=grid_spec, out_shape=jax.ShapeDtypeStruct((s, d), F32),
        compiler_params=_cparams(1), name="moe_combine",
    )(pos.reshape(-1), x, sel, y)


def _moe_plan(sel):
    s = sel.shape[0]
    e = sel[:, :MOE_TOP_K].astype(I32).T.reshape(-1)
    onehot = (e[:, None] == jnp.arange(N_EXPERTS, dtype=I32)[None, :]).astype(I32)
    rank = jnp.take_along_axis(jnp.cumsum(onehot, axis=0), e[:, None], axis=1)[:, 0] - 1
    padded = (jnp.sum(onehot, axis=0) + MOE_TM - 1) // MOE_TM * MOE_TM
    ends = jnp.cumsum(padded)
    pos = (ends - padded)[e] + rank
    n_rows = MOE_TOP_K * s + N_EXPERTS * MOE_TM
    token = jnp.tile(jnp.arange(s, dtype=I32), MOE_TOP_K)
    src = jnp.zeros((n_rows,), I32).at[pos].set(token)
    tile_start = jnp.arange(n_rows // MOE_TM, dtype=I32) * MOE_TM
    tile_expert = jnp.minimum(jnp.searchsorted(ends, tile_start, side='right'), N_EXPERTS - 1).astype(I32)
    return src, pos.reshape(MOE_TOP_K, s).astype(I32), tile_expert


def _router_kernel(x_ref, g_ref, w_ref, o_ref):
    x = x_ref[...]
    ms = jnp.mean(x * x, axis=-1, keepdims=True)
    h = x * lax.rsqrt(ms + RMS_EPS) * g_ref[...]
    logits = jnp.dot(h, w_ref[...], preferred_element_type=F32, precision=lax.Precision.HIGHEST)
    lane = lax.broadcasted_iota(I32, logits.shape, 1).astype(F32)
    logits = jnp.where(lane < N_EXPERTS, logits, -jnp.inf)
    m1 = jnp.max(logits, axis=1, keepdims=True)
    i1 = jnp.min(jnp.where(logits == m1, lane, float(LANES)), axis=1, keepdims=True)
    rest = jnp.where(lane == i1, -jnp.inf, logits)
    m2 = jnp.max(rest, axis=1, keepdims=True)
    i2 = jnp.min(jnp.where(rest == m2, lane, float(LANES)), axis=1, keepdims=True)
    e2 = jnp.exp(m2 - m1)
    w1 = 1.0 / (1.0 + e2)
    w2 = e2 / (1.0 + e2)
    o_ref[...] = (jnp.where(lane == 0.0, i1, 0.0) + jnp.where(lane == 1.0, i2, 0.0)
                  + jnp.where(lane == 2.0, w1, 0.0) + jnp.where(lane == 3.0, w2, 0.0))


def router_top2(x, g, w_router):
    m, d = x.shape
    tm = _pick(m, (256, 128))
    w = jnp.zeros((d, LANES), F32).at[:, :N_EXPERTS].set(w_router.astype(F32))
    return pl.pallas_call(
        _router_kernel, grid=(m // tm,),
        in_specs=[pl.BlockSpec((tm, d), lambda i: (i, 0)), pl.BlockSpec((1, d), lambda i: (0, 0)),
                  pl.BlockSpec((d, LANES), lambda i: (0, 0))],
        out_specs=pl.BlockSpec((tm, LANES), lambda i: (i, 0)),
        out_shape=jax.ShapeDtypeStruct((m, LANES), F32),
        compiler_params=_cparams(1), name="router_top2",
    )(x, g.reshape(1, d).astype(F32), w)


def _rope(x, tab_ref, half):
    w = x.shape[-1]
    return (x * tab_ref[0] + pltpu.roll(x, w - half, 1) * tab_ref[1]
            + pltpu.roll(x, half, 1) * tab_ref[2])


def _rope_tables(seq, rot_dim, period):
    half = rot_dim // 2
    inv_freq = ROPE_THETA ** (-jnp.arange(0, rot_dim, 2, dtype=F32) / rot_dim)
    ang = jnp.arange(seq, dtype=F32)[:, None] * inv_freq[None, :]
    cos, sin = jnp.cos(ang), jnp.sin(ang)
    ones = jnp.ones((seq, period - rot_dim), F32)
    zeros = jnp.zeros((seq, period - rot_dim), F32)
    zh = jnp.zeros((seq, half), F32)
    c = jnp.concatenate([cos, cos, ones], axis=1)
    sa = jnp.concatenate([-sin, zh, zeros], axis=1)
    sb = jnp.concatenate([zh, sin, zeros], axis=1)
    rep = LANES // period
    return jnp.stack([jnp.tile(c, (1, rep)), jnp.tile(sa, (1, rep)), jnp.tile(sb, (1, rep))], axis=0)


def _tab_spec(tm):
    return pl.BlockSpec((3, tm, LANES), lambda i: (0, i, 0))


def _row_spec(tm, w):
    return pl.BlockSpec((tm, w), lambda i: (i, 0))


def _const_spec(w):
    return pl.BlockSpec((1, w), lambda i: (0, 0))


def _prep_latent_kernel(p_ref, gq_ref, gkv_ref, cq_ref, ckv_ref, kpe_ref):
    cq = p_ref[:, :MLA_Q_LORA]
    ms = jnp.mean(cq * cq, axis=-1, keepdims=True)
    cq_ref[...] = (cq * lax.rsqrt(ms + RMS_EPS) * gq_ref[...]).astype(cq_ref.dtype)
    ckv = p_ref[:, MLA_Q_LORA:MLA_Q_LORA + MLA_KV_LORA]
    ms = jnp.mean(ckv * ckv, axis=-1, keepdims=True)
    ckv_ref[...] = (ckv * lax.rsqrt(ms + RMS_EPS) * gkv_ref[...]).astype(ckv_ref.dtype)
    kpe_ref[...] = p_ref[:, MLA_Q_LORA + MLA_KV_LORA:]


def prep_latent(p, g_cq, g_ckv):
    s = p.shape[0]
    tm = _pick(s, (512, 256, 128))
    return pl.pallas_call(
        _prep_latent_kernel, grid=(s // tm,),
        in_specs=[_row_spec(tm, p.shape[1]), _const_spec(MLA_Q_LORA), _const_spec(MLA_KV_LORA)],
        out_specs=[_row_spec(tm, MLA_Q_LORA), _row_spec(tm, MLA_KV_LORA), _row_spec(tm, LANES)],
        out_shape=[jax.ShapeDtypeStruct((s, MLA_Q_LORA), BF16),
                   jax.ShapeDtypeStruct((s, MLA_KV_LORA), BF16),
                   jax.ShapeDtypeStruct((s, LANES), F32)],
        compiler_params=_cparams(1), name="prep_latent",
    )(p, g_cq.reshape(1, -1), g_ckv.reshape(1, -1))


def _prep_mla_kernel(qup_ref, kvup_ref, kpe_ref, gq_ref, gk_ref, tab_ref, qT_ref, k_ref, vT_ref):
    half = MLA_ROPE // 2
    gqn, gqr = gq_ref[:, :LANES], gq_ref[:, LANES:]
    gkn, gkr = gk_ref[:, :LANES], gk_ref[:, LANES:]
    kpe = kpe_ref[...]
    kpe_ss = jnp.sum(kpe * kpe, axis=-1, keepdims=True)
    kr_base = _rope(kpe * gkr, tab_ref, half)
    scale = LOG2E * MLA_QK ** -0.5
    for h in range(N_HEADS):
        qn = qup_ref[:, h * MLA_PAD:h * MLA_PAD + LANES]
        qr = qup_ref[:, h * MLA_PAD + LANES:(h + 1) * MLA_PAD]
        ss = jnp.sum(qn * qn, axis=-1, keepdims=True) + jnp.sum(qr * qr, axis=-1, keepdims=True)
        r = lax.rsqrt(ss * (1.0 / MLA_QK) + RMS_EPS) * scale
        qT_ref[h, :LANES, :] = (qn * r * gqn).T.astype(BF16)
        qT_ref[h, LANES:, :] = _rope(qr * r * gqr, tab_ref, half).T.astype(BF16)
        kn = kvup_ref[:, h * LANES:(h + 1) * LANES]
        ss = jnp.sum(kn * kn, axis=-1, keepdims=True) + kpe_ss
        r = lax.rsqrt(ss * (1.0 / MLA_QK) + RMS_EPS)
        k_ref[:, h * MLA_PAD:h * MLA_PAD + LANES] = (kn * r * gkn).astype(BF16)
        k_ref[:, h * MLA_PAD + LANES:(h + 1) * MLA_PAD] = (kr_base * r).astype(BF16)
        vT_ref[h] = _vT_bf16(kvup_ref[:, BRANCH_WIDTH + h * LANES:BRANCH_WIDTH + (h + 1) * LANES])


def _qT_spec(d, t):
    return pl.BlockSpec((N_HEADS, d, t), lambda i: (0, 0, i))


def _vT_spec(t, rows):
    return pl.BlockSpec((N_HEADS, None, rows, t), lambda i: (0, i, 0, 0))


def _qkv_shapes(s, dq, wk, t, rows):
    return [jax.ShapeDtypeStruct((N_HEADS, dq, s), BF16), jax.ShapeDtypeStruct((s, wk), BF16),
            jax.ShapeDtypeStruct((N_HEADS, s // t, rows, t), BF16)]


def _vT_bf16(v):
    return v.T.astype(BF16)


def prep_mla(qup, kvup, kpe, g_q, g_k, tab):
    s = qup.shape[0]
    t = ATT_T
    pad = lambda g: jnp.zeros((1, MLA_PAD), F32).at[0, :MLA_QK].set(g)
    wq = N_HEADS * MLA_PAD
    return pl.pallas_call(
        _prep_mla_kernel, grid=(s // t,),
        in_specs=[_row_spec(t, wq), _row_spec(t, 2 * BRANCH_WIDTH), _row_spec(t, LANES),
                  _const_spec(MLA_PAD), _const_spec(MLA_PAD), _tab_spec(t)],
        out_specs=[_qT_spec(MLA_PAD, t), _row_spec(t, wq), _vT_spec(t, VT_ROWS)],
        out_shape=_qkv_shapes(s, MLA_PAD, wq, t, VT_ROWS),
        compiler_params=_cparams(1), name="prep_mla",
    )(qup, kvup, kpe, pad(g_q), pad(g_k), tab)


def _head_norm_rope(x, g, tab_ref, scale):
    ms = jnp.mean(x * x, axis=-1, keepdims=True)
    y = x * (lax.rsqrt(ms + RMS_EPS) * scale) * g
    return _rope(y, tab_ref, ROT_DIM // 2)


def _prep_sb_kernel(p_ref, qT_ref, k_ref, vT_ref, kn_ref):
    for h in range(N_HEADS):
        sl = slice(h * LANES, (h + 1) * LANES)
        qT_ref[h] = (p_ref[:, sl] * HEAD_DIM ** -0.5).T.astype(BF16)
        vT_ref[h] = p_ref[:, 2 * BRANCH_WIDTH + h * LANES:2 * BRANCH_WIDTH + (h + 1) * LANES].T.astype(BF16)
        kb = p_ref[:, BRANCH_WIDTH + h * LANES:BRANCH_WIDTH + (h + 1) * LANES].astype(BF16)
        k_ref[:, sl] = kb
        kf = kb.astype(F32)
        kn_ref[h] = jnp.zeros((8, LANES), F32) + jnp.max(jnp.sum(kf * kf, axis=1, keepdims=True))


def prep_sb(p):
    s = p.shape[0]
    t = SB_T
    return pl.pallas_call(
        _prep_sb_kernel, grid=(s // t,),
        in_specs=[_row_spec(t, 3 * BRANCH_WIDTH)],
        out_specs=[_qT_spec(HEAD_DIM, t), _row_spec(t, BRANCH_WIDTH), _vT_spec(t, HEAD_DIM),
                   pl.BlockSpec((N_HEADS, None, 8, LANES), lambda i: (0, i, 0, 0))],
        out_shape=_qkv_shapes(s, HEAD_DIM, BRANCH_WIDTH, t, HEAD_DIM)
        + [jax.ShapeDtypeStruct((N_HEADS, s // t, 8, LANES), F32)],
        compiler_params=_cparams(1), name="prep_sb",
    )(p)


def _prep_dil_kernel(p_ref, gq_ref, gk_ref, tab_ref, qT_ref, k_ref, vT_ref):
    gq, gk = gq_ref[...], gk_ref[...]
    for h in range(N_HEADS):
        sl = slice(h * LANES, (h + 1) * LANES)
        qT_ref[h] = _head_norm_rope(p_ref[:, sl], gq, tab_ref, LOG2E * HEAD_DIM ** -0.5).T.astype(BF16)
        ksl = slice(BRANCH_WIDTH + h * LANES, BRANCH_WIDTH + (h + 1) * LANES)
        k_ref[:, sl] = _head_norm_rope(p_ref[:, ksl], gk, tab_ref, 1.0).astype(BF16)
        vT_ref[h] = _vT_bf16(p_ref[:, 2 * BRANCH_WIDTH + h * LANES:2 * BRANCH_WIDTH + (h + 1) * LANES])


def prep_dil(p, g_q, g_k, tab):
    s = p.shape[0]
    t = ATT_T
    return pl.pallas_call(
        _prep_dil_kernel, grid=(s // t,),
        in_specs=[_row_spec(t, 3 * BRANCH_WIDTH), _const_spec(LANES), _const_spec(LANES), _tab_spec(t)],
        out_specs=[_qT_spec(HEAD_DIM, t), _row_spec(t, BRANCH_WIDTH), _vT_spec(t, VT_ROWS)],
        out_shape=_qkv_shapes(s, HEAD_DIM, BRANCH_WIDTH, t, VT_ROWS),
        compiler_params=_cparams(1), name="prep_dil",
    )(p, g_q.reshape(1, -1), g_k.reshape(1, -1), tab)


def _prep_dsa_kernel(p_ref, gq_ref, gk_ref, tab_ref, qT_ref, k_ref, vT_ref):
    gq, gk = gq_ref[...], gk_ref[...]
    for h in range(N_HEADS):
        sl = slice(h * LANES, (h + 1) * LANES)
        qT = _head_norm_rope(p_ref[:, sl], gq, tab_ref, LOG2E * HEAD_DIM ** -0.5).T.astype(BF16)
        for b in range(p_ref.shape[0] // DSA_TQ):
            qT_ref[b, :, sl] = qT[:, b * DSA_TQ:(b + 1) * DSA_TQ]
    k_ref[...] = _head_norm_rope(p_ref[:, BRANCH_WIDTH:BRANCH_WIDTH + LANES], gk, tab_ref, 1.0).astype(BF16)
    vT_ref[...] = _vT_bf16(p_ref[:, BRANCH_WIDTH + LANES:])


def prep_dsa(p, g_q, g_k, tab):
    s = p.shape[0]
    t = ATT_T
    nqb = t // DSA_TQ
    return pl.pallas_call(
        _prep_dsa_kernel, grid=(s // t,),
        in_specs=[_row_spec(t, BRANCH_WIDTH + 2 * LANES), _const_spec(LANES), _const_spec(LANES), _tab_spec(t)],
        out_specs=[pl.BlockSpec((nqb, HEAD_DIM, BRANCH_WIDTH), lambda i: (i, 0, 0)), _row_spec(t, LANES),
                   pl.BlockSpec((None, VT_ROWS, t), lambda i: (i, 0, 0))],
        out_shape=[jax.ShapeDtypeStruct((s // DSA_TQ, HEAD_DIM, BRANCH_WIDTH), BF16),
                   jax.ShapeDtypeStruct((s, LANES), BF16),
                   jax.ShapeDtypeStruct((s // t, VT_ROWS, t), BF16)],
        compiler_params=_cparams(1), name="prep_dsa",
    )(p, g_q.reshape(1, -1), g_k.reshape(1, -1), tab)


def _prep_idx_kernel(p_ref, tab_ref, qiT_ref, ki_ref, wiT_ref):
    half = IDX_ROT // 2
    t = p_ref.shape[0]
    nqb = t // DSA_TQ
    lane = lax.broadcasted_iota(I32, (t, LANES), 1)
    first = lane < IDX_DIM
    zero = jnp.zeros((t, LANES), F32)
    for b in range(IDX_HEADS // 2):
        x = _rope(p_ref[:, b * LANES:(b + 1) * LANES], tab_ref, half) * (IDX_DIM ** -0.5)
        hi = x.astype(BF16).astype(F32)
        lo = x - hi
        rhi = pltpu.roll(hi, IDX_DIM, 1)
        for hh, (a0, a1) in enumerate(((jnp.where(first, hi, pltpu.roll(lo, IDX_DIM, 1)), jnp.where(first, hi, zero)),
                                       (jnp.where(first, rhi, lo), jnp.where(first, rhi, zero)))):
            h = 2 * b + hh
            a0T, a1T = a0.T.astype(BF16), a1.T.astype(BF16)
            for qb in range(nqb):
                qs = slice(qb * DSA_TQ, (qb + 1) * DSA_TQ)
                qiT_ref[qb, :LANES, h * DSA_TQ:(h + 1) * DSA_TQ] = a0T[:, qs]
                qiT_ref[qb, LANES:, h * DSA_TQ:(h + 1) * DSA_TQ] = a1T[:, qs]
    kx = _rope(p_ref[:, IDX_HEADS * IDX_DIM:IDX_HEADS * IDX_DIM + LANES], tab_ref, half)
    hi = kx.astype(BF16).astype(F32)
    lo = kx - hi
    ki_ref[:, :LANES] = jnp.where(first, hi, pltpu.roll(hi, IDX_DIM, 1)).astype(BF16)
    ki_ref[:, LANES:] = jnp.where(first, lo, zero).astype(BF16)
    wT = (p_ref[:, IDX_HEADS * IDX_DIM + LANES:] * (IDX_HEADS ** -0.5)).T
    for qb in range(nqb):
        wiT_ref[qb] = wT[:IDX_HEADS, qb * DSA_TQ:(qb + 1) * DSA_TQ]


def prep_idx(p, tab):
    s = p.shape[0]
    t = ATT_T
    nqb = t // DSA_TQ
    return pl.pallas_call(
        _prep_idx_kernel, grid=(s // t,),
        in_specs=[_row_spec(t, IDX_HEADS * IDX_DIM + 2 * LANES), _tab_spec(t)],
        out_specs=[pl.BlockSpec((nqb, 2 * LANES, IDX_HEADS * DSA_TQ), lambda i: (i, 0, 0)),
                   _row_spec(t, 2 * LANES),
                   pl.BlockSpec((nqb, IDX_HEADS, DSA_TQ), lambda i: (i, 0, 0))],
        out_shape=[jax.ShapeDtypeStruct((s // DSA_TQ, 2 * LANES, IDX_HEADS * DSA_TQ), BF16),
                   jax.ShapeDtypeStruct((s, 2 * LANES), BF16),
                   jax.ShapeDtypeStruct((s // DSA_TQ, IDX_HEADS, DSA_TQ), F32)],
        compiler_params=_cparams(1), name="prep_idx",
    )(p, tab)


def _softmax_step(sT, vT, m_sc, acc_sc):
    l_sc, o_sc = acc_sc
    m_prev = m_sc[...]
    m_new = jnp.maximum(m_prev, jnp.max(sT, axis=0, keepdims=True))
    alpha = jnp.exp2(m_prev - m_new)
    p = jnp.exp2(sT - m_new)
    l_sc[...] = alpha * l_sc[...] + jnp.sum(p, axis=0, keepdims=True)
    o_sc[...] = alpha * o_sc[...] + jnp.dot(vT, p.astype(BF16), preferred_element_type=F32)
    m_sc[...] = m_new


def _softmax_init(m_sc, acc_sc):
    m_sc[...] = jnp.full(m_sc.shape, NEG, F32)
    for ref in acc_sc:
        ref[...] = jnp.zeros(ref.shape, F32)


def _softmax_result(acc_sc):
    l_sc, o_sc = acc_sc
    return o_sc[...] / l_sc[...]


def _pipelined_blocks(n_full, scores, consume, consume_last, s_sc):
    s_sc[0] = scores(0)

    def pair(jj, c):
        s_sc[1] = scores(2 * jj + 1)
        consume(s_sc[0], 2 * jj)
        s_sc[0] = scores(2 * jj + 2)
        consume(s_sc[1], 2 * jj + 1)
        return c

    lax.fori_loop(0, n_full // 2, pair, 0)

    @pl.when(n_full % 2 == 1)
    def _():
        s_sc[1] = scores(n_full)
        consume(s_sc[0], n_full - 1)
        consume_last(s_sc[1], n_full)

    @pl.when(n_full % 2 == 0)
    def _():
        consume_last(s_sc[0], n_full)


def _key_le_query(tk, tq):
    return lax.broadcasted_iota(I32, (tk, tq), 0) <= lax.broadcasted_iota(I32, (tk, tq), 1)


def _mla_kernel(qT_ref, k_ref, vT_ref, o_ref, s_sc, m_sc, l_sc, o_sc):
    acc_sc = (l_sc, o_sc)
    i = pl.program_id(1)
    t = qT_ref.shape[1]
    qT = qT_ref[...]
    _softmax_init(m_sc, acc_sc)
    scores = lambda j: jnp.dot(k_ref[j], qT, preferred_element_type=F32)
    step = lambda sT, j: _softmax_step(sT, vT_ref[j], m_sc, acc_sc)
    last = lambda sT, j: _softmax_step(jnp.where(_key_le_query(t, t), sT, NEG), vT_ref[j], m_sc, acc_sc)
    _pipelined_blocks(i, scores, step, last, s_sc)
    o_ref[...] = _softmax_result(acc_sc).T.astype(o_ref.dtype)


def _attn_scratch(tk, tq):
    return [pltpu.VMEM((2, tk, tq), F32), pltpu.VMEM((1, tq), F32), pltpu.VMEM((1, tq), F32),
            pltpu.VMEM((HEAD_DIM, tq), F32)]


def _head_qT_spec(d, t):
    return pl.BlockSpec((None, d, t), lambda h, i: (h, 0, i))


def _head_k_spec(nb, t, w):
    return pl.BlockSpec((nb, t, w), lambda h, i: (0, 0, h))


def _head_vT_spec(nb, t, rows=VT_ROWS):
    return pl.BlockSpec((None, nb, rows, t), lambda h, i: (h, 0, 0, 0))


def _head_out_spec(t):
    return pl.BlockSpec((t, HEAD_DIM), lambda h, i: (i, h))


def mla_attention(qT, k, vT):
    s = k.shape[0]
    t = ATT_T
    nb = s // t
    return pl.pallas_call(
        _mla_kernel, grid=(N_HEADS, nb),
        in_specs=[_head_qT_spec(MLA_PAD, t), _head_k_spec(nb, t, MLA_PAD), _head_vT_spec(nb, t)],
        out_specs=_head_out_spec(t),
        out_shape=jax.ShapeDtypeStruct((s, BRANCH_WIDTH), BF16),
        scratch_shapes=_attn_scratch(t, t),
        compiler_params=_cparams(2), name="mla_attention",
    )(qT, k.reshape(nb, t, -1), vT)


SB_SUB = 128


def _sb_kernel(qT_ref, k_ref, vT_ref, u_ref, kn_ref, o_ref, carry_sc, acc_sc):
    i = pl.program_id(1)
    t = qT_ref.shape[1]
    qT = qT_ref[...]
    u2 = u_ref[...]
    qf = qT.astype(F32)
    zbound = jnp.sqrt(jnp.sum(qf * qf, axis=0, keepdims=True) * jnp.max(kn_ref[...]))
    carry_sc[...] = jnp.zeros(carry_sc.shape, F32)
    acc_sc[...] = jnp.zeros(acc_sc.shape, F32)
    key = lax.broadcasted_iota(I32, (SB_SUB, t), 0)
    qry = lax.broadcasted_iota(I32, (SB_SUB, t), 1)

    def block(j, diag):
        zT = jnp.dot(k_ref[j], qT, preferred_element_type=F32)
        carry = carry_sc[...]
        parts = [None] * (t // SB_SUB)
        for c in reversed(range(t // SB_SUB)):
            zc = zT[c * SB_SUB:(c + 1) * SB_SUB]
            ls = jnp.minimum(-zc, 0.0) - jnp.log(1.0 + jnp.exp(-jnp.abs(zc)))
            if diag:
                past = key + c * SB_SUB < qry
                ls = jnp.where(past, ls, 0.0)
            hi = ls.astype(BF16)
            lo = (ls - hi.astype(F32)).astype(BF16)
            rev = jnp.dot(u2, jnp.concatenate([hi, lo], axis=0), preferred_element_type=F32) + carry
            a = jnp.exp(jnp.minimum(zc + rev, 0.0))
            if diag:
                a = jnp.where(past, a, 0.0)
            parts[c] = a.astype(BF16)
            carry = rev[0:1, :]
        carry_sc[...] = carry
        acc_sc[...] += jnp.dot(vT_ref[j], jnp.concatenate(parts, axis=0), preferred_element_type=F32)

    def all_underflow():
        return (jnp.max(carry_sc[...] + zbound) < SB_EXIT).astype(I32)

    block(i, True)

    def back(state):
        jj, _ = state
        block(i - 1 - jj, False)
        return jj + 1, all_underflow()

    lax.while_loop(lambda st: jnp.logical_and(st[0] < i, st[1] == 0), back, (jnp.int32(0), all_underflow()))
    o_ref[...] = acc_sc[...].T.astype(o_ref.dtype)


def sb_attention(qT, k, vT, kn):
    s = k.shape[0]
    t = SB_T
    nb = s // t
    tri = (np.arange(SB_SUB)[None, :] >= np.arange(SB_SUB)[:, None]).astype(np.float32)
    u2 = jnp.asarray(np.concatenate([tri, tri], axis=1), dtype=BF16)
    return pl.pallas_call(
        _sb_kernel, grid=(N_HEADS, nb),
        in_specs=[_head_qT_spec(HEAD_DIM, t), _head_k_spec(nb, t, HEAD_DIM), _head_vT_spec(nb, t, HEAD_DIM),
                  pl.BlockSpec((SB_SUB, 2 * SB_SUB), lambda h, i: (0, 0)),
                  pl.BlockSpec((None, nb, 8, LANES), lambda h, i: (h, 0, 0, 0))],
        out_specs=_head_out_spec(t),
        out_shape=jax.ShapeDtypeStruct((s, BRANCH_WIDTH), BF16),
        scratch_shapes=[pltpu.VMEM((1, t), F32), pltpu.VMEM((HEAD_DIM, t), F32)],
        compiler_params=_cparams(2), name="sb_attention",
    )(qT, k.reshape(nb, t, -1), vT, u2, kn)


def _dil_log_weights(t):
    span = max(w for w, _ in DIL_PATTERNS)
    nback = -(-span // t)
    d = np.arange(nback + 1)[:, None, None] * t + np.arange(t)[None, None, :] - np.arange(t)[None, :, None]
    mult = np.zeros(d.shape, np.float64)
    for window, dil in DIL_PATTERNS:
        mult += ((d >= 0) & (d <= window) & (d % dil == 0))
    return np.where(mult > 0, np.log2(np.maximum(mult, 1.0)), NEG).astype(np.float32)


def _dil_kernel(qT_ref, k_ref, vT_ref, w_ref, o_ref, s_sc, m_sc, l_sc, o_sc):
    acc_sc = (l_sc, o_sc)
    i = pl.program_id(1)
    nback = w_ref.shape[0] - 1
    qT = qT_ref[...]
    _softmax_init(m_sc, acc_sc)
    scores = lambda d: jnp.dot(k_ref[i - d], qT, preferred_element_type=F32)
    step = lambda sT, d: _softmax_step(sT + w_ref[d], vT_ref[i - d], m_sc, acc_sc)
    _pipelined_blocks(jnp.minimum(i, nback), scores, step, step, s_sc)
    o_ref[...] = _softmax_result(acc_sc).T.astype(o_ref.dtype)


def dil_attention(qT, k, vT):
    s = k.shape[0]
    t = ATT_T
    nb = s // t
    logw = jnp.asarray(_dil_log_weights(t))
    return pl.pallas_call(
        _dil_kernel, grid=(N_HEADS, nb),
        in_specs=[_head_qT_spec(HEAD_DIM, t), _head_k_spec(nb, t, HEAD_DIM), _head_vT_spec(nb, t),
                  pl.BlockSpec(logw.shape, lambda h, i: (0, 0, 0))],
        out_specs=_head_out_spec(t),
        out_shape=jax.ShapeDtypeStruct((s, BRANCH_WIDTH), BF16),
        scratch_shapes=_attn_scratch(t, t),
        compiler_params=_cparams(2), name="dil_attention",
    )(qT, k.reshape(nb, t, -1), vT, logw)


def _dsa_kernel(qsT_ref, qiT_ref, wiT_ref, ki_ref, k_ref, vT_ref, o_ref,
                key_sc, q1_sc, s_sc, m_sc, l_sc, o_sc, *, top_k, pos_bits):
    acc_sc = (l_sc, o_sc)
    i = pl.program_id(0)
    tq, tk = DSA_TQ, ATT_T
    nkb = (i * tq + tq + tk - 1) // tk
    qiT = qiT_ref[...]
    wiT = wiT_ref[...]
    row = lax.broadcasted_iota(I32, (tk, tq), 1) + i * tq
    col = lax.broadcasted_iota(I32, (tk, tq), 0)

    def idx_block(aT, j):
        idx = jnp.zeros((tk, tq), F32)
        for h in range(IDX_HEADS):
            idx = idx + jnp.maximum(aT[:, h * tq:(h + 1) * tq], 0.0) * wiT[h:h + 1, :]
        bits = pltpu.bitcast(idx + 0.0, I32)
        key = bits ^ ((bits >> 31) & 0x7FFFFFFF)
        key_sc[j] = jnp.where(col + j * tk <= row, key, INT_MIN)

    _pipelined_blocks(nkb - 1, lambda j: jnp.dot(ki_ref[j], qiT, preferred_element_type=F32),
                      idx_block, idx_block, s_sc)

    def count(hits):
        def body(j, acc):
            hit = hits(key_sc[j], j)
            return acc + jnp.sum(hit.reshape(tk // COUNT_ROWS, COUNT_ROWS, tq), axis=0)
        acc = lax.fori_loop(0, nkb, body, jnp.zeros((COUNT_ROWS, tq), F32))
        return jnp.sum(acc, axis=0, keepdims=True)

    def bit_body(b, carry):
        thr, cge = carry
        cand = thr + jnp.left_shift(jnp.int32(1), 31 - b)
        c = count(lambda kb, j: jnp.where(kb >= cand, 1.0, 0.0))
        ok = c >= top_k
        return jnp.where(ok, cand, thr), jnp.where(ok, c, cge)

    thr0 = jnp.full((1, tq), INT_MIN, I32)
    cge0 = jnp.zeros((1, tq), F32) + (nkb * tk).astype(F32)
    thr, cge = lax.fori_loop(0, 32, bit_body, (thr0, cge0))

    q1_sc[...] = jnp.full((1, tq), 2 ** 30, I32)

    @pl.when(jnp.max(cge) > top_k)
    def _():
        need = top_k - count(lambda kb, j: jnp.where(kb > thr, 1.0, 0.0))

        def pos_body(b, qpos):
            cand = qpos + jnp.left_shift(jnp.int32(1), pos_bits - 1 - b)
            g = count(lambda kb, j: jnp.where(kb == thr, jnp.where(col + j * tk < cand, 1.0, 0.0), 0.0))
            return jnp.where(g < need, cand, qpos)

        q1_sc[...] = lax.fori_loop(0, pos_bits, pos_body, jnp.zeros((1, tq), I32)) + 1

    q1 = q1_sc[...]
    qsT = qsT_ref[...]
    _softmax_init(m_sc, acc_sc)
    def att_block(sT, j):
        kb = key_sc[j]
        pos = col + j * tk
        bias = jnp.where(kb > thr, 0.0, jnp.where(kb == thr, jnp.where(pos < q1, 0.0, NEG), NEG))
        bias = jnp.where(pos <= row, bias, NEG)
        _softmax_step(sT + jnp.concatenate([bias] * N_HEADS, axis=1), vT_ref[j], m_sc, acc_sc)

    _pipelined_blocks(nkb - 1, lambda j: jnp.dot(k_ref[j], qsT, preferred_element_type=F32),
                      att_block, att_block, s_sc)
    o = _softmax_result(acc_sc)
    for h in range(N_HEADS):
        sl = slice(h * HEAD_DIM, (h + 1) * HEAD_DIM)
        o_ref[:, sl] = o[:, h * tq:(h + 1) * tq].T.astype(o_ref.dtype)


def dsa_attention(qsT, qiT, wiT, ki, k, vT):
    s = k.shape[0]
    tq, tk = DSA_TQ, ATT_T
    nkb = s // tk
    top_k = min(DSA_TOPK, s // 4)
    wq = N_HEADS * tq
    full3 = lambda a, b: pl.BlockSpec((nkb, a, b), lambda i: (0, 0, 0))
    return pl.pallas_call(
        functools.partial(_dsa_kernel, top_k=top_k, pos_bits=(s - 1).bit_length()),
        grid=(s // tq,),
        in_specs=[pl.BlockSpec((None, HEAD_DIM, wq), lambda i: (i, 0, 0)),
                  pl.BlockSpec((None, 2 * LANES, wq), lambda i: (i, 0, 0)),
                  pl.BlockSpec((None, IDX_HEADS, tq), lambda i: (i, 0, 0)),
                  full3(tk, 2 * LANES), full3(tk, HEAD_DIM), full3(VT_ROWS, tk)],
        out_specs=pl.BlockSpec((tq, BRANCH_WIDTH), lambda i: (i, 0)),
        out_shape=jax.ShapeDtypeStruct((s, BRANCH_WIDTH), BF16),
        scratch_shapes=[pltpu.VMEM((nkb, tk, tq), I32), pltpu.VMEM((1, tq), I32)] + _attn_scratch(tk, wq),
        compiler_params=_cparams(1), name="dsa_attention",
    )(qsT, qiT, wiT, ki.reshape(nkb, tk, -1), k.reshape(nkb, tk, -1), vT)


def _merge_kernel(g_ref, o0_ref, o1_ref, o2_ref, o3_ref, wgb_ref, b_ref, wbr_ref, out_ref):
    g = g_ref[...]
    acc = None
    for n, o_ref in enumerate((o0_ref, o1_ref, o2_ref, o3_ref)):
        gate = jax.nn.sigmoid(jnp.dot(g, wgb_ref[n], preferred_element_type=F32) + b_ref[n])
        val = gate * jnp.dot(o_ref[...], wbr_ref[n], preferred_element_type=F32)
        acc = val if acc is None else acc + val
    out_ref[...] = acc.astype(out_ref.dtype)


def gated_merge(g_lat, outs, w_gate_b, b_gate, w_branch):
    s = g_lat.shape[0]
    d = w_gate_b.shape[-1]
    tm = _pick(s, (1024, 512, 256, 128))
    tn = _pick(d, (512, 256, 128))
    o_spec = pl.BlockSpec((tm, BRANCH_WIDTH), lambda i, j: (i, 0))
    return pl.pallas_call(
        _merge_kernel, grid=(s // tm, d // tn),
        in_specs=[pl.BlockSpec((tm, GATE_RANK), lambda i, j: (i, 0)), o_spec, o_spec, o_spec, o_spec,
                  pl.BlockSpec((N_BRANCHES, GATE_RANK, tn), lambda i, j: (0, 0, j)),
                  pl.BlockSpec((N_BRANCHES, 1, tn), lambda i, j: (0, 0, j)),
                  pl.BlockSpec((N_BRANCHES, BRANCH_WIDTH, tn), lambda i, j: (0, 0, j))],
        out_specs=pl.BlockSpec((tm, tn), lambda i, j: (i, j)),
        out_shape=jax.ShapeDtypeStruct((s, d), BF16),
        compiler_params=_cparams(2), name="gated_merge",
    )(g_lat, *outs, w_gate_b, b_gate.reshape(N_BRANCHES, 1, d), w_branch)


def _cross_kernel(x_ref, g_ref, wq_ref, gq_ref, kraw_ref, gk_ref, v_ref, wo_ref, o_ref):
    x = x_ref[...]
    ms = jnp.mean(x * x, axis=-1, keepdims=True)
    h = (x * lax.rsqrt(ms + RMS_EPS) * g_ref[...]).astype(BF16)
    q = jnp.dot(h, wq_ref[...], preferred_element_type=F32)
    outs = []
    for hd in range(MEM_HEADS):
        sl = slice(hd * MEM_HEAD_DIM, (hd + 1) * MEM_HEAD_DIM)
        qh = q[:, sl]
        qh = qh * (lax.rsqrt(jnp.mean(qh * qh, axis=-1, keepdims=True) + RMS_EPS) * MEM_HEAD_DIM ** -0.5) * gq_ref[...]
        kh = kraw_ref[:, sl]
        kh = kh * lax.rsqrt(jnp.mean(kh * kh, axis=-1, keepdims=True) + RMS_EPS) * gk_ref[...]
        s = lax.dot_general(qh.astype(BF16), kh.astype(BF16), NT_DIMS, preferred_element_type=F32)
        p = jnp.exp(s - jnp.max(s, axis=1, keepdims=True))
        p = p / jnp.sum(p, axis=1, keepdims=True)
        outs.append(jnp.dot(p.astype(BF16), v_ref[:, sl], preferred_element_type=F32).astype(BF16))
    o = jnp.concatenate(outs, axis=1)
    o_ref[...] = x + jnp.dot(o, wo_ref[...], preferred_element_type=F32)


def cross_attention(x, g, w_xq, g_q, k_raw, g_k, v, w_xo):
    s, d = x.shape
    m = k_raw.shape[0]
    tm = _pick(s, (256, 128))
    c2 = lambda shape: pl.BlockSpec(shape, lambda i: (0, 0))
    return pl.pallas_call(
        _cross_kernel, grid=(s // tm,),
        in_specs=[_row_spec(tm, d), c2((1, d)), c2((d, MEM_WIDTH)), c2((1, MEM_HEAD_DIM)),
                  c2((m, MEM_WIDTH)), c2((1, MEM_HEAD_DIM)), c2((m, MEM_WIDTH)), c2((MEM_WIDTH, d))],
        out_specs=_row_spec(tm, d),
        out_shape=jax.ShapeDtypeStruct((s, d), F32),
        compiler_params=_cparams(1), name="cross_attention",
    )(x, g.reshape(1, d), w_xq, g_q.reshape(1, -1), k_raw, g_k.reshape(1, -1), v, w_xo)


def _split_w_in(w_in):
    cuts = np.cumsum([0, MLA_Q_LORA, MLA_KV_LORA, MLA_ROPE, BRANCH_WIDTH, BRANCH_WIDTH, BRANCH_WIDTH,
                      BRANCH_WIDTH, BRANCH_WIDTH, BRANCH_WIDTH, BRANCH_WIDTH, HEAD_DIM, HEAD_DIM,
                      IDX_HEADS * IDX_DIM, IDX_DIM, IDX_HEADS])
    seg = lambda a, b: w_in[:, cuts[a]:cuts[b]]
    zeros = lambda n: jnp.zeros((w_in.shape[0], n), w_in.dtype)
    w_lat = jnp.concatenate([seg(0, 3), zeros(LANES - MLA_ROPE)], axis=1)
    w_sb = seg(3, 6)
    w_dil = seg(6, 9)
    w_dsa = seg(9, 12)
    w_idx = jnp.concatenate([seg(12, 14), zeros(LANES - IDX_DIM), seg(14, 15), zeros(LANES - IDX_HEADS)], axis=1)
    return [w.astype(BF16) for w in (w_lat, w_sb, w_dil, w_dsa, w_idx)]


def _pad_mla_up(w_uq, w_ukv):
    r = w_uq.shape[0]
    wq = w_uq.reshape(r, N_HEADS, MLA_QK)
    wq = jnp.concatenate([wq, jnp.zeros((r, N_HEADS, MLA_PAD - MLA_QK), wq.dtype)], axis=2)
    wkv = w_ukv.reshape(w_ukv.shape[0], N_HEADS, 2 * HEAD_DIM)
    wkv = jnp.concatenate([wkv[:, :, :MLA_NOPE].reshape(-1, BRANCH_WIDTH),
                           wkv[:, :, MLA_NOPE:].reshape(-1, BRANCH_WIDTH)], axis=1)
    return wq.reshape(r, N_HEADS * MLA_PAD).astype(BF16), wkv.astype(BF16)


def _branches(h, p, tabs):
    tab_p, tab_i, tab_m = tabs
    w_lat, w_sb, w_dil, w_dsa, w_idx = _split_w_in(p['w_in'])
    w_uq, w_ukv = _pad_mla_up(p['w_uq'], p['w_ukv'])

    cq, ckv, kpe = prep_latent(matmul(h, w_lat, out_dtype=F32), p['g_cq'], p['g_ckv'])
    qT, k, vT = prep_mla(matmul(cq, w_uq, out_dtype=F32), matmul(ckv, w_ukv, out_dtype=F32), kpe,
                         p['g_q_mla'], p['g_k_mla'], tab_m)
    o_mla = mla_attention(qT, k, vT)
    o_sb = sb_attention(*prep_sb(matmul(h, w_sb, out_dtype=F32)))
    o_dil = dil_attention(*prep_dil(matmul(h, w_dil, out_dtype=F32), p['g_q_dil'], p['g_k_dil'], tab_p))
    qsT, ks, vsT = prep_dsa(matmul(h, w_dsa, out_dtype=F32), p['g_q_dsa'], p['g_k_dsa'], tab_p)
    qiT, ki, wiT = prep_idx(matmul(h, w_idx, out_dtype=F32), tab_i)
    o_dsa = dsa_attention(qsT, qiT, wiT, ki, ks, vsT)
    return o_mla, o_sb, o_dil, o_dsa


def _token_mixer(x, h, p, tabs):
    o_mla, o_sb, o_dil, o_dsa = _branches(h, p, tabs)
    g_lat = matmul(h, p['w_gate_a'].astype(BF16), out_dtype=BF16)
    merged = gated_merge(g_lat, (o_mla, o_sb, o_dil, o_dsa), p['w_gate_b'].astype(BF16),
                         p['b_gate'].astype(F32), p['w_branch'].astype(BF16))
    return matmul(merged, p['w_out'].astype(BF16), out_dtype=F32, res=x)


def _cross_block(x, mem, p):
    m_n = rmsnorm_rows(mem, p['ln_mem'])
    k_raw = matmul(m_n, p['w_xk'].astype(BF16), out_dtype=F32)
    v = matmul(m_n, p['w_xv'].astype(BF16), out_dtype=BF16)
    return cross_attention(x, p['ln_xattn'], p['w_xq'].astype(BF16), p['g_q_x'], k_raw, p['g_k_x'], v,
                           p['w_xo'].astype(BF16))


def _ffn_block(x, g, wg, wu, wd):
    act = glu_up(rmsnorm_rows(x, g), wg.astype(BF16), wu.astype(BF16))
    f = wd.shape[0]
    tk = max(c for c in range(256, 2049, 256) if f % c == 0)
    return matmul(act, wd.astype(BF16), out_dtype=F32, res=x, tn=1024, tk=tk)


def _moe_block(x, g, w_router, wg, wu, wd):
    sel = router_top2(x, g, w_router)
    src, pos, tile_expert = _moe_plan(sel)
    hs = gather_norm(x, g, src)
    act = _grouped_call(_glu_grouped_kernel, "moe_glu_up", hs, [wg.astype(BF16), wu.astype(BF16)],
                        tile_expert, _pick(wg.shape[2], (256, 128)), BF16)
    y = _grouped_call(_mm_grouped_kernel, "moe_down", act, [wd.astype(BF16)], tile_expert,
                      _pick(wd.shape[2], (1024, 512, 256, 128)), F32)
    return moe_combine(x, sel, y, pos)


def kernel(x, mem, ln_mix, w_in, g_cq, g_ckv, w_uq, w_ukv, g_q_mla, g_k_mla, g_q_dil, g_k_dil, g_q_dsa, g_k_dsa, w_gate_a, w_gate_b, b_gate, w_branch, w_out, ln_xattn, ln_mem, w_xq, w_xk, w_xv, g_q_x, g_k_x, w_xo, ln_ffn, w_ff_gate, w_ff_up, w_ff_down, w_router, w_e_gate, w_e_up, w_e_down):
    b, s, d = x.shape
    per_layer = dict(w_in=w_in, g_cq=g_cq, g_ckv=g_ckv, w_uq=w_uq, w_ukv=w_ukv, g_q_mla=g_q_mla,
                     g_k_mla=g_k_mla, g_q_dil=g_q_dil, g_k_dil=g_k_dil, g_q_dsa=g_q_dsa, g_k_dsa=g_k_dsa,
                     w_gate_a=w_gate_a, w_gate_b=w_gate_b, b_gate=b_gate, w_branch=w_branch, w_out=w_out,
                     ln_xattn=ln_xattn, ln_mem=ln_mem, w_xq=w_xq, w_xk=w_xk, w_xv=w_xv, g_q_x=g_q_x,
                     g_k_x=g_k_x, w_xo=w_xo)
    tabs = (_rope_tables(s, ROT_DIM, HEAD_DIM), _rope_tables(s, IDX_ROT, IDX_DIM),
            _rope_tables(s, MLA_ROPE, HEAD_DIM))
    depth = ln_mix.shape[0]
    outs = []
    for bi in range(b):
        xb, mb = x[bi], mem[bi]
        for i in range(depth):
            p = {name: val[i] for name, val in per_layer.items()}
            xb = _token_mixer(xb, rmsnorm_rows(xb, ln_mix[i]), p, tabs)
            xb = _cross_block(xb, mb, p)
            j = i // 2
            if i % 2 == 0:
                xb = _ffn_block(xb, ln_ffn[i], w_ff_gate[j], w_ff_up[j], w_ff_down[j])
            else:
                xb = _moe_block(xb, ln_ffn[i], w_router[j], w_e_gate[j], w_e_up[j], w_e_down[j])
        outs.append(xb)
    return outs[0][None] if b == 1 else jnp.stack(outs, axis=0)
```

```python
import functools
import math

import numpy as np
import jax
import jax.numpy as jnp
from jax import lax
from jax.experimental import pallas as pl
from jax.experimental.pallas import tpu as pltpu

F32 = jnp.float32
BF16 = jnp.bfloat16
I32 = jnp.int32

N_BRANCHES = 4
HEAD_DIM = 128
N_HEADS = 8
BRANCH_WIDTH = N_HEADS * HEAD_DIM
ROT_DIM = HEAD_DIM // 4
ROPE_THETA = 500000.0
RMS_EPS = 1e-6
NEG = -1e30
GATE_RANK = 256
MLA_Q_LORA = 896
MLA_KV_LORA = 256
MLA_NOPE = 128
MLA_ROPE = 64
MLA_QK = MLA_NOPE + MLA_ROPE
MLA_PAD = 256
DIL_PATTERNS = ((128, 1), (512, 4), (2048, 16))
DSA_TOPK = 256
IDX_HEADS = 8
IDX_DIM = 64
IDX_ROT = IDX_DIM // 4
MEM_HEADS = 4
MEM_HEAD_DIM = 128
MEM_WIDTH = MEM_HEADS * MEM_HEAD_DIM
N_EXPERTS = 8
MOE_TOP_K = 2
INT_MIN = -2 ** 31
LOG2E = math.log2(math.e)

ATT_T = 512
SB_T = 512
HEAD_GROUP = 2
DSA_TQ = 128
COUNT_ROWS = 64
VT_ROWS = HEAD_DIM
SB_EXIT = -105.0

V7X_VMEM_LIMIT_BYTES = 56 * 1024 * 1024
LANES = 128

NT_DIMS = (((1,), (1,)), ((), ()))


def _cparams(n_axes):
    return pltpu.CompilerParams(dimension_semantics=("arbitrary",) * n_axes,
                                vmem_limit_bytes=V7X_VMEM_LIMIT_BYTES)


def _pick(n, candidates):
    for c in candidates:
        if n % c == 0:
            return c
    return n


def _rmsnorm_kernel(x_ref, g_ref, o_ref):
    x = x_ref[...].astype(F32)
    ms = jnp.mean(x * x, axis=-1, keepdims=True)
    o_ref[...] = (x * lax.rsqrt(ms + RMS_EPS) * g_ref[...]).astype(o_ref.dtype)


def rmsnorm_rows(x, g, out_dtype=BF16):
    m, d = x.shape
    tm = _pick(m, (512, 256, 128))
    return pl.pallas_call(
        _rmsnorm_kernel, grid=(m // tm,),
        in_specs=[pl.BlockSpec((tm, d), lambda i: (i, 0)), pl.BlockSpec((1, d), lambda i: (0, 0))],
        out_specs=pl.BlockSpec((tm, d), lambda i: (i, 0)),
        out_shape=jax.ShapeDtypeStruct((m, d), out_dtype),
        compiler_params=_cparams(1), name="rmsnorm_rows",
    )(x, g.reshape(1, d).astype(F32))


def _mm_kernel(*refs, nk, has_res):
    a_ref, b_ref = refs[0], refs[1]
    r_ref = refs[2] if has_res else None
    o_ref = refs[2 + has_res]
    if nk == 1:
        part = jnp.dot(a_ref[...], b_ref[...], preferred_element_type=F32)
        if has_res:
            part = r_ref[...] + part
        o_ref[...] = part.astype(o_ref.dtype)
        return
    acc_ref = refs[3 + has_res]
    k = pl.program_id(2)

    @pl.when(k == 0)
    def _():
        acc_ref[...] = jnp.zeros(acc_ref.shape, F32)

    acc_ref[...] += jnp.dot(a_ref[...], b_ref[...], preferred_element_type=F32)

    @pl.when(k == nk - 1)
    def _():
        res = acc_ref[...]
        if has_res:
            res = r_ref[...] + res
        o_ref[...] = res.astype(o_ref.dtype)


def matmul(a, b, *, out_dtype, res=None, tm=None, tn=None, tk=None):
    m, kdim = a.shape
    n = b.shape[1]
    tm = tm or _pick(m, (1024, 512, 256, 128))
    tn = tn or _pick(n, (512, 256, 128))
    tk = tk or (kdim if kdim <= 4096 else _pick(kdim, (2048, 1024, 512)))
    nk = kdim // tk
    in_specs = [pl.BlockSpec((tm, tk), lambda i, j, k: (i, k)),
                pl.BlockSpec((tk, tn), lambda i, j, k: (k, j))]
    args = [a, b]
    if res is not None:
        in_specs.append(pl.BlockSpec((tm, tn), lambda i, j, k: (i, j)))
        args.append(res)
    scratch = [pltpu.VMEM((tm, tn), F32)] if nk > 1 else []
    return pl.pallas_call(
        functools.partial(_mm_kernel, nk=nk, has_res=res is not None),
        grid=(m // tm, n // tn, nk),
        in_specs=in_specs,
        out_specs=pl.BlockSpec((tm, tn), lambda i, j, k: (i, j)),
        out_shape=jax.ShapeDtypeStruct((m, n), out_dtype),
        scratch_shapes=scratch,
        compiler_params=_cparams(3), name="matmul",
    )(*args)


def _glu_kernel(a_ref, wg_ref, wu_ref, o_ref):
    a = a_ref[...]
    g = jnp.dot(a, wg_ref[...], preferred_element_type=F32)
    u = jnp.dot(a, wu_ref[...], preferred_element_type=F32)
    o_ref[...] = (g * jax.nn.sigmoid(g) * u).astype(o_ref.dtype)


def glu_up(a, wg, wu):
    m, kdim = a.shape
    f = wg.shape[1]
    tm = _pick(m, (1024, 512, 256, 128))
    tn = _pick(f, (512, 256, 128))
    w_spec = pl.BlockSpec((kdim, tn), lambda i, j: (0, j))
    return pl.pallas_call(
        _glu_kernel, grid=(m // tm, f // tn),
        in_specs=[pl.BlockSpec((tm, kdim), lambda i, j: (i, 0)), w_spec, w_spec],
        out_specs=pl.BlockSpec((tm, tn), lambda i, j: (i, j)),
        out_shape=jax.ShapeDtypeStruct((m, f), BF16),
        compiler_params=_cparams(2), name="glu_up",
    )(a, wg, wu)


MOE_TM = 512


def _glu_grouped_kernel(te_ref, a_ref, wg_ref, wu_ref, o_ref):
    _glu_kernel(a_ref, wg_ref, wu_ref, o_ref)


def _mm_grouped_kernel(te_ref, a_ref, w_ref, o_ref):
    o_ref[...] = jnp.dot(a_ref[...], w_ref[...], preferred_element_type=F32).astype(o_ref.dtype)


def _grouped_call(kernel, name, a, weights, tile_expert, tn, out_dtype):
    r, kdim = a.shape
    n = weights[0].shape[2]
    w_spec = pl.BlockSpec((None, kdim, tn), lambda j, i, te: (te[i], 0, j))
    grid_spec = pltpu.PrefetchScalarGridSpec(
        num_scalar_prefetch=1, grid=(n // tn, r // MOE_TM),
        in_specs=[pl.BlockSpec((MOE_TM, kdim), lambda j, i, te: (i, 0))] + [w_spec] * len(weights),
        out_specs=pl.BlockSpec((MOE_TM, tn), lambda j, i, te: (i, j)))
    return pl.pallas_call(
        kernel, grid_spec=grid_spec, out_shape=jax.ShapeDtypeStruct((r, n), out_dtype),
        compiler_params=_cparams(2), name=name,
    )(tile_expert, a, *weights)


def _row_gather(idx_ref, base, src_hbm, dst, sem, n, *, wait):
    def body(r, c):
        copy = pltpu.make_async_copy(src_hbm.at[pl.ds(idx_ref[base + r], 1)], dst.at[pl.ds(r, 1)], sem)
        if wait:
            copy.wait()
        else:
            copy.start()
        return c

    lax.fori_loop(0, n, body, 0, unroll=8)


def _gather_norm_kernel(src_ref, x_hbm, g_ref, o_ref, buf, sem):
    tm = o_ref.shape[0]
    base = pl.program_id(0) * tm
    _row_gather(src_ref, base, x_hbm, buf, sem, tm, wait=False)
    _row_gather(src_ref, base, x_hbm, buf, sem, tm, wait=True)
    x = buf[...]
    ms = jnp.mean(x * x, axis=-1, keepdims=True)
    o_ref[...] = (x * lax.rsqrt(ms + RMS_EPS) * g_ref[...]).astype(o_ref.dtype)


def gather_norm(x, g, src):
    d = x.shape[1]
    r = src.shape[0]
    grid_spec = pltpu.PrefetchScalarGridSpec(
        num_scalar_prefetch=1, grid=(r // MOE_TM,),
        in_specs=[pl.BlockSpec(memory_space=pl.ANY), pl.BlockSpec((1, d), lambda i, s: (0, 0))],
        out_specs=pl.BlockSpec((MOE_TM, d), lambda i, s: (i, 0)),
        scratch_shapes=[pltpu.VMEM((MOE_TM, d), F32), pltpu.SemaphoreType.DMA(())])
    return pl.pallas_call(
        _gather_norm_kernel, grid_spec=grid_spec, out_shape=jax.ShapeDtypeStruct((r, d), BF16),
        compiler_params=_cparams(1), name="gather_norm",
    )(src, x, g.reshape(1, d).astype(F32))


def _moe_combine_kernel(pos_ref, x_ref, sel_ref, y_hbm, o_ref, buf0, buf1, sems):
    tm = o_ref.shape[0]
    base0 = pl.program_id(0) * tm
    base1 = pos_ref.shape[0] // 2 + base0
    for wait in (False, True):
        _row_gather(pos_ref, base0, y_hbm, buf0, sems.at[0], tm, wait=wait)
        _row_gather(pos_ref, base1, y_hbm, buf1, sems.at[1], tm, wait=wait)
    sel = sel_ref[...]
    o_ref[...] = x_ref[...] + sel[:, 2:3] * buf0[...] + sel[:, 3:4] * buf1[...]


def moe_combine(x, sel, y, pos):
    s, d = x.shape
    tm = _pick(s, (256, 128))
    grid_spec = pltpu.PrefetchScalarGridSpec(
        num_scalar_prefetch=1, grid=(s // tm,),
        in_specs=[pl.BlockSpec((tm, d), lambda i, p: (i, 0)), pl.BlockSpec((tm, LANES), lambda i, p: (i, 0)),
                  pl.BlockSpec(memory_space=pl.ANY)],
        out_specs=pl.BlockSpec((tm, d), lambda i, p: (i, 0)),
        scratch_shapes=[pltpu.VMEM((tm, d), F32), pltpu.VMEM((tm, d), F32), pltpu.SemaphoreType.DMA((2,))])
    return pl.pallas_call(
        _moe_combine_kernel, grid_spec=grid_spec, out_shape=jax.ShapeDtypeStruct((s, d), F32),
        compiler_params=_cparams(1), name="moe_combine",
    )(pos.reshape(-1), x, sel, y)


def _moe_plan(sel):
    s = sel.shape[0]
    e = sel[:, :MOE_TOP_K].astype(I32).T.reshape(-1)
    onehot = (e[:, None] == jnp.arange(N_EXPERTS, dtype=I32)[None, :]).astype(I32)
    rank = jnp.take_along_axis(jnp.cumsum(onehot, axis=0), e[:, None], axis=1)[:, 0] - 1
    padded = (jnp.sum(onehot, axis=0) + MOE_TM - 1) // MOE_TM * MOE_TM
    ends = jnp.cumsum(padded)
    pos = (ends - padded)[e] + rank
    n_rows = MOE_TOP_K * s + N_EXPERTS * MOE_TM
    token = jnp.tile(jnp.arange(s, dtype=I32), MOE_TOP_K)
    src = jnp.zeros((n_rows,), I32).at[pos].set(token)
    tile_start = jnp.arange(n_rows // MOE_TM, dtype=I32) * MOE_TM
    tile_expert = jnp.minimum(jnp.searchsorted(ends, tile_start, side='right'), N_EXPERTS - 1).astype(I32)
    return src, pos.reshape(MOE_TOP_K, s).astype(I32), tile_expert


def _router_kernel(x_ref, g_ref, w_ref, o_ref):
    x = x_ref[...]
    ms = jnp.mean(x * x, axis=-1, keepdims=True)
    h = x * lax.rsqrt(ms + RMS_EPS) * g_ref[...]
    logits = jnp.dot(h, w_ref[...], preferred_element_type=F32, precision=lax.Precision.HIGHEST)
    lane = lax.broadcasted_iota(I32, logits.shape, 1).astype(F32)
    logits = jnp.where(lane < N_EXPERTS, logits, -jnp.inf)
    m1 = jnp.max(logits, axis=1, keepdims=True)
    i1 = jnp.min(jnp.where(logits == m1, lane, float(LANES)), axis=1, keepdims=True)
    rest = jnp.where(lane == i1, -jnp.inf, logits)
    m2 = jnp.max(rest, axis=1, keepdims=True)
    i2 = jnp.min(jnp.where(rest == m2, lane, float(LANES)), axis=1, keepdims=True)
    e2 = jnp.exp(m2 - m1)
    w1 = 1.0 / (1.0 + e2)
    w2 = e2 / (1.0 + e2)
    o_ref[...] = (jnp.where(lane == 0.0, i1, 0.0) + jnp.where(lane == 1.0, i2, 0.0)
                  + jnp.where(lane == 2.0, w1, 0.0) + jnp.where(lane == 3.0, w2, 0.0))


def router_top2(x, g, w_router):
    m, d = x.shape
    tm = _pick(m, (256, 128))
    w = jnp.zeros((d, LANES), F32).at[:, :N_EXPERTS].set(w_router.astype(F32))
    return pl.pallas_call(
        _router_kernel, grid=(m // tm,),
        in_specs=[pl.BlockSpec((tm, d), lambda i: (i, 0)), pl.BlockSpec((1, d), lambda i: (0, 0)),
                  pl.BlockSpec((d, LANES), lambda i: (0, 0))],
        out_specs=pl.BlockSpec((tm, LANES), lambda i: (i, 0)),
        out_shape=jax.ShapeDtypeStruct((m, LANES), F32),
        compiler_params=_cparams(1), name="router_top2",
    )(x, g.reshape(1, d).astype(F32), w)


def _rope(x, tab_ref, half):
    w = x.shape[-1]
    return (x * tab_ref[0] + pltpu.roll(x, w - half, 1) * tab_ref[1]
            + pltpu.roll(x, half, 1) * tab_ref[2])


def _rope_tables(seq, rot_dim, period):
    half = rot_dim // 2
    inv_freq = ROPE_THETA ** (-jnp.arange(0, rot_dim, 2, dtype=F32) / rot_dim)
    ang = jnp.arange(seq, dtype=F32)[:, None] * inv_freq[None, :]
    cos, sin = jnp.cos(ang), jnp.sin(ang)
    ones = jnp.ones((seq, period - rot_dim), F32)
    zeros = jnp.zeros((seq, period - rot_dim), F32)
    zh = jnp.zeros((seq, half), F32)
    c = jnp.concatenate([cos, cos, ones], axis=1)
    sa = jnp.concatenate([-sin, zh, zeros], axis=1)
    sb = jnp.concatenate([zh, sin, zeros], axis=1)
    rep = LANES // period
    return jnp.stack([jnp.tile(c, (1, rep)), jnp.tile(sa, (1, rep)), jnp.tile(sb, (1, rep))], axis=0)


def _tab_spec(tm):
    return pl.BlockSpec((3, tm, LANES), lambda i: (0, i, 0))


def _row_spec(tm, w):
    return pl.BlockSpec((tm, w), lambda i: (i, 0))


def _const_spec(w):
    return pl.BlockSpec((1, w), lambda i: (0, 0))


def _prep_latent_kernel(p_ref, gq_ref, gkv_ref, cq_ref, ckv_ref, kpe_ref):
    cq = p_ref[:, :MLA_Q_LORA]
    ms = jnp.mean(cq * cq, axis=-1, keepdims=True)
    cq_ref[...] = (cq * lax.rsqrt(ms + RMS_EPS) * gq_ref[...]).astype(cq_ref.dtype)
    ckv = p_ref[:, MLA_Q_LORA:MLA_Q_LORA + MLA_KV_LORA]
    ms = jnp.mean(ckv * ckv, axis=-1, keepdims=True)
    ckv_ref[...] = (ckv * lax.rsqrt(ms + RMS_EPS) * gkv_ref[...]).astype(ckv_ref.dtype)
    kpe_ref[...] = p_ref[:, MLA_Q_LORA + MLA_KV_LORA:]


def prep_latent(p, g_cq, g_ckv):
    s = p.shape[0]
    tm = _pick(s, (512, 256, 128))
    return pl.pallas_call(
        _prep_latent_kernel, grid=(s // tm,),
        in_specs=[_row_spec(tm, p.shape[1]), _const_spec(MLA_Q_LORA), _const_spec(MLA_KV_LORA)],
        out_specs=[_row_spec(tm, MLA_Q_LORA), _row_spec(tm, MLA_KV_LORA), _row_spec(tm, LANES)],
        out_shape=[jax.ShapeDtypeStruct((s, MLA_Q_LORA), BF16),
                   jax.ShapeDtypeStruct((s, MLA_KV_LORA), BF16),
                   jax.ShapeDtypeStruct((s, LANES), F32)],
        compiler_params=_cparams(1), name="prep_latent",
    )(p, g_cq.reshape(1, -1), g_ckv.reshape(1, -1))


def _prep_mla_kernel(qup_ref, kvup_ref, kpe_ref, gq_ref, gk_ref, tab_ref, qT_ref, k_ref, vT_ref):
    half = MLA_ROPE // 2
    gqn, gqr = gq_ref[:, :LANES], gq_ref[:, LANES:]
    gkn, gkr = gk_ref[:, :LANES], gk_ref[:, LANES:]
    kpe = kpe_ref[...]
    kpe_ss = jnp.sum(kpe * kpe, axis=-1, keepdims=True)
    kr_base = _rope(kpe * gkr, tab_ref, half)
    scale = LOG2E * MLA_QK ** -0.5
    for h in range(N_HEADS):
        qn = qup_ref[:, h * MLA_PAD:h * MLA_PAD + LANES]
        qr = qup_ref[:, h * MLA_PAD + LANES:(h + 1) * MLA_PAD]
        ss = jnp.sum(qn * qn, axis=-1, keepdims=True) + jnp.sum(qr * qr, axis=-1, keepdims=True)
        r = lax.rsqrt(ss * (1.0 / MLA_QK) + RMS_EPS) * scale
        qT_ref[h, :LANES, :] = (qn * r * gqn).T.astype(BF16)
        qT_ref[h, LANES:, :] = _rope(qr * r * gqr, tab_ref, half).T.astype(BF16)
        kn = kvup_ref[:, h * LANES:(h + 1) * LANES]
        ss = jnp.sum(kn * kn, axis=-1, keepdims=True) + kpe_ss
        r = lax.rsqrt(ss * (1.0 / MLA_QK) + RMS_EPS)
        k_ref[:, h * MLA_PAD:h * MLA_PAD + LANES] = (kn * r * gkn).astype(BF16)
        k_ref[:, h * MLA_PAD + LANES:(h + 1) * MLA_PAD] = (kr_base * r).astype(BF16)
        vT_ref[h] = _vT_bf16(kvup_ref[:, BRANCH_WIDTH + h * LANES:BRANCH_WIDTH + (h + 1) * LANES])


def _qT_spec(d, t):
    return pl.BlockSpec((N_HEADS, d, t), lambda i: (0, 0, i))


def _vT_spec(t, rows):
    return pl.BlockSpec((N_HEADS, None, rows, t), lambda i: (0, i, 0, 0))


def _qkv_shapes(s, dq, wk, t, rows):
    return [jax.ShapeDtypeStruct((N_HEADS, dq, s), BF16), jax.ShapeDtypeStruct((s, wk), BF16),
            jax.ShapeDtypeStruct((N_HEADS, s // t, rows, t), BF16)]


def _vT_bf16(v):
    return v.T.astype(BF16)


def prep_mla(qup, kvup, kpe, g_q, g_k, tab):
    s = qup.shape[0]
    t = ATT_T
    pad = lambda g: jnp.zeros((1, MLA_PAD), F32).at[0, :MLA_QK].set(g)
    wq = N_HEADS * MLA_PAD
    return pl.pallas_call(
        _prep_mla_kernel, grid=(s // t,),
        in_specs=[_row_spec(t, wq), _row_spec(t, 2 * BRANCH_WIDTH), _row_spec(t, LANES),
                  _const_spec(MLA_PAD), _const_spec(MLA_PAD), _tab_spec(t)],
        out_specs=[_qT_spec(MLA_PAD, t), _row_spec(t, wq), _vT_spec(t, VT_ROWS)],
        out_shape=_qkv_shapes(s, MLA_PAD, wq, t, VT_ROWS),
        compiler_params=_cparams(1), name="prep_mla",
    )(qup, kvup, kpe, pad(g_q), pad(g_k), tab)


def _head_norm_rope(x, g, tab_ref, scale):
    ms = jnp.mean(x * x, axis=-1, keepdims=True)
    y = x * (lax.rsqrt(ms + RMS_EPS) * scale) * g
    return _rope(y, tab_ref, ROT_DIM // 2)


def _prep_sb_kernel(p_ref, qT_ref, k_ref, vT_ref, kn_ref):
    for h in range(N_HEADS):
        sl = slice(h * LANES, (h + 1) * LANES)
        qT_ref[h] = (p_ref[:, sl] * HEAD_DIM ** -0.5).T.astype(BF16)
        vT_ref[h] = p_ref[:, 2 * BRANCH_WIDTH + h * LANES:2 * BRANCH_WIDTH + (h + 1) * LANES].T.astype(BF16)
        kb = p_ref[:, BRANCH_WIDTH + h * LANES:BRANCH_WIDTH + (h + 1) * LANES].astype(BF16)
        k_ref[:, sl] = kb
        kf = kb.astype(F32)
        kn_ref[h] = jnp.zeros((8, LANES), F32) + jnp.max(jnp.sum(kf * kf, axis=1, keepdims=True))


def prep_sb(p):
    s = p.shape[0]
    t = SB_T
    return pl.pallas_call(
        _prep_sb_kernel, grid=(s // t,),
        in_specs=[_row_spec(t, 3 * BRANCH_WIDTH)],
        out_specs=[_qT_spec(HEAD_DIM, t), _row_spec(t, BRANCH_WIDTH), _vT_spec(t, HEAD_DIM),
                   pl.BlockSpec((N_HEADS, None, 8, LANES), lambda i: (0, i, 0, 0))],
        out_shape=_qkv_shapes(s, HEAD_DIM, BRANCH_WIDTH, t, HEAD_DIM)
        + [jax.ShapeDtypeStruct((N_HEADS, s // t, 8, LANES), F32)],
        compiler_params=_cparams(1), name="prep_sb",
    )(p)


def _prep_dil_kernel(p_ref, gq_ref, gk_ref, tab_ref, qT_ref, k_ref, vT_ref):
    gq, gk = gq_ref[...], gk_ref[...]
    for h in range(N_HEADS):
        sl = slice(h * LANES, (h + 1) * LANES)
        qT_ref[h] = _head_norm_rope(p_ref[:, sl], gq, tab_ref, LOG2E * HEAD_DIM ** -0.5).T.astype(BF16)
        ksl = slice(BRANCH_WIDTH + h * LANES, BRANCH_WIDTH + (h + 1) * LANES)
        k_ref[:, sl] = _head_norm_rope(p_ref[:, ksl], gk, tab_ref, 1.0).astype(BF16)
        vT_ref[h] = _vT_bf16(p_ref[:, 2 * BRANCH_WIDTH + h * LANES:2 * BRANCH_WIDTH + (h + 1) * LANES])


def prep_dil(p, g_q, g_k, tab):
    s = p.shape[0]
    t = ATT_T
    return pl.pallas_call(
        _prep_dil_kernel, grid=(s // t,),
        in_specs=[_row_spec(t, 3 * BRANCH_WIDTH), _const_spec(LANES), _const_spec(LANES), _tab_spec(t)],
        out_specs=[_qT_spec(HEAD_DIM, t), _row_spec(t, BRANCH_WIDTH), _vT_spec(t, VT_ROWS)],
        out_shape=_qkv_shapes(s, HEAD_DIM, BRANCH_WIDTH, t, VT_ROWS),
        compiler_params=_cparams(1), name="prep_dil",
    )(p, g_q.reshape(1, -1), g_k.reshape(1, -1), tab)


def _prep_dsa_kernel(p_ref, gq_ref, gk_ref, tab_ref, qT_ref, k_ref, vT_ref):
    gq, gk = gq_ref[...], gk_ref[...]
    for h in range(N_HEADS):
        sl = slice(h * LANES, (h + 1) * LANES)
        qT = _head_norm_rope(p_ref[:, sl], gq, tab_ref, LOG2E * HEAD_DIM ** -0.5).T.astype(BF16)
        for b in range(p_ref.shape[0] // DSA_TQ):
            qT_ref[b, :, sl] = qT[:, b * DSA_TQ:(b + 1) * DSA_TQ]
    k_ref[...] = _head_norm_rope(p_ref[:, BRANCH_WIDTH:BRANCH_WIDTH + LANES], gk, tab_ref, 1.0).astype(BF16)
    vT_ref[...] = _vT_bf16(p_ref[:, BRANCH_WIDTH + LANES:])


def prep_dsa(p, g_q, g_k, tab):
    s = p.shape[0]
    t = ATT_T
    nqb = t // DSA_TQ
    return pl.pallas_call(
        _prep_dsa_kernel, grid=(s // t,),
        in_specs=[_row_spec(t, BRANCH_WIDTH + 2 * LANES), _const_spec(LANES), _const_spec(LANES), _tab_spec(t)],
        out_specs=[pl.BlockSpec((nqb, HEAD_DIM, BRANCH_WIDTH), lambda i: (i, 0, 0)), _row_spec(t, LANES),
                   pl.BlockSpec((None, VT_ROWS, t), lambda i: (i, 0, 0))],
        out_shape=[jax.ShapeDtypeStruct((s // DSA_TQ, HEAD_DIM, BRANCH_WIDTH), BF16),
                   jax.ShapeDtypeStruct((s, LANES), BF16),
                   jax.ShapeDtypeStruct((s // t, VT_ROWS, t), BF16)],
        compiler_params=_cparams(1), name="prep_dsa",
    )(p, g_q.reshape(1, -1), g_k.reshape(1, -1), tab)


def _prep_idx_kernel(p_ref, tab_ref, qiT_ref, ki_ref, wiT_ref):
    half = IDX_ROT // 2
    t = p_ref.shape[0]
    nqb = t // DSA_TQ
    lane = lax.broadcasted_iota(I32, (t, LANES), 1)
    first = lane < IDX_DIM
    zero = jnp.zeros((t, LANES), F32)
    for b in range(IDX_HEADS // 2):
        x = _rope(p_ref[:, b * LANES:(b + 1) * LANES], tab_ref, half) * (IDX_DIM ** -0.5)
        hi = x.astype(BF16).astype(F32)
        lo = x - hi
        rhi = pltpu.roll(hi, IDX_DIM, 1)
        for hh, (a0, a1) in enumerate(((jnp.where(first, hi, pltpu.roll(lo, IDX_DIM, 1)), jnp.where(first, hi, zero)),
                                       (jnp.where(first, rhi, lo), jnp.where(first, rhi, zero)))):
            h = 2 * b + hh
            a0T, a1T = a0.T.astype(BF16), a1.T.astype(BF16)
            for qb in range(nqb):
                qs = slice(qb * DSA_TQ, (qb + 1) * DSA_TQ)
                qiT_ref[qb, :LANES, h * DSA_TQ:(h + 1) * DSA_TQ] = a0T[:, qs]
                qiT_ref[qb, LANES:, h * DSA_TQ:(h + 1) * DSA_TQ] = a1T[:, qs]
    kx = _rope(p_ref[:, IDX_HEADS * IDX_DIM:IDX_HEADS * IDX_DIM + LANES], tab_ref, half)
    hi = kx.astype(BF16).astype(F32)
    lo = kx - hi
    ki_ref[:, :LANES] = jnp.where(first, hi, pltpu.roll(hi, IDX_DIM, 1)).astype(BF16)
    ki_ref[:, LANES:] = jnp.where(first, lo, zero).astype(BF16)
    wT = (p_ref[:, IDX_HEADS * IDX_DIM + LANES:] * (IDX_HEADS ** -0.5)).T
    for qb in range(nqb):
        wiT_ref[qb] = wT[:IDX_HEADS, qb * DSA_TQ:(qb + 1) * DSA_TQ]


def prep_idx(p, tab):
    s = p.shape[0]
    t = ATT_T
    nqb = t // DSA_TQ
    return pl.pallas_call(
        _prep_idx_kernel, grid=(s // t,),
        in_specs=[_row_spec(t, IDX_HEADS * IDX_DIM + 2 * LANES), _tab_spec(t)],
        out_specs=[pl.BlockSpec((nqb, 2 * LANES, IDX_HEADS * DSA_TQ), lambda i: (i, 0, 0)),
                   _row_spec(t, 2 * LANES),
                   pl.BlockSpec((nqb, IDX_HEADS, DSA_TQ), lambda i: (i, 0, 0))],
        out_shape=[jax.ShapeDtypeStruct((s // DSA_TQ, 2 * LANES, IDX_HEADS * DSA_TQ), BF16),
                   jax.ShapeDtypeStruct((s, 2 * LANES), BF16),
                   jax.ShapeDtypeStruct((s // DSA_TQ, IDX_HEADS, DSA_TQ), F32)],
        compiler_params=_cparams(1), name="prep_idx",
    )(p, tab)


def _softmax_step(sT, vT, m_sc, acc_sc):
    l_sc, o_sc = acc_sc
    m_prev = m_sc[...]
    m_new = jnp.maximum(m_prev, jnp.max(sT, axis=0, keepdims=True))
    alpha = jnp.exp2(m_prev - m_new)
    p = jnp.exp2(sT - m_new)
    l_sc[...] = alpha * l_sc[...] + jnp.sum(p, axis=0, keepdims=True)
    o_sc[...] = alpha * o_sc[...] + jnp.dot(vT, p.astype(BF16), preferred_element_type=F32)
    m_sc[...] = m_new


def _softmax_init(m_sc, acc_sc):
    m_sc[...] = jnp.full(m_sc.shape, NEG, F32)
    for ref in acc_sc:
        ref[...] = jnp.zeros(ref.shape, F32)


def _softmax_result(acc_sc):
    l_sc, o_sc = acc_sc
    return o_sc[...] / l_sc[...]


def _pipelined_blocks(n_full, scores, consume, consume_last, s_sc):
    s_sc[0] = scores(0)

    def pair(jj, c):
        s_sc[1] = scores(2 * jj + 1)
        consume(s_sc[0], 2 * jj)
        s_sc[0] = scores(2 * jj + 2)
        consume(s_sc[1], 2 * jj + 1)
        return c

    lax.fori_loop(0, n_full // 2, pair, 0)

    @pl.when(n_full % 2 == 1)
    def _():
        s_sc[1] = scores(n_full)
        consume(s_sc[0], n_full - 1)
        consume_last(s_sc[1], n_full)

    @pl.when(n_full % 2 == 0)
    def _():
        consume_last(s_sc[0], n_full)


def _key_le_query(tk, tq):
    return lax.broadcasted_iota(I32, (tk, tq), 0) <= lax.broadcasted_iota(I32, (tk, tq), 1)


def _head_group_attention(qT_ref, k_ref, vT_ref, o_ref, s_sc, m_sc, l_sc, o_sc, *,
                          n_full, block_of, bias_of, causal_last):
    g, _, t = qT_ref.shape
    w = k_ref.shape[2] // g
    accs = [(l_sc.at[h], o_sc.at[h]) for h in range(g)]
    for h in range(g):
        _softmax_init(m_sc.at[h], accs[h])

    def scores(d):
        kb = k_ref[block_of(d)]
        return jnp.stack([jnp.dot(kb[:, h * w:(h + 1) * w], qT_ref[h], preferred_element_type=F32)
                          for h in range(g)])

    def step(sT, d, causal=False):
        bias = None if bias_of is None else bias_of(d)
        for h in range(g):
            s_h = sT[h] if bias is None else sT[h] + bias
            if causal:
                s_h = jnp.where(_key_le_query(t, t), s_h, NEG)
            _softmax_step(s_h, vT_ref[h, block_of(d)], m_sc.at[h], accs[h])

    _pipelined_blocks(n_full, scores, step, functools.partial(step, causal=causal_last), s_sc)
    for h in range(g):
        o_ref[:, h * HEAD_DIM:(h + 1) * HEAD_DIM] = _softmax_result(accs[h]).T.astype(o_ref.dtype)


def _head_group_call(kernel, name, qT, k, vT, extra=()):
    n_heads, dq, s = qT.shape
    t = ATT_T
    nb = s // t
    g = HEAD_GROUP
    once = pl.Buffered(1)
    whole = lambda a: pl.BlockSpec(a.shape, lambda h, i: (0,) * a.ndim, pipeline_mode=once)
    return pl.pallas_call(
        kernel, grid=(n_heads // g, nb),
        in_specs=[pl.BlockSpec((g, dq, t), lambda h, i: (h, 0, i)),
                  pl.BlockSpec((nb, t, g * dq), lambda h, i: (0, 0, h), pipeline_mode=once),
                  pl.BlockSpec((g, nb, VT_ROWS, t), lambda h, i: (h, 0, 0, 0), pipeline_mode=once)]
        + [whole(a) for a in extra],
        out_specs=pl.BlockSpec((t, g * HEAD_DIM), lambda h, i: (i, h)),
        out_shape=jax.ShapeDtypeStruct((s, n_heads * HEAD_DIM), BF16),
        scratch_shapes=[pltpu.VMEM((2, g, t, t), F32), pltpu.VMEM((g, 1, t), F32), pltpu.VMEM((g, 1, t), F32),
                        pltpu.VMEM((g, HEAD_DIM, t), F32)],
        compiler_params=_cparams(2), name=name,
    )(qT, k.reshape(nb, t, -1), vT, *extra)


def _mla_kernel(qT_ref, k_ref, vT_ref, o_ref, *scratch):
    _head_group_attention(qT_ref, k_ref, vT_ref, o_ref, *scratch, n_full=pl.program_id(1),
                          block_of=lambda j: j, bias_of=None, causal_last=True)


def _attn_scratch(tk, tq):
    return [pltpu.VMEM((2, tk, tq), F32), pltpu.VMEM((1, tq), F32), pltpu.VMEM((1, tq), F32),
            pltpu.VMEM((HEAD_DIM, tq), F32)]


def _head_qT_spec(d, t):
    return pl.BlockSpec((None, d, t), lambda h, i: (h, 0, i))


def _head_k_spec(nb, t, w):
    return pl.BlockSpec((nb, t, w), lambda h, i: (0, 0, h))


def _head_vT_spec(nb, t, rows=VT_ROWS):
    return pl.BlockSpec((None, nb, rows, t), lambda h, i: (h, 0, 0, 0))


def _head_out_spec(t):
    return pl.BlockSpec((t, HEAD_DIM), lambda h, i: (i, h))


def mla_attention(qT, k, vT):
    return _head_group_call(_mla_kernel, "mla_attention", qT, k, vT)


SB_SUB = 128


def _sb_kernel(qT_ref, k_ref, vT_ref, u_ref, kn_ref, o_ref, carry_sc, acc_sc):
    i = pl.program_id(1)
    t = qT_ref.shape[1]
    qT = qT_ref[...]
    u2 = u_ref[...]
    qf = qT.astype(F32)
    zbound = jnp.sqrt(jnp.sum(qf * qf, axis=0, keepdims=True) * jnp.max(kn_ref[...]))
    carry_sc[...] = jnp.zeros(carry_sc.shape, F32)
    acc_sc[...] = jnp.zeros(acc_sc.shape, F32)
    key = lax.broadcasted_iota(I32, (SB_SUB, t), 0)
    qry = lax.broadcasted_iota(I32, (SB_SUB, t), 1)

    def block(j, diag):
        zT = jnp.dot(k_ref[j], qT, preferred_element_type=F32)
        carry = carry_sc[...]
        parts = [None] * (t // SB_SUB)
        for c in reversed(range(t // SB_SUB)):
            zc = zT[c * SB_SUB:(c + 1) * SB_SUB]
            ls = jnp.minimum(-zc, 0.0) - jnp.log(1.0 + jnp.exp(-jnp.abs(zc)))
            if diag:
                past = key + c * SB_SUB < qry
                ls = jnp.where(past, ls, 0.0)
            hi = ls.astype(BF16)
            lo = (ls - hi.astype(F32)).astype(BF16)
            rev = jnp.dot(u2, jnp.concatenate([hi, lo], axis=0), preferred_element_type=F32) + carry
            a = jnp.exp(jnp.minimum(zc + rev, 0.0))
            if diag:
                a = jnp.where(past, a, 0.0)
            parts[c] = a.astype(BF16)
            carry = rev[0:1, :]
        carry_sc[...] = carry
        acc_sc[...] += jnp.dot(vT_ref[j], jnp.concatenate(parts, axis=0), preferred_element_type=F32)

    def all_underflow():
        return (jnp.max(carry_sc[...] + zbound) < SB_EXIT).astype(I32)

    block(i, True)

    def back(state):
        jj, _ = state
        block(i - 1 - jj, False)
        return jj + 1, all_underflow()

    lax.while_loop(lambda st: jnp.logical_and(st[0] < i, st[1] == 0), back, (jnp.int32(0), all_underflow()))
    o_ref[...] = acc_sc[...].T.astype(o_ref.dtype)


def sb_attention(qT, k, vT, kn):
    s = k.shape[0]
    t = SB_T
    nb = s // t
    tri = (np.arange(SB_SUB)[None, :] >= np.arange(SB_SUB)[:, None]).astype(np.float32)
    u2 = jnp.asarray(np.concatenate([tri, tri], axis=1), dtype=BF16)
    return pl.pallas_call(
        _sb_kernel, grid=(N_HEADS, nb),
        in_specs=[_head_qT_spec(HEAD_DIM, t), _head_k_spec(nb, t, HEAD_DIM), _head_vT_spec(nb, t, HEAD_DIM),
                  pl.BlockSpec((SB_SUB, 2 * SB_SUB), lambda h, i: (0, 0)),
                  pl.BlockSpec((None, nb, 8, LANES), lambda h, i: (h, 0, 0, 0))],
        out_specs=_head_out_spec(t),
        out_shape=jax.ShapeDtypeStruct((s, BRANCH_WIDTH), BF16),
        scratch_shapes=[pltpu.VMEM((1, t), F32), pltpu.VMEM((HEAD_DIM, t), F32)],
        compiler_params=_cparams(2), name="sb_attention",
    )(qT, k.reshape(nb, t, -1), vT, u2, kn)


def _dil_log_weights(t):
    span = max(w for w, _ in DIL_PATTERNS)
    nback = -(-span // t)
    d = np.arange(nback + 1)[:, None, None] * t + np.arange(t)[None, None, :] - np.arange(t)[None, :, None]
    mult = np.zeros(d.shape, np.float64)
    for window, dil in DIL_PATTERNS:
        mult += ((d >= 0) & (d <= window) & (d % dil == 0))
    return np.where(mult > 0, np.log2(np.maximum(mult, 1.0)), NEG).astype(np.float32)


def _dil_kernel(qT_ref, k_ref, vT_ref, w_ref, o_ref, *scratch):
    i = pl.program_id(1)
    nback = w_ref.shape[0] - 1
    _head_group_attention(qT_ref, k_ref, vT_ref, o_ref, *scratch, n_full=jnp.minimum(i, nback),
                          block_of=lambda d: i - d, bias_of=lambda d: w_ref[d], causal_last=False)


def dil_attention(qT, k, vT):
    return _head_group_call(_dil_kernel, "dil_attention", qT, k, vT, extra=(jnp.asarray(_dil_log_weights(ATT_T)),))


def _dsa_kernel(qsT_ref, qiT_ref, wiT_ref, ki_ref, k_ref, vT_ref, o_ref,
                key_sc, q1_sc, s_sc, m_sc, l_sc, o_sc, *, top_k, pos_bits):
    acc_sc = (l_sc, o_sc)
    i = pl.program_id(0)
    tq, tk = DSA_TQ, ATT_T
    nkb = (i * tq + tq + tk - 1) // tk
    qiT = qiT_ref[...]
    wiT = wiT_ref[...]
    row = lax.broadcasted_iota(I32, (tk, tq), 1) + i * tq
    col = lax.broadcasted_iota(I32, (tk, tq), 0)

    def idx_block(aT, j):
        idx = jnp.zeros((tk, tq), F32)
        for h in range(IDX_HEADS):
            idx = idx + jnp.maximum(aT[:, h * tq:(h + 1) * tq], 0.0) * wiT[h:h + 1, :]
        bits = pltpu.bitcast(idx + 0.0, I32)
        key = bits ^ ((bits >> 31) & 0x7FFFFFFF)
        key_sc[j] = jnp.where(col + j * tk <= row, key, INT_MIN)

    _pipelined_blocks(nkb - 1, lambda j: jnp.dot(ki_ref[j], qiT, preferred_element_type=F32),
                      idx_block, idx_block, s_sc)

    def count(hits):
        def body(j, acc):
            hit = hits(key_sc[j], j)
            return acc + jnp.sum(hit.reshape(tk // COUNT_ROWS, COUNT_ROWS, tq), axis=0)
        acc = lax.fori_loop(0, nkb, body, jnp.zeros((COUNT_ROWS, tq), F32))
        return jnp.sum(acc, axis=0, keepdims=True)

    def bit_body(b, carry):
        thr, cge = carry
        cand = thr + jnp.left_shift(jnp.int32(1), 31 - b)
        c = count(lambda kb, j: jnp.where(kb >= cand, 1.0, 0.0))
        ok = c >= top_k
        return jnp.where(ok, cand, thr), jnp.where(ok, c, cge)

    thr0 = jnp.full((1, tq), INT_MIN, I32)
    cge0 = jnp.zeros((1, tq), F32) + (nkb * tk).astype(F32)
    thr, cge = lax.fori_loop(0, 32, bit_body, (thr0, cge0))

    q1_sc[...] = jnp.full((1, tq), 2 ** 30, I32)

    @pl.when(jnp.max(cge) > top_k)
    def _():
        need = top_k - count(lambda kb, j: jnp.where(kb > thr, 1.0, 0.0))

        def pos_body(b, qpos):
            cand = qpos + jnp.left_shift(jnp.int32(1), pos_bits - 1 - b)
            g = count(lambda kb, j: jnp.where(kb == thr, jnp.where(col + j * tk < cand, 1.0, 0.0), 0.0))
            return jnp.where(g < need, cand, qpos)

        q1_sc[...] = lax.fori_loop(0, pos_bits, pos_body, jnp.zeros((1, tq), I32)) + 1

    q1 = q1_sc[...]
    qsT = qsT_ref[...]
    _softmax_init(m_sc, acc_sc)
    def att_block(sT, j):
        kb = key_sc[j]
        pos = col + j * tk
        bias = jnp.where(kb > thr, 0.0, jnp.where(kb == thr, jnp.where(pos < q1, 0.0, NEG), NEG))
        bias = jnp.where(pos <= row, bias, NEG)
        _softmax_step(sT + jnp.concatenate([bias] * N_HEADS, axis=1), vT_ref[j], m_sc, acc_sc)

    _pipelined_blocks(nkb - 1, lambda j: jnp.dot(k_ref[j], qsT, preferred_element_type=F32),
                      att_block, att_block, s_sc)
    o = _softmax_result(acc_sc)
    for h in range(N_HEADS):
        sl = slice(h * HEAD_DIM, (h + 1) * HEAD_DIM)
        o_ref[:, sl] = o[:, h * tq:(h + 1) * tq].T.astype(o_ref.dtype)


def dsa_attention(qsT, qiT, wiT, ki, k, vT):
    s = k.shape[0]
    tq, tk = DSA_TQ, ATT_T
    nkb = s // tk
    top_k = min(DSA_TOPK, s // 4)
    wq = N_HEADS * tq
    full3 = lambda a, b: pl.BlockSpec((nkb, a, b), lambda i: (0, 0, 0))
    return pl.pallas_call(
        functools.partial(_dsa_kernel, top_k=top_k, pos_bits=(s - 1).bit_length()),
        grid=(s // tq,),
        in_specs=[pl.BlockSpec((None, HEAD_DIM, wq), lambda i: (i, 0, 0)),
                  pl.BlockSpec((None, 2 * LANES, wq), lambda i: (i, 0, 0)),
                  pl.BlockSpec((None, IDX_HEADS, tq), lambda i: (i, 0, 0)),
                  full3(tk, 2 * LANES), full3(tk, HEAD_DIM), full3(VT_ROWS, tk)],
        out_specs=pl.BlockSpec((tq, BRANCH_WIDTH), lambda i: (i, 0)),
        out_shape=jax.ShapeDtypeStruct((s, BRANCH_WIDTH), BF16),
        scratch_shapes=[pltpu.VMEM((nkb, tk, tq), I32), pltpu.VMEM((1, tq), I32)] + _attn_scratch(tk, wq),
        compiler_params=_cparams(1), name="dsa_attention",
    )(qsT, qiT, wiT, ki.reshape(nkb, tk, -1), k.reshape(nkb, tk, -1), vT)


def _merge_kernel(g_ref, o0_ref, o1_ref, o2_ref, o3_ref, wgb_ref, b_ref, wbr_ref, out_ref):
    g = g_ref[...]
    acc = None
    for n, o_ref in enumerate((o0_ref, o1_ref, o2_ref, o3_ref)):
        gate = jax.nn.sigmoid(jnp.dot(g, wgb_ref[n], preferred_element_type=F32) + b_ref[n])
        val = gate * jnp.dot(o_ref[...], wbr_ref[n], preferred_element_type=F32)
        acc = val if acc is None else acc + val
    out_ref[...] = acc.astype(out_ref.dtype)


def gated_merge(g_lat, outs, w_gate_b, b_gate, w_branch):
    s = g_lat.shape[0]
    d = w_gate_b.shape[-1]
    tm = _pick(s, (1024, 512, 256, 128))
    tn = _pick(d, (512, 256, 128))
    o_spec = pl.BlockSpec((tm, BRANCH_WIDTH), lambda i, j: (i, 0))
    return pl.pallas_call(
        _merge_kernel, grid=(s // tm, d // tn),
        in_specs=[pl.BlockSpec((tm, GATE_RANK), lambda i, j: (i, 0)), o_spec, o_spec, o_spec, o_spec,
                  pl.BlockSpec((N_BRANCHES, GATE_RANK, tn), lambda i, j: (0, 0, j)),
                  pl.BlockSpec((N_BRANCHES, 1, tn), lambda i, j: (0, 0, j)),
                  pl.BlockSpec((N_BRANCHES, BRANCH_WIDTH, tn), lambda i, j: (0, 0, j))],
        out_specs=pl.BlockSpec((tm, tn), lambda i, j: (i, j)),
        out_shape=jax.ShapeDtypeStruct((s, d), BF16),
        compiler_params=_cparams(2), name="gated_merge",
    )(g_lat, *outs, w_gate_b, b_gate.reshape(N_BRANCHES, 1, d), w_branch)


def _cross_kernel(x_ref, g_ref, wq_ref, gq_ref, kraw_ref, gk_ref, v_ref, wo_ref, o_ref):
    x = x_ref[...]
    ms = jnp.mean(x * x, axis=-1, keepdims=True)
    h = (x * lax.rsqrt(ms + RMS_EPS) * g_ref[...]).astype(BF16)
    q = jnp.dot(h, wq_ref[...], preferred_element_type=F32)
    outs = []
    for hd in range(MEM_HEADS):
        sl = slice(hd * MEM_HEAD_DIM, (hd + 1) * MEM_HEAD_DIM)
        qh = q[:, sl]
        qh = qh * (lax.rsqrt(jnp.mean(qh * qh, axis=-1, keepdims=True) + RMS_EPS) * MEM_HEAD_DIM ** -0.5) * gq_ref[...]
        kh = kraw_ref[:, sl]
        kh = kh * lax.rsqrt(jnp.mean(kh * kh, axis=-1, keepdims=True) + RMS_EPS) * gk_ref[...]
        s = lax.dot_general(qh.astype(BF16), kh.astype(BF16), NT_DIMS, preferred_element_type=F32)
        p = jnp.exp(s - jnp.max(s, axis=1, keepdims=True))
        p = p / jnp.sum(p, axis=1, keepdims=True)
        outs.append(jnp.dot(p.astype(BF16), v_ref[:, sl], preferred_element_type=F32).astype(BF16))
    o = jnp.concatenate(outs, axis=1)
    o_ref[...] = x + jnp.dot(o, wo_ref[...], preferred_element_type=F32)


def cross_attention(x, g, w_xq, g_q, k_raw, g_k, v, w_xo):
    s, d = x.shape
    m = k_raw.shape[0]
    tm = _pick(s, (256, 128))
    c2 = lambda shape: pl.BlockSpec(shape, lambda i: (0, 0))
    return pl.pallas_call(
        _cross_kernel, grid=(s // tm,),
        in_specs=[_row_spec(tm, d), c2((1, d)), c2((d, MEM_WIDTH)), c2((1, MEM_HEAD_DIM)),
                  c2((m, MEM_WIDTH)), c2((1, MEM_HEAD_DIM)), c2((m, MEM_WIDTH)), c2((MEM_WIDTH, d))],
        out_specs=_row_spec(tm, d),
        out_shape=jax.ShapeDtypeStruct((s, d), F32),
        compiler_params=_cparams(1), name="cross_attention",
    )(x, g.reshape(1, d), w_xq, g_q.reshape(1, -1), k_raw, g_k.reshape(1, -1), v, w_xo)


def _split_w_in(w_in):
    cuts = np.cumsum([0, MLA_Q_LORA, MLA_KV_LORA, MLA_ROPE, BRANCH_WIDTH, BRANCH_WIDTH, BRANCH_WIDTH,
                      BRANCH_WIDTH, BRANCH_WIDTH, BRANCH_WIDTH, BRANCH_WIDTH, HEAD_DIM, HEAD_DIM,
                      IDX_HEADS * IDX_DIM, IDX_DIM, IDX_HEADS])
    seg = lambda a, b: w_in[:, cuts[a]:cuts[b]]
    zeros = lambda n: jnp.zeros((w_in.shape[0], n), w_in.dtype)
    w_lat = jnp.concatenate([seg(0, 3), zeros(LANES - MLA_ROPE)], axis=1)
    w_sb = seg(3, 6)
    w_dil = seg(6, 9)
    w_dsa = seg(9, 12)
    w_idx = jnp.concatenate([seg(12, 14), zeros(LANES - IDX_DIM), seg(14, 15), zeros(LANES - IDX_HEADS)], axis=1)
    return [w.astype(BF16) for w in (w_lat, w_sb, w_dil, w_dsa, w_idx)]


def _pad_mla_up(w_uq, w_ukv):
    r = w_uq.shape[0]
    wq = w_uq.reshape(r, N_HEADS, MLA_QK)
    wq = jnp.concatenate([wq, jnp.zeros((r, N_HEADS, MLA_PAD - MLA_QK), wq.dtype)], axis=2)
    wkv = w_ukv.reshape(w_ukv.shape[0], N_HEADS, 2 * HEAD_DIM)
    wkv = jnp.concatenate([wkv[:, :, :MLA_NOPE].reshape(-1, BRANCH_WIDTH),
                           wkv[:, :, MLA_NOPE:].reshape(-1, BRANCH_WIDTH)], axis=1)
    return wq.reshape(r, N_HEADS * MLA_PAD).astype(BF16), wkv.astype(BF16)


def _branches(h, p, tabs):
    tab_p, tab_i, tab_m = tabs
    w_lat, w_sb, w_dil, w_dsa, w_idx = _split_w_in(p['w_in'])
    w_uq, w_ukv = _pad_mla_up(p['w_uq'], p['w_ukv'])

    cq, ckv, kpe = prep_latent(matmul(h, w_lat, out_dtype=F32), p['g_cq'], p['g_ckv'])
    qT, k, vT = prep_mla(matmul(cq, w_uq, out_dtype=F32), matmul(ckv, w_ukv, out_dtype=F32), kpe,
                         p['g_q_mla'], p['g_k_mla'], tab_m)
    o_mla = mla_attention(qT, k, vT)
    o_sb = sb_attention(*prep_sb(matmul(h, w_sb, out_dtype=F32)))
    o_dil = dil_attention(*prep_dil(matmul(h, w_dil, out_dtype=F32), p['g_q_dil'], p['g_k_dil'], tab_p))
    qsT, ks, vsT = prep_dsa(matmul(h, w_dsa, out_dtype=F32), p['g_q_dsa'], p['g_k_dsa'], tab_p)
    qiT, ki, wiT = prep_idx(matmul(h, w_idx, out_dtype=F32), tab_i)
    o_dsa = dsa_attention(qsT, qiT, wiT, ki, ks, vsT)
    return o_mla, o_sb, o_dil, o_dsa


def _token_mixer(x, h, p, tabs):
    o_mla, o_sb, o_dil, o_dsa = _branches(h, p, tabs)
    g_lat = matmul(h, p['w_gate_a'].astype(BF16), out_dtype=BF16)
    merged = gated_merge(g_lat, (o_mla, o_sb, o_dil, o_dsa), p['w_gate_b'].astype(BF16),
                         p['b_gate'].astype(F32), p['w_branch'].astype(BF16))
    return matmul(merged, p['w_out'].astype(BF16), out_dtype=F32, res=x)


def _cross_block(x, mem, p):
    m_n = rmsnorm_rows(mem, p['ln_mem'])
    k_raw = matmul(m_n, p['w_xk'].astype(BF16), out_dtype=F32)
    v = matmul(m_n, p['w_xv'].astype(BF16), out_dtype=BF16)
    return cross_attention(x, p['ln_xattn'], p['w_xq'].astype(BF16), p['g_q_x'], k_raw, p['g_k_x'], v,
                           p['w_xo'].astype(BF16))


def _ffn_block(x, g, wg, wu, wd):
    act = glu_up(rmsnorm_rows(x, g), wg.astype(BF16), wu.astype(BF16))
    f = wd.shape[0]
    tk = max(c for c in range(256, 2049, 256) if f % c == 0)
    return matmul(act, wd.astype(BF16), out_dtype=F32, res=x, tn=1024, tk=tk)


def _moe_block(x, g, w_router, wg, wu, wd):
    sel = router_top2(x, g, w_router)
    src, pos, tile_expert = _moe_plan(sel)
    hs = gather_norm(x, g, src)
    act = _grouped_call(_glu_grouped_kernel, "moe_glu_up", hs, [wg.astype(BF16), wu.astype(BF16)],
                        tile_expert, _pick(wg.shape[2], (256, 128)), BF16)
    y = _grouped_call(_mm_grouped_kernel, "moe_down", act, [wd.astype(BF16)], tile_expert,
                      _pick(wd.shape[2], (1024, 512, 256, 128)), F32)
    return moe_combine(x, sel, y, pos)


def kernel(x, mem, ln_mix, w_in, g_cq, g_ckv, w_uq, w_ukv, g_q_mla, g_k_mla, g_q_dil, g_k_dil, g_q_dsa, g_k_dsa, w_gate_a, w_gate_b, b_gate, w_branch, w_out, ln_xattn, ln_mem, w_xq, w_xk, w_xv, g_q_x, g_k_x, w_xo, ln_ffn, w_ff_gate, w_ff_up, w_ff_down, w_router, w_e_gate, w_e_up, w_e_down):
    b, s, d = x.shape
    per_layer = dict(w_in=w_in, g_cq=g_cq, g_ckv=g_ckv, w_uq=w_uq, w_ukv=w_ukv, g_q_mla=g_q_mla,
                     g_k_mla=g_k_mla, g_q_dil=g_q_dil, g_k_dil=g_k_dil, g_q_dsa=g_q_dsa, g_k_dsa=g_k_dsa,
                     w_gate_a=w_gate_a, w_gate_b=w_gate_b, b_gate=b_gate, w_branch=w_branch, w_out=w_out,
                     ln_xattn=ln_xattn, ln_mem=ln_mem, w_xq=w_xq, w_xk=w_xk, w_xv=w_xv, g_q_x=g_q_x,
                     g_k_x=g_k_x, w_xo=w_xo)
    tabs = (_rope_tables(s, ROT_DIM, HEAD_DIM), _rope_tables(s, IDX_ROT, IDX_DIM),
            _rope_tables(s, MLA_ROPE, HEAD_DIM))
    depth = ln_mix.shape[0]
    outs = []
    for bi in range(b):
        xb, mb = x[bi], mem[bi]
        for i in range(depth):
            p = {name: val[i] for name, val in per_layer.items()}
            xb = _token_mixer(xb, rmsnorm_rows(xb, ln_mix[i]), p, tabs)
            xb = _cross_block(xb, mb, p)
            j = i // 2
            if i % 2 == 0:
                xb = _ffn_block(xb, ln_ffn[i], w_ff_gate[j], w_ff_up[j], w_ff_down[j])
            else:
                xb = _moe_block(xb, ln_ffn[i], w_router[j], w_e_gate[j], w_e_up[j], w_e_down[j])
        outs.append(xb)
    return outs[0][None] if b == 1 else jnp.stack(outs, axis=0)
```

```python
import functools
import math

import numpy as np
import jax
import jax.numpy as jnp
from jax import lax
from jax.experimental import pallas as pl
from jax.experimental.pallas import tpu as pltpu

F32 = jnp.float32
BF16 = jnp.bfloat16
I32 = jnp.int32

N_BRANCHES = 4
HEAD_DIM = 128
N_HEADS = 8
BRANCH_WIDTH = N_HEADS * HEAD_DIM
ROT_DIM = HEAD_DIM // 4
ROPE_THETA = 500000.0
RMS_EPS = 1e-6
NEG = -1e30
GATE_RANK = 256
MLA_Q_LORA = 896
MLA_KV_LORA = 256
MLA_NOPE = 128
MLA_ROPE = 64
MLA_QK = MLA_NOPE + MLA_ROPE
MLA_PAD = 256
DIL_PATTERNS = ((128, 1), (512, 4), (2048, 16))
DSA_TOPK = 256
IDX_HEADS = 8
IDX_DIM = 64
IDX_ROT = IDX_DIM // 4
MEM_HEADS = 4
MEM_HEAD_DIM = 128
MEM_WIDTH = MEM_HEADS * MEM_HEAD_DIM
N_EXPERTS = 8
MOE_TOP_K = 2
INT_MIN = -2 ** 31
LOG2E = math.log2(math.e)

ATT_T = 512
SB_T = 512
HEAD_GROUP = 2
DSA_TQ = 128
COUNT_ROWS = 64
SB_EXIT = -105.0

V7X_VMEM_LIMIT_BYTES = 56 * 1024 * 1024
LANES = 128

NT_DIMS = (((1,), (1,)), ((), ()))


def _cparams(n_axes):
    return pltpu.CompilerParams(dimension_semantics=("arbitrary",) * n_axes,
                                vmem_limit_bytes=V7X_VMEM_LIMIT_BYTES)


def _pick(n, candidates):
    for c in candidates:
        if n % c == 0:
            return c
    return n


def _rmsnorm_kernel(x_ref, g_ref, o_ref):
    x = x_ref[...].astype(F32)
    ms = jnp.mean(x * x, axis=-1, keepdims=True)
    o_ref[...] = (x * lax.rsqrt(ms + RMS_EPS) * g_ref[...]).astype(o_ref.dtype)


def rmsnorm_rows(x, g, out_dtype=BF16):
    m, d = x.shape
    tm = _pick(m, (512, 256, 128))
    return pl.pallas_call(
        _rmsnorm_kernel, grid=(m // tm,),
        in_specs=[pl.BlockSpec((tm, d), lambda i: (i, 0)), pl.BlockSpec((1, d), lambda i: (0, 0))],
        out_specs=pl.BlockSpec((tm, d), lambda i: (i, 0)),
        out_shape=jax.ShapeDtypeStruct((m, d), out_dtype),
        compiler_params=_cparams(1), name="rmsnorm_rows",
    )(x, g.reshape(1, d).astype(F32))


def _mm_kernel(*refs, nk, has_res):
    a_ref, b_ref = refs[0], refs[1]
    r_ref = refs[2] if has_res else None
    o_ref = refs[2 + has_res]
    if nk == 1:
        part = jnp.dot(a_ref[...], b_ref[...], preferred_element_type=F32)
        if has_res:
            part = r_ref[...] + part
        o_ref[...] = part.astype(o_ref.dtype)
        return
    acc_ref = refs[3 + has_res]
    k = pl.program_id(2)

    @pl.when(k == 0)
    def _():
        acc_ref[...] = jnp.zeros(acc_ref.shape, F32)

    acc_ref[...] += jnp.dot(a_ref[...], b_ref[...], preferred_element_type=F32)

    @pl.when(k == nk - 1)
    def _():
        res = acc_ref[...]
        if has_res:
            res = r_ref[...] + res
        o_ref[...] = res.astype(o_ref.dtype)


def matmul(a, b, *, out_dtype, res=None, tm=None, tn=None, tk=None):
    m, kdim = a.shape
    n = b.shape[1]
    tm = tm or _pick(m, (1024, 512, 256, 128))
    tn = tn or _pick(n, (512, 256, 128))
    tk = tk or (kdim if kdim <= 4096 else _pick(kdim, (2048, 1024, 512)))
    nk = kdim // tk
    in_specs = [pl.BlockSpec((tm, tk), lambda i, j, k: (i, k)),
                pl.BlockSpec((tk, tn), lambda i, j, k: (k, j))]
    args = [a, b]
    if res is not None:
        in_specs.append(pl.BlockSpec((tm, tn), lambda i, j, k: (i, j)))
        args.append(res)
    scratch = [pltpu.VMEM((tm, tn), F32)] if nk > 1 else []
    return pl.pallas_call(
        functools.partial(_mm_kernel, nk=nk, has_res=res is not None),
        grid=(m // tm, n // tn, nk),
        in_specs=in_specs,
        out_specs=pl.BlockSpec((tm, tn), lambda i, j, k: (i, j)),
        out_shape=jax.ShapeDtypeStruct((m, n), out_dtype),
        scratch_shapes=scratch,
        compiler_params=_cparams(3), name="matmul",
    )(*args)


def _glu_kernel(a_ref, wg_ref, wu_ref, o_ref):
    a = a_ref[...]
    g = jnp.dot(a, wg_ref[...], preferred_element_type=F32)
    u = jnp.dot(a, wu_ref[...], preferred_element_type=F32)
    o_ref[...] = (g * jax.nn.sigmoid(g) * u).astype(o_ref.dtype)


def glu_up(a, wg, wu):
    m, kdim = a.shape
    f = wg.shape[1]
    tm = _pick(m, (1024, 512, 256, 128))
    tn = _pick(f, (512, 256, 128))
    w_spec = pl.BlockSpec((kdim, tn), lambda i, j: (0, j))
    return pl.pallas_call(
        _glu_kernel, grid=(m // tm, f // tn),
        in_specs=[pl.BlockSpec((tm, kdim), lambda i, j: (i, 0)), w_spec, w_spec],
        out_specs=pl.BlockSpec((tm, tn), lambda i, j: (i, j)),
        out_shape=jax.ShapeDtypeStruct((m, f), BF16),
        compiler_params=_cparams(2), name="glu_up",
    )(a, wg, wu)


MOE_TM = 512


def _glu_grouped_kernel(te_ref, a_ref, wg_ref, wu_ref, o_ref):
    _glu_kernel(a_ref, wg_ref, wu_ref, o_ref)


def _mm_grouped_kernel(te_ref, a_ref, w_ref, o_ref):
    o_ref[...] = jnp.dot(a_ref[...], w_ref[...], preferred_element_type=F32).astype(o_ref.dtype)


def _grouped_call(kernel, name, a, weights, tile_expert, tn, out_dtype):
    r, kdim = a.shape
    n = weights[0].shape[2]
    w_spec = pl.BlockSpec((None, kdim, tn), lambda j, i, te: (te[i], 0, j))
    grid_spec = pltpu.PrefetchScalarGridSpec(
        num_scalar_prefetch=1, grid=(n // tn, r // MOE_TM),
        in_specs=[pl.BlockSpec((MOE_TM, kdim), lambda j, i, te: (i, 0))] + [w_spec] * len(weights),
        out_specs=pl.BlockSpec((MOE_TM, tn), lambda j, i, te: (i, j)))
    return pl.pallas_call(
        kernel, grid_spec=grid_spec, out_shape=jax.ShapeDtypeStruct((r, n), out_dtype),
        compiler_params=_cparams(2), name=name,
    )(tile_expert, a, *weights)


def _row_gather(idx_ref, base, src_hbm, dst, sem, n, *, wait):
    def body(r, c):
        copy = pltpu.make_async_copy(src_hbm.at[pl.ds(idx_ref[base + r], 1)], dst.at[pl.ds(r, 1)], sem)
        if wait:
            copy.wait()
        else:
            copy.start()
        return c

    lax.fori_loop(0, n, body, 0, unroll=8)


def _gather_norm_kernel(src_ref, x_hbm, g_ref, o_ref, buf, sem):
    tm = o_ref.shape[0]
    base = pl.program_id(0) * tm
    _row_gather(src_ref, base, x_hbm, buf, sem, tm, wait=False)
    _row_gather(src_ref, base, x_hbm, buf, sem, tm, wait=True)
    x = buf[...]
    ms = jnp.mean(x * x, axis=-1, keepdims=True)
    o_ref[...] = (x * lax.rsqrt(ms + RMS_EPS) * g_ref[...]).astype(o_ref.dtype)


def gather_norm(x, g, src):
    d = x.shape[1]
    r = src.shape[0]
    grid_spec = pltpu.PrefetchScalarGridSpec(
        num_scalar_prefetch=1, grid=(r // MOE_TM,),
        in_specs=[pl.BlockSpec(memory_space=pl.ANY), pl.BlockSpec((1, d), lambda i, s: (0, 0))],
        out_specs=pl.BlockSpec((MOE_TM, d), lambda i, s: (i, 0)),
        scratch_shapes=[pltpu.VMEM((MOE_TM, d), F32), pltpu.SemaphoreType.DMA(())])
    return pl.pallas_call(
        _gather_norm_kernel, grid_spec=grid_spec, out_shape=jax.ShapeDtypeStruct((r, d), BF16),
        compiler_params=_cparams(1), name="gather_norm",
    )(src, x, g.reshape(1, d).astype(F32))


def _moe_combine_kernel(pos_ref, x_ref, sel_ref, y_hbm, o_ref, buf0, buf1, sems):
    tm = o_ref.shape[0]
    base0 = pl.program_id(0) * tm
    base1 = pos_ref.shape[0] // 2 + base0
    for wait in (False, True):
        _row_gather(pos_ref, base0, y_hbm, buf0, sems.at[0], tm, wait=wait)
        _row_gather(pos_ref, base1, y_hbm, buf1, sems.at[1], tm, wait=wait)
    sel = sel_ref[...]
    o_ref[...] = x_ref[...] + sel[:, 2:3] * buf0[...] + sel[:, 3:4] * buf1[...]


def moe_combine(x, sel, y, pos):
    s, d = x.shape
    tm = _pick(s, (256, 128))
    grid_spec = pltpu.PrefetchScalarGridSpec(
        num_scalar_prefetch=1, grid=(s // tm,),
        in_specs=[pl.BlockSpec((tm, d), lambda i, p: (i, 0)), pl.BlockSpec((tm, LANES), lambda i, p: (i, 0)),
                  pl.BlockSpec(memory_space=pl.ANY)],
        out_specs=pl.BlockSpec((tm, d), lambda i, p: (i, 0)),
        scratch_shapes=[pltpu.VMEM((tm, d), F32), pltpu.VMEM((tm, d), F32), pltpu.SemaphoreType.DMA((2,))])
    return pl.pallas_call(
        _moe_combine_kernel, grid_spec=grid_spec, out_shape=jax.ShapeDtypeStruct((s, d), F32),
        compiler_params=_cparams(1), name="moe_combine",
    )(pos.reshape(-1), x, sel, y)


def _moe_plan(sel):
    s = sel.shape[0]
    e = sel[:, :MOE_TOP_K].astype(I32).T.reshape(-1)
    onehot = (e[:, None] == jnp.arange(N_EXPERTS, dtype=I32)[None, :]).astype(I32)
    rank = jnp.take_along_axis(jnp.cumsum(onehot, axis=0), e[:, None], axis=1)[:, 0] - 1
    padded = (jnp.sum(onehot, axis=0) + MOE_TM - 1) // MOE_TM * MOE_TM
    ends = jnp.cumsum(padded)
    pos = (ends - padded)[e] + rank
    n_rows = MOE_TOP_K * s + N_EXPERTS * MOE_TM
    token = jnp.tile(jnp.arange(s, dtype=I32), MOE_TOP_K)
    src = jnp.zeros((n_rows,), I32).at[pos].set(token)
    tile_start = jnp.arange(n_rows // MOE_TM, dtype=I32) * MOE_TM
    tile_expert = jnp.minimum(jnp.searchsorted(ends, tile_start, side='right'), N_EXPERTS - 1).astype(I32)
    return src, pos.reshape(MOE_TOP_K, s).astype(I32), tile_expert


def _router_kernel(x_ref, g_ref, w_ref, o_ref):
    x = x_ref[...]
    ms = jnp.mean(x * x, axis=-1, keepdims=True)
    h = x * lax.rsqrt(ms + RMS_EPS) * g_ref[...]
    logits = jnp.dot(h, w_ref[...], preferred_element_type=F32, precision=lax.Precision.HIGHEST)
    lane = lax.broadcasted_iota(I32, logits.shape, 1).astype(F32)
    logits = jnp.where(lane < N_EXPERTS, logits, -jnp.inf)
    m1 = jnp.max(logits, axis=1, keepdims=True)
    i1 = jnp.min(jnp.where(logits == m1, lane, float(LANES)), axis=1, keepdims=True)
    rest = jnp.where(lane == i1, -jnp.inf, logits)
    m2 = jnp.max(rest, axis=1, keepdims=True)
    i2 = jnp.min(jnp.where(rest == m2, lane, float(LANES)), axis=1, keepdims=True)
    e2 = jnp.exp(m2 - m1)
    w1 = 1.0 / (1.0 + e2)
    w2 = e2 / (1.0 + e2)
    o_ref[...] = (jnp.where(lane == 0.0, i1, 0.0) + jnp.where(lane == 1.0, i2, 0.0)
                  + jnp.where(lane == 2.0, w1, 0.0) + jnp.where(lane == 3.0, w2, 0.0))


def router_top2(x, g, w_router):
    m, d = x.shape
    tm = _pick(m, (256, 128))
    w = jnp.zeros((d, LANES), F32).at[:, :N_EXPERTS].set(w_router.astype(F32))
    return pl.pallas_call(
        _router_kernel, grid=(m // tm,),
        in_specs=[pl.BlockSpec((tm, d), lambda i: (i, 0)), pl.BlockSpec((1, d), lambda i: (0, 0)),
                  pl.BlockSpec((d, LANES), lambda i: (0, 0))],
        out_specs=pl.BlockSpec((tm, LANES), lambda i: (i, 0)),
        out_shape=jax.ShapeDtypeStruct((m, LANES), F32),
        compiler_params=_cparams(1), name="router_top2",
    )(x, g.reshape(1, d).astype(F32), w)


def _rope(x, tab_ref, half):
    w = x.shape[-1]
    return (x * tab_ref[0] + pltpu.roll(x, w - half, 1) * tab_ref[1]
            + pltpu.roll(x, half, 1) * tab_ref[2])


def _rope_tables(seq, rot_dim, period):
    half = rot_dim // 2
    inv_freq = ROPE_THETA ** (-jnp.arange(0, rot_dim, 2, dtype=F32) / rot_dim)
    ang = jnp.arange(seq, dtype=F32)[:, None] * inv_freq[None, :]
    cos, sin = jnp.cos(ang), jnp.sin(ang)
    ones = jnp.ones((seq, period - rot_dim), F32)
    zeros = jnp.zeros((seq, period - rot_dim), F32)
    zh = jnp.zeros((seq, half), F32)
    c = jnp.concatenate([cos, cos, ones], axis=1)
    sa = jnp.concatenate([-sin, zh, zeros], axis=1)
    sb = jnp.concatenate([zh, sin, zeros], axis=1)
    rep = LANES // period
    return jnp.stack([jnp.tile(c, (1, rep)), jnp.tile(sa, (1, rep)), jnp.tile(sb, (1, rep))], axis=0)


def _tab_spec(tm):
    return pl.BlockSpec((3, tm, LANES), lambda i: (0, i, 0))


def _row_spec(tm, w):
    return pl.BlockSpec((tm, w), lambda i: (i, 0))


def _const_spec(w):
    return pl.BlockSpec((1, w), lambda i: (0, 0))


def _prep_latent_kernel(p_ref, gq_ref, gkv_ref, cq_ref, ckv_ref, kpe_ref):
    cq = p_ref[:, :MLA_Q_LORA]
    ms = jnp.mean(cq * cq, axis=-1, keepdims=True)
    cq_ref[...] = (cq * lax.rsqrt(ms + RMS_EPS) * gq_ref[...]).astype(cq_ref.dtype)
    ckv = p_ref[:, MLA_Q_LORA:MLA_Q_LORA + MLA_KV_LORA]
    ms = jnp.mean(ckv * ckv, axis=-1, keepdims=True)
    ckv_ref[...] = (ckv * lax.rsqrt(ms + RMS_EPS) * gkv_ref[...]).astype(ckv_ref.dtype)
    kpe_ref[...] = p_ref[:, MLA_Q_LORA + MLA_KV_LORA:]


def prep_latent(p, g_cq, g_ckv):
    s = p.shape[0]
    tm = _pick(s, (512, 256, 128))
    return pl.pallas_call(
        _prep_latent_kernel, grid=(s // tm,),
        in_specs=[_row_spec(tm, p.shape[1]), _const_spec(MLA_Q_LORA), _const_spec(MLA_KV_LORA)],
        out_specs=[_row_spec(tm, MLA_Q_LORA), _row_spec(tm, MLA_KV_LORA), _row_spec(tm, LANES)],
        out_shape=[jax.ShapeDtypeStruct((s, MLA_Q_LORA), BF16),
                   jax.ShapeDtypeStruct((s, MLA_KV_LORA), BF16),
                   jax.ShapeDtypeStruct((s, LANES), F32)],
        compiler_params=_cparams(1), name="prep_latent",
    )(p, g_cq.reshape(1, -1), g_ckv.reshape(1, -1))


def _prep_mla_kernel(qup_ref, kvup_ref, kpe_ref, gq_ref, gk_ref, tab_ref, qT_ref, k_ref, vT_ref):
    half = MLA_ROPE // 2
    gqn, gqr = gq_ref[:, :LANES], gq_ref[:, LANES:]
    gkn, gkr = gk_ref[:, :LANES], gk_ref[:, LANES:]
    kpe = kpe_ref[...]
    kpe_ss = jnp.sum(kpe * kpe, axis=-1, keepdims=True)
    kr_base = _rope(kpe * gkr, tab_ref, half)
    scale = LOG2E * MLA_QK ** -0.5
    for h in range(N_HEADS):
        qn = qup_ref[:, h * MLA_PAD:h * MLA_PAD + LANES]
        qr = qup_ref[:, h * MLA_PAD + LANES:(h + 1) * MLA_PAD]
        ss = jnp.sum(qn * qn, axis=-1, keepdims=True) + jnp.sum(qr * qr, axis=-1, keepdims=True)
        r = lax.rsqrt(ss * (1.0 / MLA_QK) + RMS_EPS) * scale
        qT_ref[h, :LANES, :] = (qn * r * gqn).T.astype(BF16)
        qT_ref[h, LANES:, :] = _rope(qr * r * gqr, tab_ref, half).T.astype(BF16)
        kn = kvup_ref[:, h * LANES:(h + 1) * LANES]
        ss = jnp.sum(kn * kn, axis=-1, keepdims=True) + kpe_ss
        r = lax.rsqrt(ss * (1.0 / MLA_QK) + RMS_EPS)
        k_ref[:, h * MLA_PAD:h * MLA_PAD + LANES] = (kn * r * gkn).astype(BF16)
        k_ref[:, h * MLA_PAD + LANES:(h + 1) * MLA_PAD] = (kr_base * r).astype(BF16)
        vT_ref[h] = _vT_bf16(kvup_ref[:, BRANCH_WIDTH + h * LANES:BRANCH_WIDTH + (h + 1) * LANES])


def _qT_spec(d, t):
    return pl.BlockSpec((N_HEADS, d, t), lambda i: (0, 0, i))


def _vT_spec(t):
    return pl.BlockSpec((N_HEADS, None, HEAD_DIM, t), lambda i: (0, i, 0, 0))


def _qkv_shapes(s, dq, wk, t):
    return [jax.ShapeDtypeStruct((N_HEADS, dq, s), BF16), jax.ShapeDtypeStruct((s, wk), BF16),
            jax.ShapeDtypeStruct((N_HEADS, s // t, HEAD_DIM, t), BF16)]


def _vT_bf16(v):
    return v.T.astype(BF16)


def prep_mla(qup, kvup, kpe, g_q, g_k, tab):
    s = qup.shape[0]
    t = ATT_T
    pad = lambda g: jnp.zeros((1, MLA_PAD), F32).at[0, :MLA_QK].set(g)
    wq = N_HEADS * MLA_PAD
    return pl.pallas_call(
        _prep_mla_kernel, grid=(s // t,),
        in_specs=[_row_spec(t, wq), _row_spec(t, 2 * BRANCH_WIDTH), _row_spec(t, LANES),
                  _const_spec(MLA_PAD), _const_spec(MLA_PAD), _tab_spec(t)],
        out_specs=[_qT_spec(MLA_PAD, t), _row_spec(t, wq), _vT_spec(t)],
        out_shape=_qkv_shapes(s, MLA_PAD, wq, t),
        compiler_params=_cparams(1), name="prep_mla",
    )(qup, kvup, kpe, pad(g_q), pad(g_k), tab)


def _head_norm_rope(x, g, tab_ref, scale):
    ms = jnp.mean(x * x, axis=-1, keepdims=True)
    y = x * (lax.rsqrt(ms + RMS_EPS) * scale) * g
    return _rope(y, tab_ref, ROT_DIM // 2)


def _prep_sb_kernel(p_ref, qT_ref, k_ref, vT_ref, kn_ref):
    for h in range(N_HEADS):
        sl = slice(h * LANES, (h + 1) * LANES)
        qT_ref[h] = (p_ref[:, sl] * HEAD_DIM ** -0.5).T.astype(BF16)
        vT_ref[h] = p_ref[:, 2 * BRANCH_WIDTH + h * LANES:2 * BRANCH_WIDTH + (h + 1) * LANES].T.astype(BF16)
        kb = p_ref[:, BRANCH_WIDTH + h * LANES:BRANCH_WIDTH + (h + 1) * LANES].astype(BF16)
        k_ref[:, sl] = kb
        kf = kb.astype(F32)
        kn_ref[h] = jnp.zeros((8, LANES), F32) + jnp.max(jnp.sum(kf * kf, axis=1, keepdims=True))


def prep_sb(p):
    s = p.shape[0]
    t = SB_T
    return pl.pallas_call(
        _prep_sb_kernel, grid=(s // t,),
        in_specs=[_row_spec(t, 3 * BRANCH_WIDTH)],
        out_specs=[_qT_spec(HEAD_DIM, t), _row_spec(t, BRANCH_WIDTH), _vT_spec(t),
                   pl.BlockSpec((N_HEADS, None, 8, LANES), lambda i: (0, i, 0, 0))],
        out_shape=_qkv_shapes(s, HEAD_DIM, BRANCH_WIDTH, t)
        + [jax.ShapeDtypeStruct((N_HEADS, s // t, 8, LANES), F32)],
        compiler_params=_cparams(1), name="prep_sb",
    )(p)


def _prep_dil_kernel(p_ref, gq_ref, gk_ref, tab_ref, qT_ref, k_ref, vT_ref):
    gq, gk = gq_ref[...], gk_ref[...]
    for h in range(N_HEADS):
        sl = slice(h * LANES, (h + 1) * LANES)
        qT_ref[h] = _head_norm_rope(p_ref[:, sl], gq, tab_ref, LOG2E * HEAD_DIM ** -0.5).T.astype(BF16)
        ksl = slice(BRANCH_WIDTH + h * LANES, BRANCH_WIDTH + (h + 1) * LANES)
        k_ref[:, sl] = _head_norm_rope(p_ref[:, ksl], gk, tab_ref, 1.0).astype(BF16)
        vT_ref[h] = _vT_bf16(p_ref[:, 2 * BRANCH_WIDTH + h * LANES:2 * BRANCH_WIDTH + (h + 1) * LANES])


def prep_dil(p, g_q, g_k, tab):
    s = p.shape[0]
    t = ATT_T
    return pl.pallas_call(
        _prep_dil_kernel, grid=(s // t,),
        in_specs=[_row_spec(t, 3 * BRANCH_WIDTH), _const_spec(LANES), _const_spec(LANES), _tab_spec(t)],
        out_specs=[_qT_spec(HEAD_DIM, t), _row_spec(t, BRANCH_WIDTH), _vT_spec(t)],
        out_shape=_qkv_shapes(s, HEAD_DIM, BRANCH_WIDTH, t),
        compiler_params=_cparams(1), name="prep_dil",
    )(p, g_q.reshape(1, -1), g_k.reshape(1, -1), tab)


def _prep_dsa_kernel(p_ref, gq_ref, gk_ref, tab_ref, qT_ref, k_ref, vT_ref):
    gq, gk = gq_ref[...], gk_ref[...]
    for h in range(N_HEADS):
        sl = slice(h * LANES, (h + 1) * LANES)
        qT = _head_norm_rope(p_ref[:, sl], gq, tab_ref, LOG2E * HEAD_DIM ** -0.5).T.astype(BF16)
        for b in range(p_ref.shape[0] // DSA_TQ):
            qT_ref[b, :, h * DSA_TQ:(h + 1) * DSA_TQ] = qT[:, b * DSA_TQ:(b + 1) * DSA_TQ]
    k_ref[...] = _head_norm_rope(p_ref[:, BRANCH_WIDTH:BRANCH_WIDTH + LANES], gk, tab_ref, 1.0).astype(BF16)
    vT_ref[...] = _vT_bf16(p_ref[:, BRANCH_WIDTH + LANES:])


def prep_dsa(p, g_q, g_k, tab):
    s = p.shape[0]
    t = ATT_T
    nqb = t // DSA_TQ
    return pl.pallas_call(
        _prep_dsa_kernel, grid=(s // t,),
        in_specs=[_row_spec(t, BRANCH_WIDTH + 2 * LANES), _const_spec(LANES), _const_spec(LANES), _tab_spec(t)],
        out_specs=[pl.BlockSpec((nqb, HEAD_DIM, N_HEADS * DSA_TQ), lambda i: (i, 0, 0)), _row_spec(t, LANES),
                   pl.BlockSpec((None, HEAD_DIM, t), lambda i: (i, 0, 0))],
        out_shape=[jax.ShapeDtypeStruct((s // DSA_TQ, HEAD_DIM, N_HEADS * DSA_TQ), BF16),
                   jax.ShapeDtypeStruct((s, LANES), BF16),
                   jax.ShapeDtypeStruct((s // t, HEAD_DIM, t), BF16)],
        compiler_params=_cparams(1), name="prep_dsa",
    )(p, g_q.reshape(1, -1), g_k.reshape(1, -1), tab)


def _prep_idx_kernel(p_ref, tab_ref, qiT_ref, ki_ref, wiT_ref):
    half = IDX_ROT // 2
    t = p_ref.shape[0]
    nqb = t // DSA_TQ
    lane = lax.broadcasted_iota(I32, (t, LANES), 1)
    first = lane < IDX_DIM
    zero = jnp.zeros((t, LANES), F32)
    for b in range(IDX_HEADS // 2):
        x = _rope(p_ref[:, b * LANES:(b + 1) * LANES], tab_ref, half) * (IDX_DIM ** -0.5)
        hi = x.astype(BF16).astype(F32)
        lo = x - hi
        rhi = pltpu.roll(hi, IDX_DIM, 1)
        for hh, (a0, a1) in enumerate(((jnp.where(first, hi, pltpu.roll(lo, IDX_DIM, 1)), jnp.where(first, hi, zero)),
                                       (jnp.where(first, rhi, lo), jnp.where(first, rhi, zero)))):
            h = 2 * b + hh
            a0T, a1T = a0.T.astype(BF16), a1.T.astype(BF16)
            for qb in range(nqb):
                qs = slice(qb * DSA_TQ, (qb + 1) * DSA_TQ)
                qiT_ref[qb, :LANES, h * DSA_TQ:(h + 1) * DSA_TQ] = a0T[:, qs]
                qiT_ref[qb, LANES:, h * DSA_TQ:(h + 1) * DSA_TQ] = a1T[:, qs]
    kx = _rope(p_ref[:, IDX_HEADS * IDX_DIM:IDX_HEADS * IDX_DIM + LANES], tab_ref, half)
    hi = kx.astype(BF16).astype(F32)
    lo = kx - hi
    ki_ref[:, :LANES] = jnp.where(first, hi, pltpu.roll(hi, IDX_DIM, 1)).astype(BF16)
    ki_ref[:, LANES:] = jnp.where(first, lo, zero).astype(BF16)
    wT = (p_ref[:, IDX_HEADS * IDX_DIM + LANES:] * (IDX_HEADS ** -0.5)).T
    for qb in range(nqb):
        wiT_ref[qb] = wT[:IDX_HEADS, qb * DSA_TQ:(qb + 1) * DSA_TQ]


def prep_idx(p, tab):
    s = p.shape[0]
    t = ATT_T
    nqb = t // DSA_TQ
    return pl.pallas_call(
        _prep_idx_kernel, grid=(s // t,),
        in_specs=[_row_spec(t, IDX_HEADS * IDX_DIM + 2 * LANES), _tab_spec(t)],
        out_specs=[pl.BlockSpec((nqb, 2 * LANES, IDX_HEADS * DSA_TQ), lambda i: (i, 0, 0)),
                   _row_spec(t, 2 * LANES),
                   pl.BlockSpec((nqb, IDX_HEADS, DSA_TQ), lambda i: (i, 0, 0))],
        out_shape=[jax.ShapeDtypeStruct((s // DSA_TQ, 2 * LANES, IDX_HEADS * DSA_TQ), BF16),
                   jax.ShapeDtypeStruct((s, 2 * LANES), BF16),
                   jax.ShapeDtypeStruct((s // DSA_TQ, IDX_HEADS, DSA_TQ), F32)],
        compiler_params=_cparams(1), name="prep_idx",
    )(p, tab)


def _softmax_step(sT, vT, m_sc, acc_sc):
    l_sc, o_sc = acc_sc
    m_prev = m_sc[...]
    m_new = jnp.maximum(m_prev, jnp.max(sT, axis=0, keepdims=True))
    alpha = jnp.exp2(m_prev - m_new)
    p = jnp.exp2(sT - m_new)
    l_sc[...] = alpha * l_sc[...] + jnp.sum(p, axis=0, keepdims=True)
    o_sc[...] = alpha * o_sc[...] + jnp.dot(vT, p.astype(BF16), preferred_element_type=F32)
    m_sc[...] = m_new


def _softmax_init(m_sc, acc_sc):
    m_sc[...] = jnp.full(m_sc.shape, NEG, F32)
    for ref in acc_sc:
        ref[...] = jnp.zeros(ref.shape, F32)


def _softmax_result(acc_sc):
    l_sc, o_sc = acc_sc
    return o_sc[...] / l_sc[...]


def _pipelined_blocks(n_full, scores, consume, consume_last, s_sc):
    s_sc[0] = scores(0)

    def pair(jj, c):
        s_sc[1] = scores(2 * jj + 1)
        consume(s_sc[0], 2 * jj)
        s_sc[0] = scores(2 * jj + 2)
        consume(s_sc[1], 2 * jj + 1)
        return c

    lax.fori_loop(0, n_full // 2, pair, 0)

    @pl.when(n_full % 2 == 1)
    def _():
        s_sc[1] = scores(n_full)
        consume(s_sc[0], n_full - 1)
        consume_last(s_sc[1], n_full)

    @pl.when(n_full % 2 == 0)
    def _():
        consume_last(s_sc[0], n_full)


def _key_le_query(tk, tq):
    return lax.broadcasted_iota(I32, (tk, tq), 0) <= lax.broadcasted_iota(I32, (tk, tq), 1)


def _head_group_attention(qT_ref, k_ref, vT_ref, o_ref, s_sc, m_sc, l_sc, o_sc, *,
                          n_full, block_of, bias_of, causal_last):
    g, _, t = qT_ref.shape
    w = k_ref.shape[2] // g
    accs = [(l_sc.at[h], o_sc.at[h]) for h in range(g)]
    for h in range(g):
        _softmax_init(m_sc.at[h], accs[h])

    def scores(d):
        kb = k_ref[block_of(d)]
        return jnp.stack([jnp.dot(kb[:, h * w:(h + 1) * w], qT_ref[h], preferred_element_type=F32)
                          for h in range(g)])

    def step(sT, d, causal=False):
        bias = None if bias_of is None else bias_of(d)
        for h in range(g):
            s_h = sT[h] if bias is None else sT[h] + bias
            if causal:
                s_h = jnp.where(_key_le_query(t, t), s_h, NEG)
            _softmax_step(s_h, vT_ref[h, block_of(d)], m_sc.at[h], accs[h])

    _pipelined_blocks(n_full, scores, step, functools.partial(step, causal=causal_last), s_sc)
    for h in range(g):
        o_ref[:, h * HEAD_DIM:(h + 1) * HEAD_DIM] = _softmax_result(accs[h]).T.astype(o_ref.dtype)


def _head_group_call(kernel, name, qT, k, vT, extra=()):
    n_heads, dq, s = qT.shape
    t = ATT_T
    nb = s // t
    g = HEAD_GROUP
    once = pl.Buffered(1)
    whole = lambda a: pl.BlockSpec(a.shape, lambda h, i: (0,) * a.ndim, pipeline_mode=once)
    return pl.pallas_call(
        kernel, grid=(n_heads // g, nb),
        in_specs=[pl.BlockSpec((g, dq, t), lambda h, i: (h, 0, i)),
                  pl.BlockSpec((nb, t, g * dq), lambda h, i: (0, 0, h), pipeline_mode=once),
                  pl.BlockSpec((g, nb, HEAD_DIM, t), lambda h, i: (h, 0, 0, 0), pipeline_mode=once)]
        + [whole(a) for a in extra],
        out_specs=pl.BlockSpec((t, g * HEAD_DIM), lambda h, i: (i, h)),
        out_shape=jax.ShapeDtypeStruct((s, n_heads * HEAD_DIM), BF16),
        scratch_shapes=[pltpu.VMEM((2, g, t, t), F32), pltpu.VMEM((g, 1, t), F32), pltpu.VMEM((g, 1, t), F32),
                        pltpu.VMEM((g, HEAD_DIM, t), F32)],
        compiler_params=_cparams(2), name=name,
    )(qT, k.reshape(nb, t, -1), vT, *extra)


def _mla_kernel(qT_ref, k_ref, vT_ref, o_ref, *scratch):
    _head_group_attention(qT_ref, k_ref, vT_ref, o_ref, *scratch, n_full=pl.program_id(1),
                          block_of=lambda j: j, bias_of=None, causal_last=True)


def _attn_scratch(tk, tq):
    return [pltpu.VMEM((2, tk, tq), F32), pltpu.VMEM((1, tq), F32), pltpu.VMEM((1, tq), F32),
            pltpu.VMEM((HEAD_DIM, tq), F32)]


def _head_qT_spec(d, t):
    return pl.BlockSpec((None, d, t), lambda h, i: (h, 0, i))


def _head_k_spec(nb, t, w):
    return pl.BlockSpec((nb, t, w), lambda h, i: (0, 0, h))


def _head_vT_spec(nb, t):
    return pl.BlockSpec((None, nb, HEAD_DIM, t), lambda h, i: (h, 0, 0, 0))


def _head_out_spec(t):
    return pl.BlockSpec((t, HEAD_DIM), lambda h, i: (i, h))


def mla_attention(qT, k, vT):
    return _head_group_call(_mla_kernel, "mla_attention", qT, k, vT)


SB_SUB = 128


def _sb_kernel(qT_ref, k_ref, vT_ref, u_ref, kn_ref, o_ref, carry_sc, acc_sc):
    i = pl.program_id(1)
    t = qT_ref.shape[1]
    qT = qT_ref[...]
    u2 = u_ref[...]
    qf = qT.astype(F32)
    zbound = jnp.sqrt(jnp.sum(qf * qf, axis=0, keepdims=True) * jnp.max(kn_ref[...]))
    carry_sc[...] = jnp.zeros(carry_sc.shape, F32)
    acc_sc[...] = jnp.zeros(acc_sc.shape, F32)
    key = lax.broadcasted_iota(I32, (SB_SUB, t), 0)
    qry = lax.broadcasted_iota(I32, (SB_SUB, t), 1)

    def block(j, diag):
        zT = jnp.dot(k_ref[j], qT, preferred_element_type=F32)
        carry = carry_sc[...]
        parts = [None] * (t // SB_SUB)
        for c in reversed(range(t // SB_SUB)):
            first = c * SB_SUB if diag else 0
            zc = zT[c * SB_SUB:(c + 1) * SB_SUB, first:]
            ls = jnp.minimum(-zc, 0.0) - jnp.log(1.0 + jnp.exp(-jnp.abs(zc)))
            if diag:
                past = (key + c * SB_SUB < qry)[:, first:]
                ls = jnp.where(past, ls, 0.0)
            hi = ls.astype(BF16)
            lo = (ls - hi.astype(F32)).astype(BF16)
            rev = jnp.dot(u2, jnp.concatenate([hi, lo], axis=0), preferred_element_type=F32) + carry[:, first:]
            a = jnp.exp(jnp.minimum(zc + rev, 0.0))
            if diag:
                a = jnp.where(past, a, 0.0)
            a = a.astype(BF16)
            parts[c] = a if first == 0 else jnp.concatenate([jnp.zeros((SB_SUB, first), BF16), a], axis=1)
            carry = rev[0:1, :] if first == 0 else jnp.concatenate([carry[:, :first], rev[0:1, :]], axis=1)
        carry_sc[...] = carry
        acc_sc[...] += jnp.dot(vT_ref[j], jnp.concatenate(parts, axis=0), preferred_element_type=F32)

    def all_underflow():
        return (jnp.max(carry_sc[...] + zbound) < SB_EXIT).astype(I32)

    block(i, True)

    def back(state):
        jj, _ = state
        block(i - 1 - jj, False)
        return jj + 1, all_underflow()

    lax.while_loop(lambda st: jnp.logical_and(st[0] < i, st[1] == 0), back, (jnp.int32(0), all_underflow()))
    o_ref[...] = acc_sc[...].T.astype(o_ref.dtype)


def sb_attention(qT, k, vT, kn):
    s = k.shape[0]
    t = SB_T
    nb = s // t
    tri = (np.arange(SB_SUB)[None, :] >= np.arange(SB_SUB)[:, None]).astype(np.float32)
    u2 = jnp.asarray(np.concatenate([tri, tri], axis=1), dtype=BF16)
    return pl.pallas_call(
        _sb_kernel, grid=(N_HEADS, nb),
        in_specs=[_head_qT_spec(HEAD_DIM, t), _head_k_spec(nb, t, HEAD_DIM), _head_vT_spec(nb, t),
                  pl.BlockSpec((SB_SUB, 2 * SB_SUB), lambda h, i: (0, 0)),
                  pl.BlockSpec((None, nb, 8, LANES), lambda h, i: (h, 0, 0, 0))],
        out_specs=_head_out_spec(t),
        out_shape=jax.ShapeDtypeStruct((s, BRANCH_WIDTH), BF16),
        scratch_shapes=[pltpu.VMEM((1, t), F32), pltpu.VMEM((HEAD_DIM, t), F32)],
        compiler_params=_cparams(2), name="sb_attention",
    )(qT, k.reshape(nb, t, -1), vT, u2, kn)


def _dil_log_weights(t):
    span = max(w for w, _ in DIL_PATTERNS)
    nback = -(-span // t)
    d = np.arange(nback + 1)[:, None, None] * t + np.arange(t)[None, None, :] - np.arange(t)[None, :, None]
    mult = np.zeros(d.shape, np.float64)
    for window, dil in DIL_PATTERNS:
        mult += ((d >= 0) & (d <= window) & (d % dil == 0))
    return np.where(mult > 0, np.log2(np.maximum(mult, 1.0)), NEG).astype(np.float32)


def _dil_kernel(qT_ref, k_ref, vT_ref, w_ref, o_ref, *scratch):
    i = pl.program_id(1)
    nback = w_ref.shape[0] - 1
    _head_group_attention(qT_ref, k_ref, vT_ref, o_ref, *scratch, n_full=jnp.minimum(i, nback),
                          block_of=lambda d: i - d, bias_of=lambda d: w_ref[d], causal_last=False)


def dil_attention(qT, k, vT):
    return _head_group_call(_dil_kernel, "dil_attention", qT, k, vT, extra=(jnp.asarray(_dil_log_weights(ATT_T)),))


def _dsa_kernel(qsT_ref, qiT_ref, wiT_ref, ki_ref, k_ref, vT_ref, o_ref,
                key_sc, q1_sc, s_sc, m_sc, l_sc, o_sc, *, top_k, pos_bits):
    acc_sc = (l_sc, o_sc)
    i = pl.program_id(0)
    tq, tk = DSA_TQ, ATT_T
    nkb = (i * tq + tq + tk - 1) // tk
    qiT = qiT_ref[...]
    wiT = wiT_ref[...]
    row = lax.broadcasted_iota(I32, (tk, tq), 1) + i * tq
    col = lax.broadcasted_iota(I32, (tk, tq), 0)

    def idx_block(aT, j):
        idx = jnp.zeros((tk, tq), F32)
        for h in range(IDX_HEADS):
            idx = idx + jnp.maximum(aT[:, h * tq:(h + 1) * tq], 0.0) * wiT[h:h + 1, :]
        bits = pltpu.bitcast(idx + 0.0, I32)
        key = bits ^ ((bits >> 31) & 0x7FFFFFFF)
        key_sc[j] = jnp.where(col + j * tk <= row, key, INT_MIN)

    _pipelined_blocks(nkb - 1, lambda j: jnp.dot(ki_ref[j], qiT, preferred_element_type=F32),
                      idx_block, idx_block, s_sc)

    def count(hits):
        def body(j, acc):
            hit = hits(key_sc[j], j)
            return acc + jnp.sum(hit.reshape(tk // COUNT_ROWS, COUNT_ROWS, tq), axis=0)
        acc = lax.fori_loop(0, nkb, body, jnp.zeros((COUNT_ROWS, tq), F32))
        return jnp.sum(acc, axis=0, keepdims=True)

    def bit_body(b, carry):
        thr, cge = carry
        cand = thr + jnp.left_shift(jnp.int32(1), 31 - b)
        c = count(lambda kb, j: jnp.where(kb >= cand, 1.0, 0.0))
        ok = c >= top_k
        return jnp.where(ok, cand, thr), jnp.where(ok, c, cge)

    thr0 = jnp.full((1, tq), INT_MIN, I32)
    cge0 = jnp.zeros((1, tq), F32) + (nkb * tk).astype(F32)
    thr, cge = lax.fori_loop(0, 32, bit_body, (thr0, cge0))

    q1_sc[...] = jnp.full((1, tq), 2 ** 30, I32)

    @pl.when(jnp.max(cge) > top_k)
    def _():
        need = top_k - count(lambda kb, j: jnp.where(kb > thr, 1.0, 0.0))

        def pos_body(b, qpos):
            cand = qpos + jnp.left_shift(jnp.int32(1), pos_bits - 1 - b)
            g = count(lambda kb, j: jnp.where(kb == thr, jnp.where(col + j * tk < cand, 1.0, 0.0), 0.0))
            return jnp.where(g < need, cand, qpos)

        q1_sc[...] = lax.fori_loop(0, pos_bits, pos_body, jnp.zeros((1, tq), I32)) + 1

    q1 = q1_sc[...]
    qsT = qsT_ref[...]
    _softmax_init(m_sc, acc_sc)
    def att_block(sT, j):
        kb = key_sc[j]
        pos = col + j * tk
        bias = jnp.where(kb > thr, 0.0, jnp.where(kb == thr, jnp.where(pos < q1, 0.0, NEG), NEG))
        bias = jnp.where(pos <= row, bias, NEG)
        _softmax_step(sT + jnp.concatenate([bias] * N_HEADS, axis=1), vT_ref[j], m_sc, acc_sc)

    _pipelined_blocks(nkb - 1, lambda j: jnp.dot(k_ref[j], qsT, preferred_element_type=F32),
                      att_block, att_block, s_sc)
    o = _softmax_result(acc_sc)
    for h in range(N_HEADS):
        sl = slice(h * HEAD_DIM, (h + 1) * HEAD_DIM)
        o_ref[:, sl] = o[:, h * tq:(h + 1) * tq].T.astype(o_ref.dtype)


def dsa_attention(qsT, qiT, wiT, ki, k, vT):
    s = k.shape[0]
    tq, tk = DSA_TQ, ATT_T
    nkb = s // tk
    top_k = min(DSA_TOPK, s // 4)
    wq = N_HEADS * tq
    full3 = lambda a, b: pl.BlockSpec((nkb, a, b), lambda i: (0, 0, 0))
    return pl.pallas_call(
        functools.partial(_dsa_kernel, top_k=top_k, pos_bits=(s - 1).bit_length()),
        grid=(s // tq,),
        in_specs=[pl.BlockSpec((None, HEAD_DIM, wq), lambda i: (i, 0, 0)),
                  pl.BlockSpec((None, 2 * LANES, wq), lambda i: (i, 0, 0)),
                  pl.BlockSpec((None, IDX_HEADS, tq), lambda i: (i, 0, 0)),
                  full3(tk, 2 * LANES), full3(tk, HEAD_DIM), full3(HEAD_DIM, tk)],
        out_specs=pl.BlockSpec((tq, BRANCH_WIDTH), lambda i: (i, 0)),
        out_shape=jax.ShapeDtypeStruct((s, BRANCH_WIDTH), BF16),
        scratch_shapes=[pltpu.VMEM((nkb, tk, tq), I32), pltpu.VMEM((1, tq), I32)] + _attn_scratch(tk, wq),
        compiler_params=_cparams(1), name="dsa_attention",
    )(qsT, qiT, wiT, ki.reshape(nkb, tk, -1), k.reshape(nkb, tk, -1), vT)


def _merge_kernel(g_ref, o0_ref, o1_ref, o2_ref, o3_ref, wgb_ref, b_ref, wbr_ref, out_ref):
    g = g_ref[...]
    acc = None
    for n, o_ref in enumerate((o0_ref, o1_ref, o2_ref, o3_ref)):
        gate = jax.nn.sigmoid(jnp.dot(g, wgb_ref[n], preferred_element_type=F32) + b_ref[n])
        val = gate * jnp.dot(o_ref[...], wbr_ref[n], preferred_element_type=F32)
        acc = val if acc is None else acc + val
    out_ref[...] = acc.astype(out_ref.dtype)


def gated_merge(g_lat, outs, w_gate_b, b_gate, w_branch):
    s = g_lat.shape[0]
    d = w_gate_b.shape[-1]
    tm = _pick(s, (1024, 512, 256, 128))
    tn = _pick(d, (512, 256, 128))
    o_spec = pl.BlockSpec((tm, BRANCH_WIDTH), lambda i, j: (i, 0))
    return pl.pallas_call(
        _merge_kernel, grid=(s // tm, d // tn),
        in_specs=[pl.BlockSpec((tm, GATE_RANK), lambda i, j: (i, 0)), o_spec, o_spec, o_spec, o_spec,
                  pl.BlockSpec((N_BRANCHES, GATE_RANK, tn), lambda i, j: (0, 0, j)),
                  pl.BlockSpec((N_BRANCHES, 1, tn), lambda i, j: (0, 0, j)),
                  pl.BlockSpec((N_BRANCHES, BRANCH_WIDTH, tn), lambda i, j: (0, 0, j))],
        out_specs=pl.BlockSpec((tm, tn), lambda i, j: (i, j)),
        out_shape=jax.ShapeDtypeStruct((s, d), BF16),
        compiler_params=_cparams(2), name="gated_merge",
    )(g_lat, *outs, w_gate_b, b_gate.reshape(N_BRANCHES, 1, d), w_branch)


def _cross_kernel(x_ref, g_ref, wq_ref, gq_ref, kraw_ref, gk_ref, v_ref, wo_ref, o_ref):
    x = x_ref[...]
    ms = jnp.mean(x * x, axis=-1, keepdims=True)
    h = (x * lax.rsqrt(ms + RMS_EPS) * g_ref[...]).astype(BF16)
    q = jnp.dot(h, wq_ref[...], preferred_element_type=F32)
    outs = []
    for hd in range(MEM_HEADS):
        sl = slice(hd * MEM_HEAD_DIM, (hd + 1) * MEM_HEAD_DIM)
        qh = q[:, sl]
        qh = qh * (lax.rsqrt(jnp.mean(qh * qh, axis=-1, keepdims=True) + RMS_EPS) * MEM_HEAD_DIM ** -0.5) * gq_ref[...]
        kh = kraw_ref[:, sl]
        kh = kh * lax.rsqrt(jnp.mean(kh * kh, axis=-1, keepdims=True) + RMS_EPS) * gk_ref[...]
        s = lax.dot_general(qh.astype(BF16), kh.astype(BF16), NT_DIMS, preferred_element_type=F32)
        p = jnp.exp(s - jnp.max(s, axis=1, keepdims=True))
        p = p / jnp.sum(p, axis=1, keepdims=True)
        outs.append(jnp.dot(p.astype(BF16), v_ref[:, sl], preferred_element_type=F32).astype(BF16))
    o = jnp.concatenate(outs, axis=1)
    o_ref[...] = x + jnp.dot(o, wo_ref[...], preferred_element_type=F32)


def cross_attention(x, g, w_xq, g_q, k_raw, g_k, v, w_xo):
    s, d = x.shape
    m = k_raw.shape[0]
    tm = _pick(s, (256, 128))
    c2 = lambda shape: pl.BlockSpec(shape, lambda i: (0, 0))
    return pl.pallas_call(
        _cross_kernel, grid=(s // tm,),
        in_specs=[_row_spec(tm, d), c2((1, d)), c2((d, MEM_WIDTH)), c2((1, MEM_HEAD_DIM)),
                  c2((m, MEM_WIDTH)), c2((1, MEM_HEAD_DIM)), c2((m, MEM_WIDTH)), c2((MEM_WIDTH, d))],
        out_specs=_row_spec(tm, d),
        out_shape=jax.ShapeDtypeStruct((s, d), F32),
        compiler_params=_cparams(1), name="cross_attention",
    )(x, g.reshape(1, d), w_xq, g_q.reshape(1, -1), k_raw, g_k.reshape(1, -1), v, w_xo)


def _split_w_in(w_in):
    cuts = np.cumsum([0, MLA_Q_LORA, MLA_KV_LORA, MLA_ROPE, BRANCH_WIDTH, BRANCH_WIDTH, BRANCH_WIDTH,
                      BRANCH_WIDTH, BRANCH_WIDTH, BRANCH_WIDTH, BRANCH_WIDTH, HEAD_DIM, HEAD_DIM,
                      IDX_HEADS * IDX_DIM, IDX_DIM, IDX_HEADS])
    seg = lambda a, b: w_in[:, cuts[a]:cuts[b]]
    zeros = lambda n: jnp.zeros((w_in.shape[0], n), w_in.dtype)
    w_lat = jnp.concatenate([seg(0, 3), zeros(LANES - MLA_ROPE)], axis=1)
    w_sb = seg(3, 6)
    w_dil = seg(6, 9)
    w_dsa = seg(9, 12)
    w_idx = jnp.concatenate([seg(12, 14), zeros(LANES - IDX_DIM), seg(14, 15), zeros(LANES - IDX_HEADS)], axis=1)
    return [w.astype(BF16) for w in (w_lat, w_sb, w_dil, w_dsa, w_idx)]


def _pad_mla_up(w_uq, w_ukv):
    r = w_uq.shape[0]
    wq = w_uq.reshape(r, N_HEADS, MLA_QK)
    wq = jnp.concatenate([wq, jnp.zeros((r, N_HEADS, MLA_PAD - MLA_QK), wq.dtype)], axis=2)
    wkv = w_ukv.reshape(w_ukv.shape[0], N_HEADS, 2 * HEAD_DIM)
    wkv = jnp.concatenate([wkv[:, :, :MLA_NOPE].reshape(-1, BRANCH_WIDTH),
                           wkv[:, :, MLA_NOPE:].reshape(-1, BRANCH_WIDTH)], axis=1)
    return wq.reshape(r, N_HEADS * MLA_PAD).astype(BF16), wkv.astype(BF16)


def _branches(h, p, tabs):
    tab_p, tab_i, tab_m = tabs
    w_lat, w_sb, w_dil, w_dsa, w_idx = _split_w_in(p['w_in'])
    w_uq, w_ukv = _pad_mla_up(p['w_uq'], p['w_ukv'])

    cq, ckv, kpe = prep_latent(matmul(h, w_lat, out_dtype=F32), p['g_cq'], p['g_ckv'])
    qT, k, vT = prep_mla(matmul(cq, w_uq, out_dtype=F32), matmul(ckv, w_ukv, out_dtype=F32), kpe,
                         p['g_q_mla'], p['g_k_mla'], tab_m)
    o_mla = mla_attention(qT, k, vT)
    o_sb = sb_attention(*prep_sb(matmul(h, w_sb, out_dtype=F32)))
    o_dil = dil_attention(*prep_dil(matmul(h, w_dil, out_dtype=F32), p['g_q_dil'], p['g_k_dil'], tab_p))
    qsT, ks, vsT = prep_dsa(matmul(h, w_dsa, out_dtype=F32), p['g_q_dsa'], p['g_k_dsa'], tab_p)
    qiT, ki, wiT = prep_idx(matmul(h, w_idx, out_dtype=F32), tab_i)
    o_dsa = dsa_attention(qsT, qiT, wiT, ki, ks, vsT)
    return o_mla, o_sb, o_dil, o_dsa


def _token_mixer(x, h, p, tabs):
    o_mla, o_sb, o_dil, o_dsa = _branches(h, p, tabs)
    g_lat = matmul(h, p['w_gate_a'].astype(BF16), out_dtype=BF16)
    merged = gated_merge(g_lat, (o_mla, o_sb, o_dil, o_dsa), p['w_gate_b'].astype(BF16),
                         p['b_gate'].astype(F32), p['w_branch'].astype(BF16))
    return matmul(merged, p['w_out'].astype(BF16), out_dtype=F32, res=x)


def _cross_block(x, mem, p):
    m_n = rmsnorm_rows(mem, p['ln_mem'])
    k_raw = matmul(m_n, p['w_xk'].astype(BF16), out_dtype=F32)
    v = matmul(m_n, p['w_xv'].astype(BF16), out_dtype=BF16)
    return cross_attention(x, p['ln_xattn'], p['w_xq'].astype(BF16), p['g_q_x'], k_raw, p['g_k_x'], v,
                           p['w_xo'].astype(BF16))


def _ffn_block(x, g, wg, wu, wd):
    act = glu_up(rmsnorm_rows(x, g), wg.astype(BF16), wu.astype(BF16))
    f = wd.shape[0]
    tk = max(c for c in range(256, 2049, 256) if f % c == 0)
    return matmul(act, wd.astype(BF16), out_dtype=F32, res=x, tn=1024, tk=tk)


def _moe_block(x, g, w_router, wg, wu, wd):
    sel = router_top2(x, g, w_router)
    src, pos, tile_expert = _moe_plan(sel)
    hs = gather_norm(x, g, src)
    act = _grouped_call(_glu_grouped_kernel, "moe_glu_up", hs, [wg.astype(BF16), wu.astype(BF16)],
                        tile_expert, _pick(wg.shape[2], (256, 128)), BF16)
    y = _grouped_call(_mm_grouped_kernel, "moe_down", act, [wd.astype(BF16)], tile_expert,
                      _pick(wd.shape[2], (1024, 512, 256, 128)), F32)
    return moe_combine(x, sel, y, pos)


def kernel(x, mem, ln_mix, w_in, g_cq, g_ckv, w_uq, w_ukv, g_q_mla, g_k_mla, g_q_dil, g_k_dil, g_q_dsa, g_k_dsa, w_gate_a, w_gate_b, b_gate, w_branch, w_out, ln_xattn, ln_mem, w_xq, w_xk, w_xv, g_q_x, g_k_x, w_xo, ln_ffn, w_ff_gate, w_ff_up, w_ff_down, w_router, w_e_gate, w_e_up, w_e_down):
    b, s, d = x.shape
    per_layer = dict(w_in=w_in, g_cq=g_cq, g_ckv=g_ckv, w_uq=w_uq, w_ukv=w_ukv, g_q_mla=g_q_mla,
                     g_k_mla=g_k_mla, g_q_dil=g_q_dil, g_k_dil=g_k_dil, g_q_dsa=g_q_dsa, g_k_dsa=g_k_dsa,
                     w_gate_a=w_gate_a, w_gate_b=w_gate_b, b_gate=b_gate, w_branch=w_branch, w_out=w_out,
                     ln_xattn=ln_xattn, ln_mem=ln_mem, w_xq=w_xq, w_xk=w_xk, w_xv=w_xv, g_q_x=g_q_x,
                     g_k_x=g_k_x, w_xo=w_xo)
    tabs = (_rope_tables(s, ROT_DIM, HEAD_DIM), _rope_tables(s, IDX_ROT, IDX_DIM),
            _rope_tables(s, MLA_ROPE, HEAD_DIM))
    depth = ln_mix.shape[0]
    outs = []
    for bi in range(b):
        xb, mb = x[bi], mem[bi]
        for i in range(depth):
            p = {name: val[i] for name, val in per_layer.items()}
            xb = _token_mixer(xb, rmsnorm_rows(xb, ln_mix[i]), p, tabs)
            xb = _cross_block(xb, mb, p)
            j = i // 2
            if i % 2 == 0:
                xb = _ffn_block(xb, ln_ffn[i], w_ff_gate[j], w_ff_up[j], w_ff_down[j])
            else:
                xb = _moe_block(xb, ln_ffn[i], w_router[j], w_e_gate[j], w_e_up[j], w_e_down[j])
        outs.append(xb)
    return outs[0][None] if b == 1 else jnp.stack(outs, axis=0)
```

```python
import functools
import math

import numpy as np
import jax
import jax.numpy as jnp
from jax import lax
from jax.experimental import pallas as pl
from jax.experimental.pallas import tpu as pltpu

F32 = jnp.float32
BF16 = jnp.bfloat16
I32 = jnp.int32

N_BRANCHES = 4
HEAD_DIM = 128
N_HEADS = 8
BRANCH_WIDTH = N_HEADS * HEAD_DIM
ROT_DIM = HEAD_DIM // 4
ROPE_THETA = 500000.0
RMS_EPS = 1e-6
NEG = -1e30
GATE_RANK = 256
MLA_Q_LORA = 896
MLA_KV_LORA = 256
MLA_NOPE = 128
MLA_ROPE = 64
MLA_QK = MLA_NOPE + MLA_ROPE
MLA_PAD = 256
DIL_PATTERNS = ((128, 1), (512, 4), (2048, 16))
DSA_TOPK = 256
IDX_HEADS = 8
IDX_DIM = 64
IDX_ROT = IDX_DIM // 4
MEM_HEADS = 4
MEM_HEAD_DIM = 128
MEM_WIDTH = MEM_HEADS * MEM_HEAD_DIM
N_EXPERTS = 8
MOE_TOP_K = 2
INT_MIN = -2 ** 31
LOG2E = math.log2(math.e)

ATT_T = 512
SB_T = 512
MLA_HEAD_GROUP = 2
DIL_HEAD_GROUP = 2
DSA_TQ = 128
COUNT_ROWS = 64
SB_EXIT = -105.0

V7X_VMEM_LIMIT_BYTES = 56 * 1024 * 1024
LANES = 128

NT_DIMS = (((1,), (1,)), ((), ()))


def _cparams(n_axes):
    return pltpu.CompilerParams(dimension_semantics=("arbitrary",) * n_axes,
                                vmem_limit_bytes=V7X_VMEM_LIMIT_BYTES)


def _pick(n, candidates):
    for c in candidates:
        if n % c == 0:
            return c
    return n


def _rmsnorm_kernel(x_ref, g_ref, o_ref):
    x = x_ref[...].astype(F32)
    ms = jnp.mean(x * x, axis=-1, keepdims=True)
    o_ref[...] = (x * lax.rsqrt(ms + RMS_EPS) * g_ref[...]).astype(o_ref.dtype)


def rmsnorm_rows(x, g, out_dtype=BF16):
    m, d = x.shape
    tm = _pick(m, (512, 256, 128))
    return pl.pallas_call(
        _rmsnorm_kernel, grid=(m // tm,),
        in_specs=[pl.BlockSpec((tm, d), lambda i: (i, 0)), pl.BlockSpec((1, d), lambda i: (0, 0))],
        out_specs=pl.BlockSpec((tm, d), lambda i: (i, 0)),
        out_shape=jax.ShapeDtypeStruct((m, d), out_dtype),
        compiler_params=_cparams(1), name="rmsnorm_rows",
    )(x, g.reshape(1, d).astype(F32))


def _mm_kernel(*refs, nk, has_res):
    a_ref, b_ref = refs[0], refs[1]
    r_ref = refs[2] if has_res else None
    o_ref = refs[2 + has_res]
    if nk == 1:
        part = jnp.dot(a_ref[...], b_ref[...], preferred_element_type=F32)
        if has_res:
            part = r_ref[...] + part
        o_ref[...] = part.astype(o_ref.dtype)
        return
    acc_ref = refs[3 + has_res]
    k = pl.program_id(2)

    @pl.when(k == 0)
    def _():
        acc_ref[...] = jnp.zeros(acc_ref.shape, F32)

    acc_ref[...] += jnp.dot(a_ref[...], b_ref[...], preferred_element_type=F32)

    @pl.when(k == nk - 1)
    def _():
        res = acc_ref[...]
        if has_res:
            res = r_ref[...] + res
        o_ref[...] = res.astype(o_ref.dtype)


def matmul(a, b, *, out_dtype, res=None, tm=None, tn=None, tk=None):
    m, kdim = a.shape
    n = b.shape[1]
    tm = tm or _pick(m, (1024, 512, 256, 128))
    tn = tn or _pick(n, (512, 256, 128))
    tk = tk or (kdim if kdim <= 4096 else _pick(kdim, (2048, 1024, 512)))
    nk = kdim // tk
    in_specs = [pl.BlockSpec((tm, tk), lambda i, j, k: (i, k)),
                pl.BlockSpec((tk, tn), lambda i, j, k: (k, j))]
    args = [a, b]
    if res is not None:
        in_specs.append(pl.BlockSpec((tm, tn), lambda i, j, k: (i, j)))
        args.append(res)
    scratch = [pltpu.VMEM((tm, tn), F32)] if nk > 1 else []
    return pl.pallas_call(
        functools.partial(_mm_kernel, nk=nk, has_res=res is not None),
        grid=(m // tm, n // tn, nk),
        in_specs=in_specs,
        out_specs=pl.BlockSpec((tm, tn), lambda i, j, k: (i, j)),
        out_shape=jax.ShapeDtypeStruct((m, n), out_dtype),
        scratch_shapes=scratch,
        compiler_params=_cparams(3), name="matmul",
    )(*args)


def _glu_kernel(a_ref, wg_ref, wu_ref, o_ref):
    a = a_ref[...]
    g = jnp.dot(a, wg_ref[...], preferred_element_type=F32)
    u = jnp.dot(a, wu_ref[...], preferred_element_type=F32)
    o_ref[...] = (g * jax.nn.sigmoid(g) * u).astype(o_ref.dtype)


def glu_up(a, wg, wu):
    m, kdim = a.shape
    f = wg.shape[1]
    tm = _pick(m, (1024, 512, 256, 128))
    tn = _pick(f, (512, 256, 128))
    w_spec = pl.BlockSpec((kdim, tn), lambda i, j: (0, j))
    return pl.pallas_call(
        _glu_kernel, grid=(m // tm, f // tn),
        in_specs=[pl.BlockSpec((tm, kdim), lambda i, j: (i, 0)), w_spec, w_spec],
        out_specs=pl.BlockSpec((tm, tn), lambda i, j: (i, j)),
        out_shape=jax.ShapeDtypeStruct((m, f), BF16),
        compiler_params=_cparams(2), name="glu_up",
    )(a, wg, wu)


MOE_TM = 512


def _glu_grouped_kernel(te_ref, a_ref, wg_ref, wu_ref, o_ref):
    _glu_kernel(a_ref, wg_ref, wu_ref, o_ref)


def _mm_grouped_kernel(te_ref, a_ref, w_ref, o_ref):
    o_ref[...] = jnp.dot(a_ref[...], w_ref[...], preferred_element_type=F32).astype(o_ref.dtype)


def _grouped_call(kernel, name, a, weights, tile_expert, tn, out_dtype):
    r, kdim = a.shape
    n = weights[0].shape[2]
    w_spec = pl.BlockSpec((None, kdim, tn), lambda j, i, te: (te[i], 0, j))
    grid_spec = pltpu.PrefetchScalarGridSpec(
        num_scalar_prefetch=1, grid=(n // tn, r // MOE_TM),
        in_specs=[pl.BlockSpec((MOE_TM, kdim), lambda j, i, te: (i, 0))] + [w_spec] * len(weights),
        out_specs=pl.BlockSpec((MOE_TM, tn), lambda j, i, te: (i, j)))
    return pl.pallas_call(
        kernel, grid_spec=grid_spec, out_shape=jax.ShapeDtypeStruct((r, n), out_dtype),
        compiler_params=_cparams(2), name=name,
    )(tile_expert, a, *weights)


def _row_gather(idx_ref, base, src_hbm, dst, sem, n, *, wait):
    def body(r, c):
        copy = pltpu.make_async_copy(src_hbm.at[pl.ds(idx_ref[base + r], 1)], dst.at[pl.ds(r, 1)], sem)
        if wait:
            copy.wait()
        else:
            copy.start()
        return c

    lax.fori_loop(0, n, body, 0, unroll=8)


def _gather_ahead(gather):
    i, n = pl.program_id(0), pl.num_programs(0)
    slot = i % 2

    @pl.when(i == 0)
    def _():
        gather(0, 0, False)

    @pl.when(i + 1 < n)
    def _():
        gather(i + 1, 1 - slot, False)

    gather(i, slot, True)
    return slot


def _gather_norm_kernel(src_ref, x_hbm, g_ref, o_ref, buf, sems):
    tm = o_ref.shape[0]

    def gather(step, slot, wait):
        _row_gather(src_ref, step * tm, x_hbm, buf.at[slot], sems.at[slot], tm, wait=wait)

    x = buf[_gather_ahead(gather)]
    ms = jnp.mean(x * x, axis=-1, keepdims=True)
    o_ref[...] = (x * lax.rsqrt(ms + RMS_EPS) * g_ref[...]).astype(o_ref.dtype)


def gather_norm(x, g, src):
    d = x.shape[1]
    r = src.shape[0]
    grid_spec = pltpu.PrefetchScalarGridSpec(
        num_scalar_prefetch=1, grid=(r // MOE_TM,),
        in_specs=[pl.BlockSpec(memory_space=pl.ANY), pl.BlockSpec((1, d), lambda i, s: (0, 0))],
        out_specs=pl.BlockSpec((MOE_TM, d), lambda i, s: (i, 0)),
        scratch_shapes=[pltpu.VMEM((2, MOE_TM, d), F32), pltpu.SemaphoreType.DMA((2,))])
    return pl.pallas_call(
        _gather_norm_kernel, grid_spec=grid_spec, out_shape=jax.ShapeDtypeStruct((r, d), BF16),
        compiler_params=_cparams(1), name="gather_norm",
    )(src, x, g.reshape(1, d).astype(F32))


def _moe_combine_kernel(pos_ref, x_ref, sel_ref, y_hbm, o_ref, buf, sems):
    tm = o_ref.shape[0]
    half = pos_ref.shape[0] // MOE_TOP_K

    def gather(step, slot, wait):
        for which in range(MOE_TOP_K):
            _row_gather(pos_ref, which * half + step * tm, y_hbm, buf.at[slot, which], sems.at[slot, which], tm,
                        wait=wait)

    slot = _gather_ahead(gather)
    sel = sel_ref[...]
    o_ref[...] = x_ref[...] + sel[:, 2:3] * buf[slot, 0] + sel[:, 3:4] * buf[slot, 1]


def moe_combine(x, sel, y, pos):
    s, d = x.shape
    tm = _pick(s, (256, 128))
    grid_spec = pltpu.PrefetchScalarGridSpec(
        num_scalar_prefetch=1, grid=(s // tm,),
        in_specs=[pl.BlockSpec((tm, d), lambda i, p: (i, 0)), pl.BlockSpec((tm, LANES), lambda i, p: (i, 0)),
                  pl.BlockSpec(memory_space=pl.ANY)],
        out_specs=pl.BlockSpec((tm, d), lambda i, p: (i, 0)),
        scratch_shapes=[pltpu.VMEM((2, MOE_TOP_K, tm, d), F32), pltpu.SemaphoreType.DMA((2, MOE_TOP_K))])
    return pl.pallas_call(
        _moe_combine_kernel, grid_spec=grid_spec, out_shape=jax.ShapeDtypeStruct((s, d), F32),
        compiler_params=_cparams(1), name="moe_combine",
    )(pos.reshape(-1), x, sel, y)


def _moe_plan(sel):
    s = sel.shape[0]
    e = sel[:, :MOE_TOP_K].astype(I32).T.reshape(-1)
    onehot = (e[:, None] == jnp.arange(N_EXPERTS, dtype=I32)[None, :]).astype(I32)
    rank = jnp.take_along_axis(jnp.cumsum(onehot, axis=0), e[:, None], axis=1)[:, 0] - 1
    padded = (jnp.sum(onehot, axis=0) + MOE_TM - 1) // MOE_TM * MOE_TM
    ends = jnp.cumsum(padded)
    pos = (ends - padded)[e] + rank
    n_rows = MOE_TOP_K * s + N_EXPERTS * MOE_TM
    token = jnp.tile(jnp.arange(s, dtype=I32), MOE_TOP_K)
    src = jnp.zeros((n_rows,), I32).at[pos].set(token)
    tile_start = jnp.arange(n_rows // MOE_TM, dtype=I32) * MOE_TM
    tile_expert = jnp.minimum(jnp.searchsorted(ends, tile_start, side='right'), N_EXPERTS - 1).astype(I32)
    return src, pos.reshape(MOE_TOP_K, s).astype(I32), tile_expert


def _router_kernel(x_ref, g_ref, w_ref, o_ref):
    x = x_ref[...]
    ms = jnp.mean(x * x, axis=-1, keepdims=True)
    h = x * lax.rsqrt(ms + RMS_EPS) * g_ref[...]
    logits = jnp.dot(h, w_ref[...], preferred_element_type=F32, precision=lax.Precision.HIGHEST)
    lane = lax.broadcasted_iota(I32, logits.shape, 1).astype(F32)
    logits = jnp.where(lane < N_EXPERTS, logits, -jnp.inf)
    m1 = jnp.max(logits, axis=1, keepdims=True)
    i1 = jnp.min(jnp.where(logits == m1, lane, float(LANES)), axis=1, keepdims=True)
    rest = jnp.where(lane == i1, -jnp.inf, logits)
    m2 = jnp.max(rest, axis=1, keepdims=True)
    i2 = jnp.min(jnp.where(rest == m2, lane, float(LANES)), axis=1, keepdims=True)
    e2 = jnp.exp(m2 - m1)
    w1 = 1.0 / (1.0 + e2)
    w2 = e2 / (1.0 + e2)
    o_ref[...] = (jnp.where(lane == 0.0, i1, 0.0) + jnp.where(lane == 1.0, i2, 0.0)
                  + jnp.where(lane == 2.0, w1, 0.0) + jnp.where(lane == 3.0, w2, 0.0))


def router_top2(x, g, w_router):
    m, d = x.shape
    tm = _pick(m, (256, 128))
    w = jnp.zeros((d, LANES), F32).at[:, :N_EXPERTS].set(w_router.astype(F32))
    return pl.pallas_call(
        _router_kernel, grid=(m // tm,),
        in_specs=[pl.BlockSpec((tm, d), lambda i: (i, 0)), pl.BlockSpec((1, d), lambda i: (0, 0)),
                  pl.BlockSpec((d, LANES), lambda i: (0, 0))],
        out_specs=pl.BlockSpec((tm, LANES), lambda i: (i, 0)),
        out_shape=jax.ShapeDtypeStruct((m, LANES), F32),
        compiler_params=_cparams(1), name="router_top2",
    )(x, g.reshape(1, d).astype(F32), w)


def _rope(x, tab_ref, half):
    w = x.shape[-1]
    return (x * tab_ref[0] + pltpu.roll(x, w - half, 1) * tab_ref[1]
            + pltpu.roll(x, half, 1) * tab_ref[2])


def _rope_tables(seq, rot_dim, period):
    half = rot_dim // 2
    inv_freq = ROPE_THETA ** (-jnp.arange(0, rot_dim, 2, dtype=F32) / rot_dim)
    ang = jnp.arange(seq, dtype=F32)[:, None] * inv_freq[None, :]
    cos, sin = jnp.cos(ang), jnp.sin(ang)
    ones = jnp.ones((seq, period - rot_dim), F32)
    zeros = jnp.zeros((seq, period - rot_dim), F32)
    zh = jnp.zeros((seq, half), F32)
    c = jnp.concatenate([cos, cos, ones], axis=1)
    sa = jnp.concatenate([-sin, zh, zeros], axis=1)
    sb = jnp.concatenate([zh, sin, zeros], axis=1)
    rep = LANES // period
    return jnp.stack([jnp.tile(c, (1, rep)), jnp.tile(sa, (1, rep)), jnp.tile(sb, (1, rep))], axis=0)


def _tab_spec(tm):
    return pl.BlockSpec((3, tm, LANES), lambda i: (0, i, 0))


def _row_spec(tm, w):
    return pl.BlockSpec((tm, w), lambda i: (i, 0))


def _const_spec(w):
    return pl.BlockSpec((1, w), lambda i: (0, 0))


def _prep_latent_kernel(p_ref, gq_ref, gkv_ref, cq_ref, ckv_ref, kpe_ref):
    cq = p_ref[:, :MLA_Q_LORA]
    ms = jnp.mean(cq * cq, axis=-1, keepdims=True)
    cq_ref[...] = (cq * lax.rsqrt(ms + RMS_EPS) * gq_ref[...]).astype(cq_ref.dtype)
    ckv = p_ref[:, MLA_Q_LORA:MLA_Q_LORA + MLA_KV_LORA]
    ms = jnp.mean(ckv * ckv, axis=-1, keepdims=True)
    ckv_ref[...] = (ckv * lax.rsqrt(ms + RMS_EPS) * gkv_ref[...]).astype(ckv_ref.dtype)
    kpe_ref[...] = p_ref[:, MLA_Q_LORA + MLA_KV_LORA:]


def prep_latent(p, g_cq, g_ckv):
    s = p.shape[0]
    tm = _pick(s, (512, 256, 128))
    return pl.pallas_call(
        _prep_latent_kernel, grid=(s // tm,),
        in_specs=[_row_spec(tm, p.shape[1]), _const_spec(MLA_Q_LORA), _const_spec(MLA_KV_LORA)],
        out_specs=[_row_spec(tm, MLA_Q_LORA), _row_spec(tm, MLA_KV_LORA), _row_spec(tm, LANES)],
        out_shape=[jax.ShapeDtypeStruct((s, MLA_Q_LORA), BF16),
                   jax.ShapeDtypeStruct((s, MLA_KV_LORA), BF16),
                   jax.ShapeDtypeStruct((s, LANES), F32)],
        compiler_params=_cparams(1), name="prep_latent",
    )(p, g_cq.reshape(1, -1), g_ckv.reshape(1, -1))


def _prep_mla_kernel(qup_ref, kvup_ref, kpe_ref, gq_ref, gk_ref, tab_ref, qT_ref, k_ref, vT_ref):
    half = MLA_ROPE // 2
    gqn, gqr = gq_ref[:, :LANES], gq_ref[:, LANES:]
    gkn, gkr = gk_ref[:, :LANES], gk_ref[:, LANES:]
    kpe = kpe_ref[...]
    kpe_ss = jnp.sum(kpe * kpe, axis=-1, keepdims=True)
    kr_base = _rope(kpe * gkr, tab_ref, half)
    scale = LOG2E * MLA_QK ** -0.5
    for h in range(N_HEADS):
        qn = qup_ref[:, h * MLA_PAD:h * MLA_PAD + LANES]
        qr = qup_ref[:, h * MLA_PAD + LANES:(h + 1) * MLA_PAD]
        ss = jnp.sum(qn * qn, axis=-1, keepdims=True) + jnp.sum(qr * qr, axis=-1, keepdims=True)
        r = lax.rsqrt(ss * (1.0 / MLA_QK) + RMS_EPS) * scale
        qT_ref[h, :LANES, :] = (qn * r * gqn).T.astype(BF16)
        qT_ref[h, LANES:, :] = _rope(qr * r * gqr, tab_ref, half).T.astype(BF16)
        kn = kvup_ref[:, h * LANES:(h + 1) * LANES]
        ss = jnp.sum(kn * kn, axis=-1, keepdims=True) + kpe_ss
        r = lax.rsqrt(ss * (1.0 / MLA_QK) + RMS_EPS)
        k_ref[:, h * MLA_PAD:h * MLA_PAD + LANES] = (kn * r * gkn).astype(BF16)
        k_ref[:, h * MLA_PAD + LANES:(h + 1) * MLA_PAD] = (kr_base * r).astype(BF16)
        vT_ref[h] = _vT_bf16(kvup_ref[:, BRANCH_WIDTH + h * LANES:BRANCH_WIDTH + (h + 1) * LANES])


def _qT_spec(d, t):
    return pl.BlockSpec((N_HEADS, d, t), lambda i: (0, 0, i))


def _vT_spec(t):
    return pl.BlockSpec((N_HEADS, None, HEAD_DIM, t), lambda i: (0, i, 0, 0))


def _qkv_shapes(s, dq, wk, t):
    return [jax.ShapeDtypeStruct((N_HEADS, dq, s), BF16), jax.ShapeDtypeStruct((s, wk), BF16),
            jax.ShapeDtypeStruct((N_HEADS, s // t, HEAD_DIM, t), BF16)]


def _vT_bf16(v):
    return v.T.astype(BF16)


def prep_mla(qup, kvup, kpe, g_q, g_k, tab):
    s = qup.shape[0]
    t = ATT_T
    pad = lambda g: jnp.zeros((1, MLA_PAD), F32).at[0, :MLA_QK].set(g)
    wq = N_HEADS * MLA_PAD
    return pl.pallas_call(
        _prep_mla_kernel, grid=(s // t,),
        in_specs=[_row_spec(t, wq), _row_spec(t, 2 * BRANCH_WIDTH), _row_spec(t, LANES),
                  _const_spec(MLA_PAD), _const_spec(MLA_PAD), _tab_spec(t)],
        out_specs=[_qT_spec(MLA_PAD, t), _row_spec(t, wq), _vT_spec(t)],
        out_shape=_qkv_shapes(s, MLA_PAD, wq, t),
        compiler_params=_cparams(1), name="prep_mla",
    )(qup, kvup, kpe, pad(g_q), pad(g_k), tab)


def _head_norm_rope(x, g, tab_ref, scale):
    ms = jnp.mean(x * x, axis=-1, keepdims=True)
    y = x * (lax.rsqrt(ms + RMS_EPS) * scale) * g
    return _rope(y, tab_ref, ROT_DIM // 2)


def _prep_sb_kernel(p_ref, qT_ref, k_ref, vT_ref, kn_ref):
    for h in range(N_HEADS):
        sl = slice(h * LANES, (h + 1) * LANES)
        qT_ref[h] = (p_ref[:, sl] * HEAD_DIM ** -0.5).T.astype(BF16)
        vT_ref[h] = p_ref[:, 2 * BRANCH_WIDTH + h * LANES:2 * BRANCH_WIDTH + (h + 1) * LANES].T.astype(BF16)
        kb = p_ref[:, BRANCH_WIDTH + h * LANES:BRANCH_WIDTH + (h + 1) * LANES].astype(BF16)
        k_ref[:, sl] = kb
        kf = kb.astype(F32)
        kn_ref[h] = jnp.zeros((8, LANES), F32) + jnp.max(jnp.sum(kf * kf, axis=1, keepdims=True))


def prep_sb(p):
    s = p.shape[0]
    t = SB_T
    return pl.pallas_call(
        _prep_sb_kernel, grid=(s // t,),
        in_specs=[_row_spec(t, 3 * BRANCH_WIDTH)],
        out_specs=[_qT_spec(HEAD_DIM, t), _row_spec(t, BRANCH_WIDTH), _vT_spec(t),
                   pl.BlockSpec((N_HEADS, None, 8, LANES), lambda i: (0, i, 0, 0))],
        out_shape=_qkv_shapes(s, HEAD_DIM, BRANCH_WIDTH, t)
        + [jax.ShapeDtypeStruct((N_HEADS, s // t, 8, LANES), F32)],
        compiler_params=_cparams(1), name="prep_sb",
    )(p)


def _prep_dil_kernel(p_ref, gq_ref, gk_ref, tab_ref, qT_ref, k_ref, vT_ref):
    gq, gk = gq_ref[...], gk_ref[...]
    for h in range(N_HEADS):
        sl = slice(h * LANES, (h + 1) * LANES)
        qT_ref[h] = _head_norm_rope(p_ref[:, sl], gq, tab_ref, LOG2E * HEAD_DIM ** -0.5).T.astype(BF16)
        ksl = slice(BRANCH_WIDTH + h * LANES, BRANCH_WIDTH + (h + 1) * LANES)
        k_ref[:, sl] = _head_norm_rope(p_ref[:, ksl], gk, tab_ref, 1.0).astype(BF16)
        vT_ref[h] = _vT_bf16(p_ref[:, 2 * BRANCH_WIDTH + h * LANES:2 * BRANCH_WIDTH + (h + 1) * LANES])


def prep_dil(p, g_q, g_k, tab):
    s = p.shape[0]
    t = ATT_T
    return pl.pallas_call(
        _prep_dil_kernel, grid=(s // t,),
        in_specs=[_row_spec(t, 3 * BRANCH_WIDTH), _const_spec(LANES), _const_spec(LANES), _tab_spec(t)],
        out_specs=[_qT_spec(HEAD_DIM, t), _row_spec(t, BRANCH_WIDTH), _vT_spec(t)],
        out_shape=_qkv_shapes(s, HEAD_DIM, BRANCH_WIDTH, t),
        compiler_params=_cparams(1), name="prep_dil",
    )(p, g_q.reshape(1, -1), g_k.reshape(1, -1), tab)


def _prep_dsa_kernel(p_ref, gq_ref, gk_ref, tab_ref, qT_ref, k_ref, vT_ref):
    gq, gk = gq_ref[...], gk_ref[...]
    for h in range(N_HEADS):
        sl = slice(h * LANES, (h + 1) * LANES)
        qT = _head_norm_rope(p_ref[:, sl], gq, tab_ref, LOG2E * HEAD_DIM ** -0.5).T.astype(BF16)
        for b in range(p_ref.shape[0] // DSA_TQ):
            qT_ref[b, :, h * DSA_TQ:(h + 1) * DSA_TQ] = qT[:, b * DSA_TQ:(b + 1) * DSA_TQ]
    k_ref[...] = _head_norm_rope(p_ref[:, BRANCH_WIDTH:BRANCH_WIDTH + LANES], gk, tab_ref, 1.0).astype(BF16)
    vT_ref[...] = _vT_bf16(p_ref[:, BRANCH_WIDTH + LANES:])


def prep_dsa(p, g_q, g_k, tab):
    s = p.shape[0]
    t = ATT_T
    nqb = t // DSA_TQ
    return pl.pallas_call(
        _prep_dsa_kernel, grid=(s // t,),
        in_specs=[_row_spec(t, BRANCH_WIDTH + 2 * LANES), _const_spec(LANES), _const_spec(LANES), _tab_spec(t)],
        out_specs=[pl.BlockSpec((nqb, HEAD_DIM, N_HEADS * DSA_TQ), lambda i: (i, 0, 0)), _row_spec(t, LANES),
                   pl.BlockSpec((None, HEAD_DIM, t), lambda i: (i, 0, 0))],
        out_shape=[jax.ShapeDtypeStruct((s // DSA_TQ, HEAD_DIM, N_HEADS * DSA_TQ), BF16),
                   jax.ShapeDtypeStruct((s, LANES), BF16),
                   jax.ShapeDtypeStruct((s // t, HEAD_DIM, t), BF16)],
        compiler_params=_cparams(1), name="prep_dsa",
    )(p, g_q.reshape(1, -1), g_k.reshape(1, -1), tab)


def _prep_idx_kernel(p_ref, tab_ref, qiT_ref, ki_ref, wiT_ref):
    half = IDX_ROT // 2
    t = p_ref.shape[0]
    nqb = t // DSA_TQ
    lane = lax.broadcasted_iota(I32, (t, LANES), 1)
    first = lane < IDX_DIM
    zero = jnp.zeros((t, LANES), F32)
    for b in range(IDX_HEADS // 2):
        x = _rope(p_ref[:, b * LANES:(b + 1) * LANES], tab_ref, half) * (IDX_DIM ** -0.5)
        hi = x.astype(BF16).astype(F32)
        lo = x - hi
        rhi = pltpu.roll(hi, IDX_DIM, 1)
        for hh, (a0, a1) in enumerate(((jnp.where(first, hi, pltpu.roll(lo, IDX_DIM, 1)), jnp.where(first, hi, zero)),
                                       (jnp.where(first, rhi, lo), jnp.where(first, rhi, zero)))):
            h = 2 * b + hh
            a0T, a1T = a0.T.astype(BF16), a1.T.astype(BF16)
            for qb in range(nqb):
                qs = slice(qb * DSA_TQ, (qb + 1) * DSA_TQ)
                qiT_ref[qb, :LANES, h * DSA_TQ:(h + 1) * DSA_TQ] = a0T[:, qs]
                qiT_ref[qb, LANES:, h * DSA_TQ:(h + 1) * DSA_TQ] = a1T[:, qs]
    kx = _rope(p_ref[:, IDX_HEADS * IDX_DIM:IDX_HEADS * IDX_DIM + LANES], tab_ref, half)
    hi = kx.astype(BF16).astype(F32)
    lo = kx - hi
    ki_ref[:, :LANES] = jnp.where(first, hi, pltpu.roll(hi, IDX_DIM, 1)).astype(BF16)
    ki_ref[:, LANES:] = jnp.where(first, lo, zero).astype(BF16)
    wT = (p_ref[:, IDX_HEADS * IDX_DIM + LANES:] * (IDX_HEADS ** -0.5)).T
    for qb in range(nqb):
        wiT_ref[qb] = wT[:IDX_HEADS, qb * DSA_TQ:(qb + 1) * DSA_TQ]


def prep_idx(p, tab):
    s = p.shape[0]
    t = ATT_T
    nqb = t // DSA_TQ
    return pl.pallas_call(
        _prep_idx_kernel, grid=(s // t,),
        in_specs=[_row_spec(t, IDX_HEADS * IDX_DIM + 2 * LANES), _tab_spec(t)],
        out_specs=[pl.BlockSpec((nqb, 2 * LANES, IDX_HEADS * DSA_TQ), lambda i: (i, 0, 0)),
                   _row_spec(t, 2 * LANES),
                   pl.BlockSpec((nqb, IDX_HEADS, DSA_TQ), lambda i: (i, 0, 0))],
        out_shape=[jax.ShapeDtypeStruct((s // DSA_TQ, 2 * LANES, IDX_HEADS * DSA_TQ), BF16),
                   jax.ShapeDtypeStruct((s, 2 * LANES), BF16),
                   jax.ShapeDtypeStruct((s // DSA_TQ, IDX_HEADS, DSA_TQ), F32)],
        compiler_params=_cparams(1), name="prep_idx",
    )(p, tab)


def _softmax_step(sT, vT, m_sc, acc_sc):
    l_sc, o_sc = acc_sc
    m_prev = m_sc[...]
    m_new = jnp.maximum(m_prev, jnp.max(sT, axis=0, keepdims=True))
    alpha = jnp.exp2(m_prev - m_new)
    p = jnp.exp2(sT - m_new)
    l_sc[...] = alpha * l_sc[...] + jnp.sum(p, axis=0, keepdims=True)
    o_sc[...] = alpha * o_sc[...] + jnp.dot(vT, p.astype(BF16), preferred_element_type=F32)
    m_sc[...] = m_new


def _softmax_init(m_sc, acc_sc):
    m_sc[...] = jnp.full(m_sc.shape, NEG, F32)
    for ref in acc_sc:
        ref[...] = jnp.zeros(ref.shape, F32)


def _softmax_result(acc_sc):
    l_sc, o_sc = acc_sc
    return o_sc[...] / l_sc[...]


def _pipelined_blocks(n_full, scores, consume, consume_last, s_sc):
    s_sc[0] = scores(0)

    def pair(jj, c):
        s_sc[1] = scores(2 * jj + 1)
        consume(s_sc[0], 2 * jj)
        s_sc[0] = scores(2 * jj + 2)
        consume(s_sc[1], 2 * jj + 1)
        return c

    lax.fori_loop(0, n_full // 2, pair, 0)

    @pl.when(n_full % 2 == 1)
    def _():
        s_sc[1] = scores(n_full)
        consume(s_sc[0], n_full - 1)
        consume_last(s_sc[1], n_full)

    @pl.when(n_full % 2 == 0)
    def _():
        consume_last(s_sc[0], n_full)


def _key_le_query(tk, tq):
    return lax.broadcasted_iota(I32, (tk, tq), 0) <= lax.broadcasted_iota(I32, (tk, tq), 1)


def _head_group_attention(qT_ref, k_ref, vT_ref, o_ref, s_sc, m_sc, l_sc, o_sc, *,
                          n_full, block_of, bias_of, causal_last):
    g, _, t = qT_ref.shape
    w = k_ref.shape[2] // g
    accs = [(l_sc.at[h], o_sc.at[h]) for h in range(g)]
    for h in range(g):
        _softmax_init(m_sc.at[h], accs[h])

    def scores(d):
        kb = k_ref[block_of(d)]
        return jnp.stack([jnp.dot(kb[:, h * w:(h + 1) * w], qT_ref[h], preferred_element_type=F32)
                          for h in range(g)])

    def step(sT, d, causal=False):
        bias = None if bias_of is None else bias_of(d)
        for h in range(g):
            s_h = sT[h] if bias is None else sT[h] + bias
            if causal:
                s_h = jnp.where(_key_le_query(t, t), s_h, NEG)
            _softmax_step(s_h, vT_ref[h, block_of(d)], m_sc.at[h], accs[h])

    _pipelined_blocks(n_full, scores, step, functools.partial(step, causal=causal_last), s_sc)
    for h in range(g):
        o_ref[:, h * HEAD_DIM:(h + 1) * HEAD_DIM] = _softmax_result(accs[h]).T.astype(o_ref.dtype)


def _head_group_call(kernel, name, g, qT, k, vT, extra=()):
    n_heads, dq, s = qT.shape
    t = ATT_T
    nb = s // t
    once = pl.Buffered(1)
    whole = lambda a: pl.BlockSpec(a.shape, lambda h, i: (0,) * a.ndim, pipeline_mode=once)
    return pl.pallas_call(
        kernel, grid=(n_heads // g, nb),
        in_specs=[pl.BlockSpec((g, dq, t), lambda h, i: (h, 0, i)),
                  pl.BlockSpec((nb, t, g * dq), lambda h, i: (0, 0, h), pipeline_mode=once),
                  pl.BlockSpec((g, nb, HEAD_DIM, t), lambda h, i: (h, 0, 0, 0), pipeline_mode=once)]
        + [whole(a) for a in extra],
        out_specs=pl.BlockSpec((t, g * HEAD_DIM), lambda h, i: (i, h)),
        out_shape=jax.ShapeDtypeStruct((s, n_heads * HEAD_DIM), BF16),
        scratch_shapes=[pltpu.VMEM((2, g, t, t), F32), pltpu.VMEM((g, 1, t), F32), pltpu.VMEM((g, 1, t), F32),
                        pltpu.VMEM((g, HEAD_DIM, t), F32)],
        compiler_params=_cparams(2), name=name,
    )(qT, k.reshape(nb, t, -1), vT, *extra)


def _mla_kernel(qT_ref, k_ref, vT_ref, o_ref, *scratch):
    _head_group_attention(qT_ref, k_ref, vT_ref, o_ref, *scratch, n_full=pl.program_id(1),
                          block_of=lambda j: j, bias_of=None, causal_last=True)


def _attn_scratch(tk, tq):
    return [pltpu.VMEM((2, tk, tq), F32), pltpu.VMEM((1, tq), F32), pltpu.VMEM((1, tq), F32),
            pltpu.VMEM((HEAD_DIM, tq), F32)]


def _head_qT_spec(d, t):
    return pl.BlockSpec((None, d, t), lambda h, i: (h, 0, i))


def _head_k_spec(nb, t, w):
    return pl.BlockSpec((nb, t, w), lambda h, i: (0, 0, h))


def _head_vT_spec(nb, t):
    return pl.BlockSpec((None, nb, HEAD_DIM, t), lambda h, i: (h, 0, 0, 0))


def _head_out_spec(t):
    return pl.BlockSpec((t, HEAD_DIM), lambda h, i: (i, h))


def mla_attention(qT, k, vT):
    return _head_group_call(_mla_kernel, "mla_attention", MLA_HEAD_GROUP, qT, k, vT)


SB_SUB = 128


def _sb_kernel(qT_ref, k_ref, vT_ref, u_ref, kn_ref, o_ref, carry_sc, acc_sc):
    i = pl.program_id(1)
    t = qT_ref.shape[1]
    qT = qT_ref[...]
    u2 = u_ref[...]
    qf = qT.astype(F32)
    zbound = jnp.sqrt(jnp.sum(qf * qf, axis=0, keepdims=True) * jnp.max(kn_ref[...]))
    carry_sc[...] = jnp.zeros(carry_sc.shape, F32)
    acc_sc[...] = jnp.zeros(acc_sc.shape, F32)
    key = lax.broadcasted_iota(I32, (SB_SUB, t), 0)
    qry = lax.broadcasted_iota(I32, (SB_SUB, t), 1)

    def block(j, diag):
        zT = jnp.dot(k_ref[j], qT, preferred_element_type=F32)
        carry = carry_sc[...]
        parts = [None] * (t // SB_SUB)
        for c in reversed(range(t // SB_SUB)):
            first = c * SB_SUB if diag else 0
            zc = zT[c * SB_SUB:(c + 1) * SB_SUB, first:]
            ls = jnp.minimum(-zc, 0.0) - jnp.log(1.0 + jnp.exp(-jnp.abs(zc)))
            if diag:
                past = (key + c * SB_SUB < qry)[:, first:]
                ls = jnp.where(past, ls, 0.0)
            hi = ls.astype(BF16)
            lo = (ls - hi.astype(F32)).astype(BF16)
            rev = jnp.dot(u2, jnp.concatenate([hi, lo], axis=0), preferred_element_type=F32) + carry[:, first:]
            a = jnp.exp(jnp.minimum(zc + rev, 0.0))
            if diag:
                a = jnp.where(past, a, 0.0)
            a = a.astype(BF16)
            parts[c] = a if first == 0 else jnp.concatenate([jnp.zeros((SB_SUB, first), BF16), a], axis=1)
            carry = rev[0:1, :] if first == 0 else jnp.concatenate([carry[:, :first], rev[0:1, :]], axis=1)
        carry_sc[...] = carry
        acc_sc[...] += jnp.dot(vT_ref[j], jnp.concatenate(parts, axis=0), preferred_element_type=F32)

    def all_underflow():
        return (jnp.max(carry_sc[...] + zbound) < SB_EXIT).astype(I32)

    block(i, True)

    def back(state):
        jj, _ = state
        block(i - 1 - jj, False)
        return jj + 1, all_underflow()

    lax.while_loop(lambda st: jnp.logical_and(st[0] < i, st[1] == 0), back, (jnp.int32(0), all_underflow()))
    o_ref[...] = acc_sc[...].T.astype(o_ref.dtype)


def sb_attention(qT, k, vT, kn):
    s = k.shape[0]
    t = SB_T
    nb = s // t
    tri = (np.arange(SB_SUB)[None, :] >= np.arange(SB_SUB)[:, None]).astype(np.float32)
    u2 = jnp.asarray(np.concatenate([tri, tri], axis=1), dtype=BF16)
    return pl.pallas_call(
        _sb_kernel, grid=(N_HEADS, nb),
        in_specs=[_head_qT_spec(HEAD_DIM, t), _head_k_spec(nb, t, HEAD_DIM), _head_vT_spec(nb, t),
                  pl.BlockSpec((SB_SUB, 2 * SB_SUB), lambda h, i: (0, 0)),
                  pl.BlockSpec((None, nb, 8, LANES), lambda h, i: (h, 0, 0, 0))],
        out_specs=_head_out_spec(t),
        out_shape=jax.ShapeDtypeStruct((s, BRANCH_WIDTH), BF16),
        scratch_shapes=[pltpu.VMEM((1, t), F32), pltpu.VMEM((HEAD_DIM, t), F32)],
        compiler_params=_cparams(2), name="sb_attention",
    )(qT, k.reshape(nb, t, -1), vT, u2, kn)


def _dil_log_weights(t):
    span = max(w for w, _ in DIL_PATTERNS)
    nback = -(-span // t)
    d = np.arange(nback + 1)[:, None, None] * t + np.arange(t)[None, None, :] - np.arange(t)[None, :, None]
    mult = np.zeros(d.shape, np.float64)
    for window, dil in DIL_PATTERNS:
        mult += ((d >= 0) & (d <= window) & (d % dil == 0))
    return np.where(mult > 0, np.log2(np.maximum(mult, 1.0)), NEG).astype(np.float32)


def _dil_kernel(qT_ref, k_ref, vT_ref, w_ref, o_ref, *scratch):
    i = pl.program_id(1)
    nback = w_ref.shape[0] - 1
    _head_group_attention(qT_ref, k_ref, vT_ref, o_ref, *scratch, n_full=jnp.minimum(i, nback),
                          block_of=lambda d: i - d, bias_of=lambda d: w_ref[d], causal_last=False)


def dil_attention(qT, k, vT):
    return _head_group_call(_dil_kernel, "dil_attention", DIL_HEAD_GROUP, qT, k, vT,
                            extra=(jnp.asarray(_dil_log_weights(ATT_T)),))


def _dsa_kernel(qsT_ref, qiT_ref, wiT_ref, ki_ref, k_ref, vT_ref, o_ref,
                key_sc, q1_sc, s_sc, m_sc, l_sc, o_sc, *, top_k, pos_bits):
    acc_sc = (l_sc, o_sc)
    i = pl.program_id(0)
    tq, tk = DSA_TQ, ATT_T
    nkb = (i * tq + tq + tk - 1) // tk
    qiT = qiT_ref[...]
    wiT = wiT_ref[...]
    row = lax.broadcasted_iota(I32, (tk, tq), 1) + i * tq
    col = lax.broadcasted_iota(I32, (tk, tq), 0)

    def idx_block(aT, j):
        idx = jnp.zeros((tk, tq), F32)
        for h in range(IDX_HEADS):
            idx = idx + jnp.maximum(aT[:, h * tq:(h + 1) * tq], 0.0) * wiT[h:h + 1, :]
        bits = pltpu.bitcast(idx + 0.0, I32)
        key = bits ^ ((bits >> 31) & 0x7FFFFFFF)
        key_sc[j] = jnp.where(col + j * tk <= row, key, INT_MIN)

    _pipelined_blocks(nkb - 1, lambda j: jnp.dot(ki_ref[j], qiT, preferred_element_type=F32),
                      idx_block, idx_block, s_sc)

    @pl.when(nkb % 2 == 1)
    def _():
        key_sc[nkb] = jnp.full((tk, tq), INT_MIN, I32)

    def count(hits):
        def part(j):
            return jnp.sum(hits(key_sc[j], j).reshape(tk // COUNT_ROWS, COUNT_ROWS, tq), axis=0)

        def body(jj, acc):
            return acc + part(2 * jj) + part(2 * jj + 1)

        acc = lax.fori_loop(0, (nkb + 1) // 2, body, jnp.zeros((COUNT_ROWS, tq), F32))
        return jnp.sum(acc, axis=0, keepdims=True)

    def bit_body(b, carry):
        thr, cge = carry
        cand = thr + jnp.left_shift(jnp.int32(1), 31 - b)
        c = count(lambda kb, j: jnp.where(kb >= cand, 1.0, 0.0))
        ok = c >= top_k
        return jnp.where(ok, cand, thr), jnp.where(ok, c, cge)

    thr0 = jnp.full((1, tq), INT_MIN, I32)
    cge0 = jnp.zeros((1, tq), F32) + (nkb * tk).astype(F32)
    thr, cge = lax.fori_loop(0, 32, bit_body, (thr0, cge0))

    q1_sc[...] = jnp.full((1, tq), 2 ** 30, I32)

    @pl.when(jnp.max(cge) > top_k)
    def _():
        need = top_k - count(lambda kb, j: jnp.where(kb > thr, 1.0, 0.0))

        def pos_body(b, qpos):
            cand = qpos + jnp.left_shift(jnp.int32(1), pos_bits - 1 - b)
            g = count(lambda kb, j: jnp.where(kb == thr, jnp.where(col + j * tk < cand, 1.0, 0.0), 0.0))
            return jnp.where(g < need, cand, qpos)

        q1_sc[...] = lax.fori_loop(0, pos_bits, pos_body, jnp.zeros((1, tq), I32)) + 1

    q1 = q1_sc[...]
    qsT = qsT_ref[...]
    _softmax_init(m_sc, acc_sc)
    def att_block(sT, j):
        kb = key_sc[j]
        pos = col + j * tk
        bias = jnp.where(kb > thr, 0.0, jnp.where(kb == thr, jnp.where(pos < q1, 0.0, NEG), NEG))
        bias = jnp.where(pos <= row, bias, NEG)
        _softmax_step(sT + jnp.concatenate([bias] * N_HEADS, axis=1), vT_ref[j], m_sc, acc_sc)

    _pipelined_blocks(nkb - 1, lambda j: jnp.dot(k_ref[j], qsT, preferred_element_type=F32),
                      att_block, att_block, s_sc)
    o = _softmax_result(acc_sc)
    for h in range(N_HEADS):
        sl = slice(h * HEAD_DIM, (h + 1) * HEAD_DIM)
        o_ref[:, sl] = o[:, h * tq:(h + 1) * tq].T.astype(o_ref.dtype)


def dsa_attention(qsT, qiT, wiT, ki, k, vT):
    s = k.shape[0]
    tq, tk = DSA_TQ, ATT_T
    nkb = s // tk
    top_k = min(DSA_TOPK, s // 4)
    wq = N_HEADS * tq
    full3 = lambda a, b: pl.BlockSpec((nkb, a, b), lambda i: (0, 0, 0))
    return pl.pallas_call(
        functools.partial(_dsa_kernel, top_k=top_k, pos_bits=(s - 1).bit_length()),
        grid=(s // tq,),
        in_specs=[pl.BlockSpec((None, HEAD_DIM, wq), lambda i: (i, 0, 0)),
                  pl.BlockSpec((None, 2 * LANES, wq), lambda i: (i, 0, 0)),
                  pl.BlockSpec((None, IDX_HEADS, tq), lambda i: (i, 0, 0)),
                  full3(tk, 2 * LANES), full3(tk, HEAD_DIM), full3(HEAD_DIM, tk)],
        out_specs=pl.BlockSpec((tq, BRANCH_WIDTH), lambda i: (i, 0)),
        out_shape=jax.ShapeDtypeStruct((s, BRANCH_WIDTH), BF16),
        scratch_shapes=[pltpu.VMEM((nkb + nkb % 2, tk, tq), I32), pltpu.VMEM((1, tq), I32)]
        + _attn_scratch(tk, wq),
        compiler_params=_cparams(1), name="dsa_attention",
    )(qsT, qiT, wiT, ki.reshape(nkb, tk, -1), k.reshape(nkb, tk, -1), vT)


def _merge_kernel(g_ref, o0_ref, o1_ref, o2_ref, o3_ref, wgb_ref, b_ref, wbr_ref, out_ref):
    g = g_ref[...]
    acc = None
    for n, o_ref in enumerate((o0_ref, o1_ref, o2_ref, o3_ref)):
        gate = jax.nn.sigmoid(jnp.dot(g, wgb_ref[n], preferred_element_type=F32) + b_ref[n])
        val = gate * jnp.dot(o_ref[...], wbr_ref[n], preferred_element_type=F32)
        acc = val if acc is None else acc + val
    out_ref[...] = acc.astype(out_ref.dtype)


def gated_merge(g_lat, outs, w_gate_b, b_gate, w_branch):
    s = g_lat.shape[0]
    d = w_gate_b.shape[-1]
    tm = _pick(s, (1024, 512, 256, 128))
    tn = _pick(d, (512, 256, 128))
    o_spec = pl.BlockSpec((tm, BRANCH_WIDTH), lambda i, j: (i, 0))
    return pl.pallas_call(
        _merge_kernel, grid=(s // tm, d // tn),
        in_specs=[pl.BlockSpec((tm, GATE_RANK), lambda i, j: (i, 0)), o_spec, o_spec, o_spec, o_spec,
                  pl.BlockSpec((N_BRANCHES, GATE_RANK, tn), lambda i, j: (0, 0, j)),
                  pl.BlockSpec((N_BRANCHES, 1, tn), lambda i, j: (0, 0, j)),
                  pl.BlockSpec((N_BRANCHES, BRANCH_WIDTH, tn), lambda i, j: (0, 0, j))],
        out_specs=pl.BlockSpec((tm, tn), lambda i, j: (i, j)),
        out_shape=jax.ShapeDtypeStruct((s, d), BF16),
        compiler_params=_cparams(2), name="gated_merge",
    )(g_lat, *outs, w_gate_b, b_gate.reshape(N_BRANCHES, 1, d), w_branch)


def _cross_kernel(x_ref, g_ref, wq_ref, gq_ref, kraw_ref, gk_ref, v_ref, wo_ref, o_ref):
    x = x_ref[...]
    ms = jnp.mean(x * x, axis=-1, keepdims=True)
    h = (x * lax.rsqrt(ms + RMS_EPS) * g_ref[...]).astype(BF16)
    q = jnp.dot(h, wq_ref[...], preferred_element_type=F32)
    outs = []
    for hd in range(MEM_HEADS):
        sl = slice(hd * MEM_HEAD_DIM, (hd + 1) * MEM_HEAD_DIM)
        qh = q[:, sl]
        qh = qh * (lax.rsqrt(jnp.mean(qh * qh, axis=-1, keepdims=True) + RMS_EPS) * MEM_HEAD_DIM ** -0.5) * gq_ref[...]
        kh = kraw_ref[:, sl]
        kh = kh * lax.rsqrt(jnp.mean(kh * kh, axis=-1, keepdims=True) + RMS_EPS) * gk_ref[...]
        s = lax.dot_general(qh.astype(BF16), kh.astype(BF16), NT_DIMS, preferred_element_type=F32)
        p = jnp.exp(s - jnp.max(s, axis=1, keepdims=True))
        p = p / jnp.sum(p, axis=1, keepdims=True)
        outs.append(jnp.dot(p.astype(BF16), v_ref[:, sl], preferred_element_type=F32).astype(BF16))
    o = jnp.concatenate(outs, axis=1)
    o_ref[...] = x + jnp.dot(o, wo_ref[...], preferred_element_type=F32)


def cross_attention(x, g, w_xq, g_q, k_raw, g_k, v, w_xo):
    s, d = x.shape
    m = k_raw.shape[0]
    tm = _pick(s, (256, 128))
    c2 = lambda shape: pl.BlockSpec(shape, lambda i: (0, 0))
    return pl.pallas_call(
        _cross_kernel, grid=(s // tm,),
        in_specs=[_row_spec(tm, d), c2((1, d)), c2((d, MEM_WIDTH)), c2((1, MEM_HEAD_DIM)),
                  c2((m, MEM_WIDTH)), c2((1, MEM_HEAD_DIM)), c2((m, MEM_WIDTH)), c2((MEM_WIDTH, d))],
        out_specs=_row_spec(tm, d),
        out_shape=jax.ShapeDtypeStruct((s, d), F32),
        compiler_params=_cparams(1), name="cross_attention",
    )(x, g.reshape(1, d), w_xq, g_q.reshape(1, -1), k_raw, g_k.reshape(1, -1), v, w_xo)


def _split_w_in(w_in):
    cuts = np.cumsum([0, MLA_Q_LORA, MLA_KV_LORA, MLA_ROPE, BRANCH_WIDTH, BRANCH_WIDTH, BRANCH_WIDTH,
                      BRANCH_WIDTH, BRANCH_WIDTH, BRANCH_WIDTH, BRANCH_WIDTH, HEAD_DIM, HEAD_DIM,
                      IDX_HEADS * IDX_DIM, IDX_DIM, IDX_HEADS])
    seg = lambda a, b: w_in[:, cuts[a]:cuts[b]]
    zeros = lambda n: jnp.zeros((w_in.shape[0], n), w_in.dtype)
    w_lat = jnp.concatenate([seg(0, 3), zeros(LANES - MLA_ROPE)], axis=1)
    w_sb = seg(3, 6)
    w_dil = seg(6, 9)
    w_dsa = seg(9, 12)
    w_idx = jnp.concatenate([seg(12, 14), zeros(LANES - IDX_DIM), seg(14, 15), zeros(LANES - IDX_HEADS)], axis=1)
    return [w.astype(BF16) for w in (w_lat, w_sb, w_dil, w_dsa, w_idx)]


def _pad_mla_up(w_uq, w_ukv):
    r = w_uq.shape[0]
    wq = w_uq.reshape(r, N_HEADS, MLA_QK)
    wq = jnp.concatenate([wq, jnp.zeros((r, N_HEADS, MLA_PAD - MLA_QK), wq.dtype)], axis=2)
    wkv = w_ukv.reshape(w_ukv.shape[0], N_HEADS, 2 * HEAD_DIM)
    wkv = jnp.concatenate([wkv[:, :, :MLA_NOPE].reshape(-1, BRANCH_WIDTH),
                           wkv[:, :, MLA_NOPE:].reshape(-1, BRANCH_WIDTH)], axis=1)
    return wq.reshape(r, N_HEADS * MLA_PAD).astype(BF16), wkv.astype(BF16)


def _branches(h, p, tabs):
    tab_p, tab_i, tab_m = tabs
    w_lat, w_sb, w_dil, w_dsa, w_idx = _split_w_in(p['w_in'])
    w_uq, w_ukv = _pad_mla_up(p['w_uq'], p['w_ukv'])

    cq, ckv, kpe = prep_latent(matmul(h, w_lat, out_dtype=F32), p['g_cq'], p['g_ckv'])
    qT, k, vT = prep_mla(matmul(cq, w_uq, out_dtype=F32), matmul(ckv, w_ukv, out_dtype=F32), kpe,
                         p['g_q_mla'], p['g_k_mla'], tab_m)
    o_mla = mla_attention(qT, k, vT)
    o_sb = sb_attention(*prep_sb(matmul(h, w_sb, out_dtype=F32)))
    o_dil = dil_attention(*prep_dil(matmul(h, w_dil, out_dtype=F32), p['g_q_dil'], p['g_k_dil'], tab_p))
    qsT, ks, vsT = prep_dsa(matmul(h, w_dsa, out_dtype=F32), p['g_q_dsa'], p['g_k_dsa'], tab_p)
    qiT, ki, wiT = prep_idx(matmul(h, w_idx, out_dtype=F32), tab_i)
    o_dsa = dsa_attention(qsT, qiT, wiT, ki, ks, vsT)
    return o_mla, o_sb, o_dil, o_dsa


def _token_mixer(x, h, p, tabs):
    o_mla, o_sb, o_dil, o_dsa = _branches(h, p, tabs)
    g_lat = matmul(h, p['w_gate_a'].astype(BF16), out_dtype=BF16)
    merged = gated_merge(g_lat, (o_mla, o_sb, o_dil, o_dsa), p['w_gate_b'].astype(BF16),
                         p['b_gate'].astype(F32), p['w_branch'].astype(BF16))
    return matmul(merged, p['w_out'].astype(BF16), out_dtype=F32, res=x)


def _cross_block(x, mem, p):
    m_n = rmsnorm_rows(mem, p['ln_mem'])
    k_raw = matmul(m_n, p['w_xk'].astype(BF16), out_dtype=F32)
    v = matmul(m_n, p['w_xv'].astype(BF16), out_dtype=BF16)
    return cross_attention(x, p['ln_xattn'], p['w_xq'].astype(BF16), p['g_q_x'], k_raw, p['g_k_x'], v,
                           p['w_xo'].astype(BF16))


def _ffn_block(x, g, wg, wu, wd):
    act = glu_up(rmsnorm_rows(x, g), wg.astype(BF16), wu.astype(BF16))
    f = wd.shape[0]
    tk = max(c for c in range(256, 2049, 256) if f % c == 0)
    return matmul(act, wd.astype(BF16), out_dtype=F32, res=x, tn=1024, tk=tk)


def _moe_block(x, g, w_router, wg, wu, wd):
    sel = router_top2(x, g, w_router)
    src, pos, tile_expert = _moe_plan(sel)
    hs = gather_norm(x, g, src)
    act = _grouped_call(_glu_grouped_kernel, "moe_glu_up", hs, [wg.astype(BF16), wu.astype(BF16)],
                        tile_expert, _pick(wg.shape[2], (256, 128)), BF16)
    y = _grouped_call(_mm_grouped_kernel, "moe_down", act, [wd.astype(BF16)], tile_expert,
                      _pick(wd.shape[2], (1024, 512, 256, 128)), F32)
    return moe_combine(x, sel, y, pos)


def kernel(x, mem, ln_mix, w_in, g_cq, g_ckv, w_uq, w_ukv, g_q_mla, g_k_mla, g_q_dil, g_k_dil, g_q_dsa, g_k_dsa, w_gate_a, w_gate_b, b_gate, w_branch, w_out, ln_xattn, ln_mem, w_xq, w_xk, w_xv, g_q_x, g_k_x, w_xo, ln_ffn, w_ff_gate, w_ff_up, w_ff_down, w_router, w_e_gate, w_e_up, w_e_down):
    b, s, d = x.shape
    per_layer = dict(w_in=w_in, g_cq=g_cq, g_ckv=g_ckv, w_uq=w_uq, w_ukv=w_ukv, g_q_mla=g_q_mla,
                     g_k_mla=g_k_mla, g_q_dil=g_q_dil, g_k_dil=g_k_dil, g_q_dsa=g_q_dsa, g_k_dsa=g_k_dsa,
                     w_gate_a=w_gate_a, w_gate_b=w_gate_b, b_gate=b_gate, w_branch=w_branch, w_out=w_out,
                     ln_xattn=ln_xattn, ln_mem=ln_mem, w_xq=w_xq, w_xk=w_xk, w_xv=w_xv, g_q_x=g_q_x,
                     g_k_x=g_k_x, w_xo=w_xo)
    tabs = (_rope_tables(s, ROT_DIM, HEAD_DIM), _rope_tables(s, IDX_ROT, IDX_DIM),
            _rope_tables(s, MLA_ROPE, HEAD_DIM))
    depth = ln_mix.shape[0]
    outs = []
    for bi in range(b):
        xb, mb = x[bi], mem[bi]
        for i in range(depth):
            p = {name: val[i] for name, val in per_layer.items()}
            xb = _token_mixer(xb, rmsnorm_rows(xb, ln_mix[i]), p, tabs)
            xb = _cross_block(xb, mb, p)
            j = i // 2
            if i % 2 == 0:
                xb = _ffn_block(xb, ln_ffn[i], w_ff_gate[j], w_ff_up[j], w_ff_down[j])
            else:
                xb = _moe_block(xb, ln_ffn[i], w_router[j], w_e_gate[j], w_e_up[j], w_e_down[j])
        outs.append(xb)
    return outs[0][None] if b == 1 else jnp.stack(outs, axis=0)
```

```python
import functools
import math

import numpy as np
import jax
import jax.numpy as jnp
from jax import lax
from jax.experimental import pallas as pl
from jax.experimental.pallas import tpu as pltpu

F32 = jnp.float32
BF16 = jnp.bfloat16
I32 = jnp.int32

N_BRANCHES = 4
HEAD_DIM = 128
N_HEADS = 8
BRANCH_WIDTH = N_HEADS * HEAD_DIM
ROT_DIM = HEAD_DIM // 4
ROPE_THETA = 500000.0
RMS_EPS = 1e-6
NEG = -1e30
GATE_RANK = 256
MLA_Q_LORA = 896
MLA_KV_LORA = 256
MLA_NOPE = 128
MLA_ROPE = 64
MLA_QK = MLA_NOPE + MLA_ROPE
MLA_PAD = 256
DIL_PATTERNS = ((128, 1), (512, 4), (2048, 16))
DSA_TOPK = 256
IDX_HEADS = 8
IDX_DIM = 64
IDX_ROT = IDX_DIM // 4
MEM_HEADS = 4
MEM_HEAD_DIM = 128
MEM_WIDTH = MEM_HEADS * MEM_HEAD_DIM
N_EXPERTS = 8
MOE_TOP_K = 2
INT_MIN = -2 ** 31
LOG2E = math.log2(math.e)

ATT_T = 512
SB_T = 512
MLA_HEAD_GROUP = 2
DIL_HEAD_GROUP = 2
DSA_TQ = 128
COUNT_ROWS = 64
SB_EXIT = -105.0

V7X_VMEM_LIMIT_BYTES = 56 * 1024 * 1024
LANES = 128

NT_DIMS = (((1,), (1,)), ((), ()))


def _cparams(n_axes):
    return pltpu.CompilerParams(dimension_semantics=("arbitrary",) * n_axes,
                                vmem_limit_bytes=V7X_VMEM_LIMIT_BYTES)


def _pick(n, candidates):
    for c in candidates:
        if n % c == 0:
            return c
    return n


def _rmsnorm_kernel(x_ref, g_ref, o_ref):
    x = x_ref[...].astype(F32)
    ms = jnp.mean(x * x, axis=-1, keepdims=True)
    o_ref[...] = (x * lax.rsqrt(ms + RMS_EPS) * g_ref[...]).astype(o_ref.dtype)


def rmsnorm_rows(x, g, out_dtype=BF16):
    m, d = x.shape
    tm = _pick(m, (512, 256, 128))
    return pl.pallas_call(
        _rmsnorm_kernel, grid=(m // tm,),
        in_specs=[pl.BlockSpec((tm, d), lambda i: (i, 0)), pl.BlockSpec((1, d), lambda i: (0, 0))],
        out_specs=pl.BlockSpec((tm, d), lambda i: (i, 0)),
        out_shape=jax.ShapeDtypeStruct((m, d), out_dtype),
        compiler_params=_cparams(1), name="rmsnorm_rows",
    )(x, g.reshape(1, d).astype(F32))


def _mm_kernel(*refs, nk, has_res):
    a_ref, b_ref = refs[0], refs[1]
    r_ref = refs[2] if has_res else None
    o_ref = refs[2 + has_res]
    if nk == 1:
        part = jnp.dot(a_ref[...], b_ref[...], preferred_element_type=F32)
        if has_res:
            part = r_ref[...] + part
        o_ref[...] = part.astype(o_ref.dtype)
        return
    acc_ref = refs[3 + has_res]
    k = pl.program_id(2)

    @pl.when(k == 0)
    def _():
        acc_ref[...] = jnp.zeros(acc_ref.shape, F32)

    acc_ref[...] += jnp.dot(a_ref[...], b_ref[...], preferred_element_type=F32)

    @pl.when(k == nk - 1)
    def _():
        res = acc_ref[...]
        if has_res:
            res = r_ref[...] + res
        o_ref[...] = res.astype(o_ref.dtype)


def matmul(a, b, *, out_dtype, res=None, tm=None, tn=None, tk=None):
    m, kdim = a.shape
    n = b.shape[1]
    tm = tm or _pick(m, (1024, 512, 256, 128))
    tn = tn or _pick(n, (512, 256, 128))
    tk = tk or (kdim if kdim <= 4096 else _pick(kdim, (2048, 1024, 512)))
    nk = kdim // tk
    in_specs = [pl.BlockSpec((tm, tk), lambda i, j, k: (i, k)),
                pl.BlockSpec((tk, tn), lambda i, j, k: (k, j))]
    args = [a, b]
    if res is not None:
        in_specs.append(pl.BlockSpec((tm, tn), lambda i, j, k: (i, j)))
        args.append(res)
    scratch = [pltpu.VMEM((tm, tn), F32)] if nk > 1 else []
    return pl.pallas_call(
        functools.partial(_mm_kernel, nk=nk, has_res=res is not None),
        grid=(m // tm, n // tn, nk),
        in_specs=in_specs,
        out_specs=pl.BlockSpec((tm, tn), lambda i, j, k: (i, j)),
        out_shape=jax.ShapeDtypeStruct((m, n), out_dtype),
        scratch_shapes=scratch,
        compiler_params=_cparams(3), name="matmul",
    )(*args)


def _glu_kernel(a_ref, wg_ref, wu_ref, o_ref):
    a = a_ref[...]
    g = jnp.dot(a, wg_ref[...], preferred_element_type=F32)
    u = jnp.dot(a, wu_ref[...], preferred_element_type=F32)
    o_ref[...] = (g * jax.nn.sigmoid(g) * u).astype(o_ref.dtype)


def glu_up(a, wg, wu):
    m, kdim = a.shape
    f = wg.shape[1]
    tm = _pick(m, (1024, 512, 256, 128))
    tn = _pick(f, (512, 256, 128))
    w_spec = pl.BlockSpec((kdim, tn), lambda i, j: (0, j))
    return pl.pallas_call(
        _glu_kernel, grid=(m // tm, f // tn),
        in_specs=[pl.BlockSpec((tm, kdim), lambda i, j: (i, 0)), w_spec, w_spec],
        out_specs=pl.BlockSpec((tm, tn), lambda i, j: (i, j)),
        out_shape=jax.ShapeDtypeStruct((m, f), BF16),
        compiler_params=_cparams(2), name="glu_up",
    )(a, wg, wu)


MOE_TM = 512


def _glu_grouped_kernel(te_ref, a_ref, wg_ref, wu_ref, o_ref):
    _glu_kernel(a_ref, wg_ref, wu_ref, o_ref)


def _mm_grouped_kernel(te_ref, a_ref, w_ref, o_ref):
    o_ref[...] = jnp.dot(a_ref[...], w_ref[...], preferred_element_type=F32).astype(o_ref.dtype)


def _grouped_call(kernel, name, a, weights, tile_expert, tn, out_dtype):
    r, kdim = a.shape
    n = weights[0].shape[2]
    w_spec = pl.BlockSpec((None, kdim, tn), lambda j, i, te: (te[i], 0, j))
    grid_spec = pltpu.PrefetchScalarGridSpec(
        num_scalar_prefetch=1, grid=(n // tn, r // MOE_TM),
        in_specs=[pl.BlockSpec((MOE_TM, kdim), lambda j, i, te: (i, 0))] + [w_spec] * len(weights),
        out_specs=pl.BlockSpec((MOE_TM, tn), lambda j, i, te: (i, j)))
    return pl.pallas_call(
        kernel, grid_spec=grid_spec, out_shape=jax.ShapeDtypeStruct((r, n), out_dtype),
        compiler_params=_cparams(2), name=name,
    )(tile_expert, a, *weights)


def _row_gather(idx_ref, base, src_hbm, dst, sem, n, *, wait):
    def body(r, c):
        copy = pltpu.make_async_copy(src_hbm.at[pl.ds(idx_ref[base + r], 1)], dst.at[pl.ds(r, 1)], sem)
        if wait:
            copy.wait()
        else:
            copy.start()
        return c

    lax.fori_loop(0, n, body, 0, unroll=8)


def _gather_ahead(gather):
    i, n = pl.program_id(0), pl.num_programs(0)
    slot = i % 2

    @pl.when(i == 0)
    def _():
        gather(0, 0, False)

    @pl.when(i + 1 < n)
    def _():
        gather(i + 1, 1 - slot, False)

    gather(i, slot, True)
    return slot


def _gather_norm_kernel(src_ref, x_hbm, g_ref, o_ref, buf, sems):
    tm = o_ref.shape[0]

    def gather(step, slot, wait):
        _row_gather(src_ref, step * tm, x_hbm, buf.at[slot], sems.at[slot], tm, wait=wait)

    x = buf[_gather_ahead(gather)]
    ms = jnp.mean(x * x, axis=-1, keepdims=True)
    o_ref[...] = (x * lax.rsqrt(ms + RMS_EPS) * g_ref[...]).astype(o_ref.dtype)


def gather_norm(x, g, src):
    d = x.shape[1]
    r = src.shape[0]
    grid_spec = pltpu.PrefetchScalarGridSpec(
        num_scalar_prefetch=1, grid=(r // MOE_TM,),
        in_specs=[pl.BlockSpec(memory_space=pl.ANY), pl.BlockSpec((1, d), lambda i, s: (0, 0))],
        out_specs=pl.BlockSpec((MOE_TM, d), lambda i, s: (i, 0)),
        scratch_shapes=[pltpu.VMEM((2, MOE_TM, d), F32), pltpu.SemaphoreType.DMA((2,))])
    return pl.pallas_call(
        _gather_norm_kernel, grid_spec=grid_spec, out_shape=jax.ShapeDtypeStruct((r, d), BF16),
        compiler_params=_cparams(1), name="gather_norm",
    )(src, x, g.reshape(1, d).astype(F32))


def _moe_combine_kernel(pos_ref, x_ref, sel_ref, y_hbm, o_ref, buf, sems):
    tm = o_ref.shape[0]
    half = pos_ref.shape[0] // MOE_TOP_K

    def gather(step, slot, wait):
        for which in range(MOE_TOP_K):
            _row_gather(pos_ref, which * half + step * tm, y_hbm, buf.at[slot, which], sems.at[slot, which], tm,
                        wait=wait)

    slot = _gather_ahead(gather)
    sel = sel_ref[...]
    o_ref[...] = x_ref[...] + sel[:, 2:3] * buf[slot, 0] + sel[:, 3:4] * buf[slot, 1]


def moe_combine(x, sel, y, pos):
    s, d = x.shape
    tm = _pick(s, (256, 128))
    grid_spec = pltpu.PrefetchScalarGridSpec(
        num_scalar_prefetch=1, grid=(s // tm,),
        in_specs=[pl.BlockSpec((tm, d), lambda i, p: (i, 0)), pl.BlockSpec((tm, LANES), lambda i, p: (i, 0)),
                  pl.BlockSpec(memory_space=pl.ANY)],
        out_specs=pl.BlockSpec((tm, d), lambda i, p: (i, 0)),
        scratch_shapes=[pltpu.VMEM((2, MOE_TOP_K, tm, d), F32), pltpu.SemaphoreType.DMA((2, MOE_TOP_K))])
    return pl.pallas_call(
        _moe_combine_kernel, grid_spec=grid_spec, out_shape=jax.ShapeDtypeStruct((s, d), F32),
        compiler_params=_cparams(1), name="moe_combine",
    )(pos.reshape(-1), x, sel, y)


def _moe_plan(sel):
    s = sel.shape[0]
    e = sel[:, :MOE_TOP_K].astype(I32).T.reshape(-1)
    onehot = (e[:, None] == jnp.arange(N_EXPERTS, dtype=I32)[None, :]).astype(I32)
    rank = jnp.take_along_axis(jnp.cumsum(onehot, axis=0), e[:, None], axis=1)[:, 0] - 1
    padded = (jnp.sum(onehot, axis=0) + MOE_TM - 1) // MOE_TM * MOE_TM
    ends = jnp.cumsum(padded)
    pos = (ends - padded)[e] + rank
    n_rows = MOE_TOP_K * s + N_EXPERTS * MOE_TM
    token = jnp.tile(jnp.arange(s, dtype=I32), MOE_TOP_K)
    src = jnp.zeros((n_rows,), I32).at[pos].set(token)
    tile_start = jnp.arange(n_rows // MOE_TM, dtype=I32) * MOE_TM
    tile_expert = jnp.minimum(jnp.searchsorted(ends, tile_start, side='right'), N_EXPERTS - 1).astype(I32)
    return src, pos.reshape(MOE_TOP_K, s).astype(I32), tile_expert


def _router_kernel(x_ref, g_ref, w_ref, o_ref):
    x = x_ref[...]
    ms = jnp.mean(x * x, axis=-1, keepdims=True)
    h = x * lax.rsqrt(ms + RMS_EPS) * g_ref[...]
    logits = jnp.dot(h, w_ref[...], preferred_element_type=F32, precision=lax.Precision.HIGHEST)
    lane = lax.broadcasted_iota(I32, logits.shape, 1).astype(F32)
    logits = jnp.where(lane < N_EXPERTS, logits, -jnp.inf)
    m1 = jnp.max(logits, axis=1, keepdims=True)
    i1 = jnp.min(jnp.where(logits == m1, lane, float(LANES)), axis=1, keepdims=True)
    rest = jnp.where(lane == i1, -jnp.inf, logits)
    m2 = jnp.max(rest, axis=1, keepdims=True)
    i2 = jnp.min(jnp.where(rest == m2, lane, float(LANES)), axis=1, keepdims=True)
    e2 = jnp.exp(m2 - m1)
    w1 = 1.0 / (1.0 + e2)
    w2 = e2 / (1.0 + e2)
    o_ref[...] = (jnp.where(lane == 0.0, i1, 0.0) + jnp.where(lane == 1.0, i2, 0.0)
                  + jnp.where(lane == 2.0, w1, 0.0) + jnp.where(lane == 3.0, w2, 0.0))


def router_top2(x, g, w_router):
    m, d = x.shape
    tm = _pick(m, (256, 128))
    w = jnp.zeros((d, LANES), F32).at[:, :N_EXPERTS].set(w_router.astype(F32))
    return pl.pallas_call(
        _router_kernel, grid=(m // tm,),
        in_specs=[pl.BlockSpec((tm, d), lambda i: (i, 0)), pl.BlockSpec((1, d), lambda i: (0, 0)),
                  pl.BlockSpec((d, LANES), lambda i: (0, 0))],
        out_specs=pl.BlockSpec((tm, LANES), lambda i: (i, 0)),
        out_shape=jax.ShapeDtypeStruct((m, LANES), F32),
        compiler_params=_cparams(1), name="router_top2",
    )(x, g.reshape(1, d).astype(F32), w)


def _rope(x, tab_ref, half):
    w = x.shape[-1]
    return (x * tab_ref[0] + pltpu.roll(x, w - half, 1) * tab_ref[1]
            + pltpu.roll(x, half, 1) * tab_ref[2])


def _rope_tables(seq, rot_dim, period):
    half = rot_dim // 2
    inv_freq = ROPE_THETA ** (-jnp.arange(0, rot_dim, 2, dtype=F32) / rot_dim)
    ang = jnp.arange(seq, dtype=F32)[:, None] * inv_freq[None, :]
    cos, sin = jnp.cos(ang), jnp.sin(ang)
    ones = jnp.ones((seq, period - rot_dim), F32)
    zeros = jnp.zeros((seq, period - rot_dim), F32)
    zh = jnp.zeros((seq, half), F32)
    c = jnp.concatenate([cos, cos, ones], axis=1)
    sa = jnp.concatenate([-sin, zh, zeros], axis=1)
    sb = jnp.concatenate([zh, sin, zeros], axis=1)
    rep = LANES // period
    return jnp.stack([jnp.tile(c, (1, rep)), jnp.tile(sa, (1, rep)), jnp.tile(sb, (1, rep))], axis=0)


def _tab_spec(tm):
    return pl.BlockSpec((3, tm, LANES), lambda i: (0, i, 0))


def _row_spec(tm, w):
    return pl.BlockSpec((tm, w), lambda i: (i, 0))


def _const_spec(w):
    return pl.BlockSpec((1, w), lambda i: (0, 0))


def _prep_latent_kernel(p_ref, gq_ref, gkv_ref, cq_ref, ckv_ref, kpe_ref):
    cq = p_ref[:, :MLA_Q_LORA]
    ms = jnp.mean(cq * cq, axis=-1, keepdims=True)
    cq_ref[...] = (cq * lax.rsqrt(ms + RMS_EPS) * gq_ref[...]).astype(cq_ref.dtype)
    ckv = p_ref[:, MLA_Q_LORA:MLA_Q_LORA + MLA_KV_LORA]
    ms = jnp.mean(ckv * ckv, axis=-1, keepdims=True)
    ckv_ref[...] = (ckv * lax.rsqrt(ms + RMS_EPS) * gkv_ref[...]).astype(ckv_ref.dtype)
    kpe_ref[...] = p_ref[:, MLA_Q_LORA + MLA_KV_LORA:]


def prep_latent(p, g_cq, g_ckv):
    s = p.shape[0]
    tm = _pick(s, (512, 256, 128))
    return pl.pallas_call(
        _prep_latent_kernel, grid=(s // tm,),
        in_specs=[_row_spec(tm, p.shape[1]), _const_spec(MLA_Q_LORA), _const_spec(MLA_KV_LORA)],
        out_specs=[_row_spec(tm, MLA_Q_LORA), _row_spec(tm, MLA_KV_LORA), _row_spec(tm, LANES)],
        out_shape=[jax.ShapeDtypeStruct((s, MLA_Q_LORA), BF16),
                   jax.ShapeDtypeStruct((s, MLA_KV_LORA), BF16),
                   jax.ShapeDtypeStruct((s, LANES), F32)],
        compiler_params=_cparams(1), name="prep_latent",
    )(p, g_cq.reshape(1, -1), g_ckv.reshape(1, -1))


def _prep_mla_kernel(qup_ref, kvup_ref, kpe_ref, gq_ref, gk_ref, tab_ref, qT_ref, k_ref, vT_ref):
    half = MLA_ROPE // 2
    gqn, gqr = gq_ref[:, :LANES], gq_ref[:, LANES:]
    gkn, gkr = gk_ref[:, :LANES], gk_ref[:, LANES:]
    kpe = kpe_ref[...]
    kpe_ss = jnp.sum(kpe * kpe, axis=-1, keepdims=True)
    kr_base = _rope(kpe * gkr, tab_ref, half)
    scale = LOG2E * MLA_QK ** -0.5
    for h in range(N_HEADS):
        qn = qup_ref[:, h * MLA_PAD:h * MLA_PAD + LANES]
        qr = qup_ref[:, h * MLA_PAD + LANES:(h + 1) * MLA_PAD]
        ss = jnp.sum(qn * qn, axis=-1, keepdims=True) + jnp.sum(qr * qr, axis=-1, keepdims=True)
        r = lax.rsqrt(ss * (1.0 / MLA_QK) + RMS_EPS) * scale
        qT_ref[h, :LANES, :] = (qn * r * gqn).T.astype(BF16)
        qT_ref[h, LANES:, :] = _rope(qr * r * gqr, tab_ref, half).T.astype(BF16)
        kn = kvup_ref[:, h * LANES:(h + 1) * LANES]
        ss = jnp.sum(kn * kn, axis=-1, keepdims=True) + kpe_ss
        r = lax.rsqrt(ss * (1.0 / MLA_QK) + RMS_EPS)
        k_ref[:, h * MLA_PAD:h * MLA_PAD + LANES] = (kn * r * gkn).astype(BF16)
        k_ref[:, h * MLA_PAD + LANES:(h + 1) * MLA_PAD] = (kr_base * r).astype(BF16)
        vT_ref[h] = _vT_bf16(kvup_ref[:, BRANCH_WIDTH + h * LANES:BRANCH_WIDTH + (h + 1) * LANES])


def _qT_spec(d, t):
    return pl.BlockSpec((N_HEADS, d, t), lambda i: (0, 0, i))


def _vT_spec(t):
    return pl.BlockSpec((N_HEADS, None, HEAD_DIM, t), lambda i: (0, i, 0, 0))


def _qkv_shapes(s, dq, wk, t):
    return [jax.ShapeDtypeStruct((N_HEADS, dq, s), BF16), jax.ShapeDtypeStruct((s, wk), BF16),
            jax.ShapeDtypeStruct((N_HEADS, s // t, HEAD_DIM, t), BF16)]


def _vT_bf16(v):
    return v.T.astype(BF16)


def prep_mla(qup, kvup, kpe, g_q, g_k, tab):
    s = qup.shape[0]
    t = ATT_T
    pad = lambda g: jnp.zeros((1, MLA_PAD), F32).at[0, :MLA_QK].set(g)
    wq = N_HEADS * MLA_PAD
    return pl.pallas_call(
        _prep_mla_kernel, grid=(s // t,),
        in_specs=[_row_spec(t, wq), _row_spec(t, 2 * BRANCH_WIDTH), _row_spec(t, LANES),
                  _const_spec(MLA_PAD), _const_spec(MLA_PAD), _tab_spec(t)],
        out_specs=[_qT_spec(MLA_PAD, t), _row_spec(t, wq), _vT_spec(t)],
        out_shape=_qkv_shapes(s, MLA_PAD, wq, t),
        compiler_params=_cparams(1), name="prep_mla",
    )(qup, kvup, kpe, pad(g_q), pad(g_k), tab)


def _head_norm_rope(x, g, tab_ref, scale):
    ms = jnp.mean(x * x, axis=-1, keepdims=True)
    y = x * (lax.rsqrt(ms + RMS_EPS) * scale) * g
    return _rope(y, tab_ref, ROT_DIM // 2)


def _prep_sb_kernel(p_ref, qT_ref, k_ref, vT_ref, kn_ref):
    for h in range(N_HEADS):
        sl = slice(h * LANES, (h + 1) * LANES)
        qT_ref[h] = (p_ref[:, sl] * HEAD_DIM ** -0.5).T.astype(BF16)
        vT_ref[h] = p_ref[:, 2 * BRANCH_WIDTH + h * LANES:2 * BRANCH_WIDTH + (h + 1) * LANES].T.astype(BF16)
        kb = p_ref[:, BRANCH_WIDTH + h * LANES:BRANCH_WIDTH + (h + 1) * LANES].astype(BF16)
        k_ref[:, sl] = kb
        kf = kb.astype(F32)
        kn_ref[h] = jnp.zeros((8, LANES), F32) + jnp.max(jnp.sum(kf * kf, axis=1, keepdims=True))


def prep_sb(p):
    s = p.shape[0]
    t = SB_T
    return pl.pallas_call(
        _prep_sb_kernel, grid=(s // t,),
        in_specs=[_row_spec(t, 3 * BRANCH_WIDTH)],
        out_specs=[_qT_spec(HEAD_DIM, t), _row_spec(t, BRANCH_WIDTH), _vT_spec(t),
                   pl.BlockSpec((N_HEADS, None, 8, LANES), lambda i: (0, i, 0, 0))],
        out_shape=_qkv_shapes(s, HEAD_DIM, BRANCH_WIDTH, t)
        + [jax.ShapeDtypeStruct((N_HEADS, s // t, 8, LANES), F32)],
        compiler_params=_cparams(1), name="prep_sb",
    )(p)


def _prep_dil_kernel(p_ref, gq_ref, gk_ref, tab_ref, qT_ref, k_ref, vT_ref):
    gq, gk = gq_ref[...], gk_ref[...]
    for h in range(N_HEADS):
        sl = slice(h * LANES, (h + 1) * LANES)
        qT_ref[h] = _head_norm_rope(p_ref[:, sl], gq, tab_ref, LOG2E * HEAD_DIM ** -0.5).T.astype(BF16)
        ksl = slice(BRANCH_WIDTH + h * LANES, BRANCH_WIDTH + (h + 1) * LANES)
        k_ref[:, sl] = _head_norm_rope(p_ref[:, ksl], gk, tab_ref, 1.0).astype(BF16)
        vT_ref[h] = _vT_bf16(p_ref[:, 2 * BRANCH_WIDTH + h * LANES:2 * BRANCH_WIDTH + (h + 1) * LANES])


def prep_dil(p, g_q, g_k, tab):
    s = p.shape[0]
    t = ATT_T
    return pl.pallas_call(
        _prep_dil_kernel, grid=(s // t,),
        in_specs=[_row_spec(t, 3 * BRANCH_WIDTH), _const_spec(LANES), _const_spec(LANES), _tab_spec(t)],
        out_specs=[_qT_spec(HEAD_DIM, t), _row_spec(t, BRANCH_WIDTH), _vT_spec(t)],
        out_shape=_qkv_shapes(s, HEAD_DIM, BRANCH_WIDTH, t),
        compiler_params=_cparams(1), name="prep_dil",
    )(p, g_q.reshape(1, -1), g_k.reshape(1, -1), tab)


def _prep_dsa_kernel(p_ref, gq_ref, gk_ref, tab_ref, qT_ref, k_ref, vT_ref):
    gq, gk = gq_ref[...], gk_ref[...]
    for h in range(N_HEADS):
        sl = slice(h * LANES, (h + 1) * LANES)
        qT = _head_norm_rope(p_ref[:, sl], gq, tab_ref, LOG2E * HEAD_DIM ** -0.5).T.astype(BF16)
        for b in range(p_ref.shape[0] // DSA_TQ):
            qT_ref[b, :, h * DSA_TQ:(h + 1) * DSA_TQ] = qT[:, b * DSA_TQ:(b + 1) * DSA_TQ]
    k_ref[...] = _head_norm_rope(p_ref[:, BRANCH_WIDTH:BRANCH_WIDTH + LANES], gk, tab_ref, 1.0).astype(BF16)
    vT_ref[...] = _vT_bf16(p_ref[:, BRANCH_WIDTH + LANES:])


def prep_dsa(p, g_q, g_k, tab):
    s = p.shape[0]
    t = ATT_T
    nqb = t // DSA_TQ
    return pl.pallas_call(
        _prep_dsa_kernel, grid=(s // t,),
        in_specs=[_row_spec(t, BRANCH_WIDTH + 2 * LANES), _const_spec(LANES), _const_spec(LANES), _tab_spec(t)],
        out_specs=[pl.BlockSpec((nqb, HEAD_DIM, N_HEADS * DSA_TQ), lambda i: (i, 0, 0)), _row_spec(t, LANES),
                   pl.BlockSpec((None, HEAD_DIM, t), lambda i: (i, 0, 0))],
        out_shape=[jax.ShapeDtypeStruct((s // DSA_TQ, HEAD_DIM, N_HEADS * DSA_TQ), BF16),
                   jax.ShapeDtypeStruct((s, LANES), BF16),
                   jax.ShapeDtypeStruct((s // t, HEAD_DIM, t), BF16)],
        compiler_params=_cparams(1), name="prep_dsa",
    )(p, g_q.reshape(1, -1), g_k.reshape(1, -1), tab)


def _prep_idx_kernel(p_ref, tab_ref, qiT_ref, ki_ref, wiT_ref):
    half = IDX_ROT // 2
    t = p_ref.shape[0]
    nqb = t // DSA_TQ
    lane = lax.broadcasted_iota(I32, (t, LANES), 1)
    first = lane < IDX_DIM
    zero = jnp.zeros((t, LANES), F32)
    for b in range(IDX_HEADS // 2):
        x = _rope(p_ref[:, b * LANES:(b + 1) * LANES], tab_ref, half) * (IDX_DIM ** -0.5)
        hi = x.astype(BF16).astype(F32)
        lo = x - hi
        rhi = pltpu.roll(hi, IDX_DIM, 1)
        for hh, (a0, a1) in enumerate(((jnp.where(first, hi, pltpu.roll(lo, IDX_DIM, 1)), jnp.where(first, hi, zero)),
                                       (jnp.where(first, rhi, lo), jnp.where(first, rhi, zero)))):
            h = 2 * b + hh
            a0T, a1T = a0.T.astype(BF16), a1.T.astype(BF16)
            for qb in range(nqb):
                qs = slice(qb * DSA_TQ, (qb + 1) * DSA_TQ)
                qiT_ref[qb, :LANES, h * DSA_TQ:(h + 1) * DSA_TQ] = a0T[:, qs]
                qiT_ref[qb, LANES:, h * DSA_TQ:(h + 1) * DSA_TQ] = a1T[:, qs]
    kx = _rope(p_ref[:, IDX_HEADS * IDX_DIM:IDX_HEADS * IDX_DIM + LANES], tab_ref, half)
    hi = kx.astype(BF16).astype(F32)
    lo = kx - hi
    ki_ref[:, :LANES] = jnp.where(first, hi, pltpu.roll(hi, IDX_DIM, 1)).astype(BF16)
    ki_ref[:, LANES:] = jnp.where(first, lo, zero).astype(BF16)
    wT = (p_ref[:, IDX_HEADS * IDX_DIM + LANES:] * (IDX_HEADS ** -0.5)).T
    for qb in range(nqb):
        wiT_ref[qb] = wT[:IDX_HEADS, qb * DSA_TQ:(qb + 1) * DSA_TQ]


def prep_idx(p, tab):
    s = p.shape[0]
    t = ATT_T
    nqb = t // DSA_TQ
    return pl.pallas_call(
        _prep_idx_kernel, grid=(s // t,),
        in_specs=[_row_spec(t, IDX_HEADS * IDX_DIM + 2 * LANES), _tab_spec(t)],
        out_specs=[pl.BlockSpec((nqb, 2 * LANES, IDX_HEADS * DSA_TQ), lambda i: (i, 0, 0)),
                   _row_spec(t, 2 * LANES),
                   pl.BlockSpec((nqb, IDX_HEADS, DSA_TQ), lambda i: (i, 0, 0))],
        out_shape=[jax.ShapeDtypeStruct((s // DSA_TQ, 2 * LANES, IDX_HEADS * DSA_TQ), BF16),
                   jax.ShapeDtypeStruct((s, 2 * LANES), BF16),
                   jax.ShapeDtypeStruct((s // DSA_TQ, IDX_HEADS, DSA_TQ), F32)],
        compiler_params=_cparams(1), name="prep_idx",
    )(p, tab)


def _softmax_step(sT, vT, m_sc, acc_sc):
    l_sc, o_sc = acc_sc
    m_prev = m_sc[...]
    m_new = jnp.maximum(m_prev, jnp.max(sT, axis=0, keepdims=True))
    alpha = jnp.exp2(m_prev - m_new)
    p = jnp.exp2(sT - m_new)
    l_sc[...] = alpha * l_sc[...] + jnp.sum(p, axis=0, keepdims=True)
    o_sc[...] = alpha * o_sc[...] + jnp.dot(vT, p.astype(BF16), preferred_element_type=F32)
    m_sc[...] = m_new


def _softmax_init(m_sc, acc_sc):
    m_sc[...] = jnp.full(m_sc.shape, NEG, F32)
    for ref in acc_sc:
        ref[...] = jnp.zeros(ref.shape, F32)


def _softmax_result(acc_sc):
    l_sc, o_sc = acc_sc
    return o_sc[...] / l_sc[...]


def _pipelined_blocks(n_full, scores, consume, consume_last, s_sc):
    s_sc[0] = scores(0)

    def pair(jj, c):
        s_sc[1] = scores(2 * jj + 1)
        consume(s_sc[0], 2 * jj)
        s_sc[0] = scores(2 * jj + 2)
        consume(s_sc[1], 2 * jj + 1)
        return c

    lax.fori_loop(0, n_full // 2, pair, 0)

    @pl.when(n_full % 2 == 1)
    def _():
        s_sc[1] = scores(n_full)
        consume(s_sc[0], n_full - 1)
        consume_last(s_sc[1], n_full)

    @pl.when(n_full % 2 == 0)
    def _():
        consume_last(s_sc[0], n_full)


def _key_le_query(tk, tq):
    return lax.broadcasted_iota(I32, (tk, tq), 0) <= lax.broadcasted_iota(I32, (tk, tq), 1)


def _head_group_attention(qT_ref, k_ref, vT_ref, o_ref, s_sc, m_sc, l_sc, o_sc, *,
                          n_full, block_of, bias_of, causal_last):
    g, _, t = qT_ref.shape
    w = k_ref.shape[2] // g
    accs = [(l_sc.at[h], o_sc.at[h]) for h in range(g)]
    for h in range(g):
        _softmax_init(m_sc.at[h], accs[h])

    def scores(d):
        kb = k_ref[block_of(d)]
        return jnp.stack([jnp.dot(kb[:, h * w:(h + 1) * w], qT_ref[h], preferred_element_type=F32)
                          for h in range(g)])

    def step(sT, d, causal=False):
        bias = None if bias_of is None else bias_of(d)
        for h in range(g):
            s_h = sT[h] if bias is None else sT[h] + bias
            if causal:
                s_h = jnp.where(_key_le_query(t, t), s_h, NEG)
            _softmax_step(s_h, vT_ref[h, block_of(d)], m_sc.at[h], accs[h])

    _pipelined_blocks(n_full, scores, step, functools.partial(step, causal=causal_last), s_sc)
    for h in range(g):
        o_ref[:, h * HEAD_DIM:(h + 1) * HEAD_DIM] = _softmax_result(accs[h]).T.astype(o_ref.dtype)


def _head_group_call(kernel, name, g, qT, k, vT, extra=()):
    n_heads, dq, s = qT.shape
    t = ATT_T
    nb = s // t
    once = pl.Buffered(1)
    whole = lambda a: pl.BlockSpec(a.shape, lambda h, i: (0,) * a.ndim, pipeline_mode=once)
    return pl.pallas_call(
        kernel, grid=(n_heads // g, nb),
        in_specs=[pl.BlockSpec((g, dq, t), lambda h, i: (h, 0, i)),
                  pl.BlockSpec((nb, t, g * dq), lambda h, i: (0, 0, h), pipeline_mode=once),
                  pl.BlockSpec((g, nb, HEAD_DIM, t), lambda h, i: (h, 0, 0, 0), pipeline_mode=once)]
        + [whole(a) for a in extra],
        out_specs=pl.BlockSpec((t, g * HEAD_DIM), lambda h, i: (i, h)),
        out_shape=jax.ShapeDtypeStruct((s, n_heads * HEAD_DIM), BF16),
        scratch_shapes=[pltpu.VMEM((2, g, t, t), F32), pltpu.VMEM((g, 1, t), F32), pltpu.VMEM((g, 1, t), F32),
                        pltpu.VMEM((g, HEAD_DIM, t), F32)],
        compiler_params=_cparams(2), name=name,
    )(qT, k.reshape(nb, t, -1), vT, *extra)


def _mla_kernel(qT_ref, k_ref, vT_ref, o_ref, *scratch):
    _head_group_attention(qT_ref, k_ref, vT_ref, o_ref, *scratch, n_full=pl.program_id(1),
                          block_of=lambda j: j, bias_of=None, causal_last=True)


def _attn_scratch(tk, tq):
    return [pltpu.VMEM((2, tk, tq), F32), pltpu.VMEM((1, tq), F32), pltpu.VMEM((1, tq), F32),
            pltpu.VMEM((HEAD_DIM, tq), F32)]


def _head_qT_spec(d, t):
    return pl.BlockSpec((None, d, t), lambda h, i: (h, 0, i))


def _head_k_spec(nb, t, w):
    return pl.BlockSpec((nb, t, w), lambda h, i: (0, 0, h))


def _head_vT_spec(nb, t):
    return pl.BlockSpec((None, nb, HEAD_DIM, t), lambda h, i: (h, 0, 0, 0))


def _head_out_spec(t):
    return pl.BlockSpec((t, HEAD_DIM), lambda h, i: (i, h))


def mla_attention(qT, k, vT):
    return _head_group_call(_mla_kernel, "mla_attention", MLA_HEAD_GROUP, qT, k, vT)


SB_SUB = 128


def _sb_kernel(qT_ref, k_ref, vT_ref, u_ref, kn_ref, o_ref, carry_sc, acc_sc):
    i = pl.program_id(1)
    t = qT_ref.shape[1]
    qT = qT_ref[...]
    u2 = u_ref[...]
    qf = qT.astype(F32)
    zbound = jnp.sqrt(jnp.sum(qf * qf, axis=0, keepdims=True) * jnp.max(kn_ref[...]))
    carry_sc[...] = jnp.zeros(carry_sc.shape, F32)
    acc_sc[...] = jnp.zeros(acc_sc.shape, F32)
    key = lax.broadcasted_iota(I32, (SB_SUB, t), 0)
    qry = lax.broadcasted_iota(I32, (SB_SUB, t), 1)

    def block(j, diag):
        zT = jnp.dot(k_ref[j], qT, preferred_element_type=F32)
        carry = carry_sc[...]
        parts = [None] * (t // SB_SUB)
        for c in reversed(range(t // SB_SUB)):
            first = c * SB_SUB if diag else 0
            zc = zT[c * SB_SUB:(c + 1) * SB_SUB, first:]
            ls = jnp.minimum(-zc, 0.0) - jnp.log(1.0 + jnp.exp(-jnp.abs(zc)))
            if diag:
                past = (key + c * SB_SUB < qry)[:, first:]
                ls = jnp.where(past, ls, 0.0)
            hi = ls.astype(BF16)
            lo = (ls - hi.astype(F32)).astype(BF16)
            rev = jnp.dot(u2, jnp.concatenate([hi, lo], axis=0), preferred_element_type=F32) + carry[:, first:]
            a = jnp.exp(jnp.minimum(zc + rev, 0.0))
            if diag:
                a = jnp.where(past, a, 0.0)
            a = a.astype(BF16)
            parts[c] = a if first == 0 else jnp.concatenate([jnp.zeros((SB_SUB, first), BF16), a], axis=1)
            carry = rev[0:1, :] if first == 0 else jnp.concatenate([carry[:, :first], rev[0:1, :]], axis=1)
        carry_sc[...] = carry
        acc_sc[...] += jnp.dot(vT_ref[j], jnp.concatenate(parts, axis=0), preferred_element_type=F32)

    def all_underflow():
        return (jnp.max(carry_sc[...] + zbound) < SB_EXIT).astype(I32)

    block(i, True)

    def back(state):
        jj, _ = state
        block(i - 1 - jj, False)
        return jj + 1, all_underflow()

    lax.while_loop(lambda st: jnp.logical_and(st[0] < i, st[1] == 0), back, (jnp.int32(0), all_underflow()))
    o_ref[...] = acc_sc[...].T.astype(o_ref.dtype)


def sb_attention(qT, k, vT, kn):
    s = k.shape[0]
    t = SB_T
    nb = s // t
    tri = (np.arange(SB_SUB)[None, :] >= np.arange(SB_SUB)[:, None]).astype(np.float32)
    u2 = jnp.asarray(np.concatenate([tri, tri], axis=1), dtype=BF16)
    return pl.pallas_call(
        _sb_kernel, grid=(N_HEADS, nb),
        in_specs=[_head_qT_spec(HEAD_DIM, t), _head_k_spec(nb, t, HEAD_DIM), _head_vT_spec(nb, t),
                  pl.BlockSpec((SB_SUB, 2 * SB_SUB), lambda h, i: (0, 0)),
                  pl.BlockSpec((None, nb, 8, LANES), lambda h, i: (h, 0, 0, 0))],
        out_specs=_head_out_spec(t),
        out_shape=jax.ShapeDtypeStruct((s, BRANCH_WIDTH), BF16),
        scratch_shapes=[pltpu.VMEM((1, t), F32), pltpu.VMEM((HEAD_DIM, t), F32)],
        compiler_params=_cparams(2), name="sb_attention",
    )(qT, k.reshape(nb, t, -1), vT, u2, kn)


def _dil_log_weights(t):
    span = max(w for w, _ in DIL_PATTERNS)
    nback = -(-span // t)
    d = np.arange(nback + 1)[:, None, None] * t + np.arange(t)[None, None, :] - np.arange(t)[None, :, None]
    mult = np.zeros(d.shape, np.float64)
    for window, dil in DIL_PATTERNS:
        mult += ((d >= 0) & (d <= window) & (d % dil == 0))
    return np.where(mult > 0, np.log2(np.maximum(mult, 1.0)), NEG).astype(np.float32)


def _dil_kernel(qT_ref, k_ref, vT_ref, w_ref, o_ref, *scratch):
    i = pl.program_id(1)
    nback = w_ref.shape[0] - 1
    _head_group_attention(qT_ref, k_ref, vT_ref, o_ref, *scratch, n_full=jnp.minimum(i, nback),
                          block_of=lambda d: i - d, bias_of=lambda d: w_ref[d], causal_last=False)


def dil_attention(qT, k, vT):
    return _head_group_call(_dil_kernel, "dil_attention", DIL_HEAD_GROUP, qT, k, vT,
                            extra=(jnp.asarray(_dil_log_weights(ATT_T)),))


def _dsa_kernel(qsT_ref, qiT_ref, wiT_ref, ki_ref, k_ref, vT_ref, o_ref,
                key_sc, q1_sc, s_sc, m_sc, l_sc, o_sc, *, top_k, pos_bits):
    acc_sc = (l_sc, o_sc)
    i = pl.program_id(0)
    tq, tk = DSA_TQ, ATT_T
    nkb = (i * tq + tq + tk - 1) // tk
    qiT = qiT_ref[...]
    wiT = wiT_ref[...]
    row = lax.broadcasted_iota(I32, (tk, tq), 1) + i * tq
    col = lax.broadcasted_iota(I32, (tk, tq), 0)

    def idx_block(aT, j):
        idx = jnp.zeros((tk, tq), F32)
        for h in range(IDX_HEADS):
            idx = idx + jnp.maximum(aT[:, h * tq:(h + 1) * tq], 0.0) * wiT[h:h + 1, :]
        bits = pltpu.bitcast(idx + 0.0, I32)
        key = bits ^ ((bits >> 31) & 0x7FFFFFFF)
        key_sc[j] = jnp.where(col + j * tk <= row, key, INT_MIN)

    _pipelined_blocks(nkb - 1, lambda j: jnp.dot(ki_ref[j], qiT, preferred_element_type=F32),
                      idx_block, idx_block, s_sc)

    @pl.when(nkb % 2 == 1)
    def _():
        key_sc[nkb] = jnp.full((tk, tq), INT_MIN, I32)

    def count(hits):
        def part(j):
            return jnp.sum(hits(key_sc[j], j).reshape(tk // COUNT_ROWS, COUNT_ROWS, tq), axis=0)

        def body(jj, acc):
            return acc + part(2 * jj) + part(2 * jj + 1)

        acc = lax.fori_loop(0, (nkb + 1) // 2, body, jnp.zeros((COUNT_ROWS, tq), F32))
        return jnp.sum(acc, axis=0, keepdims=True)

    def bit_body(b, carry):
        thr, cge = carry
        cand = thr + jnp.left_shift(jnp.int32(1), 31 - b)
        c = count(lambda kb, j: jnp.where(kb >= cand, 1.0, 0.0))
        ok = c >= top_k
        return jnp.where(ok, cand, thr), jnp.where(ok, c, cge)

    thr0 = jnp.full((1, tq), INT_MIN, I32)
    cge0 = jnp.zeros((1, tq), F32) + (nkb * tk).astype(F32)
    thr, cge = lax.fori_loop(0, 32, bit_body, (thr0, cge0))

    q1_sc[...] = jnp.full((1, tq), 2 ** 30, I32)

    @pl.when(jnp.max(cge) > top_k)
    def _():
        need = top_k - count(lambda kb, j: jnp.where(kb > thr, 1.0, 0.0))

        def pos_body(b, qpos):
            cand = qpos + jnp.left_shift(jnp.int32(1), pos_bits - 1 - b)
            g = count(lambda kb, j: jnp.where(kb == thr, jnp.where(col + j * tk < cand, 1.0, 0.0), 0.0))
            return jnp.where(g < need, cand, qpos)

        q1_sc[...] = lax.fori_loop(0, pos_bits, pos_body, jnp.zeros((1, tq), I32)) + 1

    q1 = q1_sc[...]
    qsT = qsT_ref[...]
    _softmax_init(m_sc, acc_sc)
    def att_block(sT, j):
        kb = key_sc[j]
        pos = col + j * tk
        bias = jnp.where(kb > thr, 0.0, jnp.where(kb == thr, jnp.where(pos < q1, 0.0, NEG), NEG))
        bias = jnp.where(pos <= row, bias, NEG)
        _softmax_step(sT + jnp.concatenate([bias] * N_HEADS, axis=1), vT_ref[j], m_sc, acc_sc)

    _pipelined_blocks(nkb - 1, lambda j: jnp.dot(k_ref[j], qsT, preferred_element_type=F32),
                      att_block, att_block, s_sc)
    o = _softmax_result(acc_sc)
    for h in range(N_HEADS):
        sl = slice(h * HEAD_DIM, (h + 1) * HEAD_DIM)
        o_ref[:, sl] = o[:, h * tq:(h + 1) * tq].T.astype(o_ref.dtype)


def dsa_attention(qsT, qiT, wiT, ki, k, vT):
    s = k.shape[0]
    tq, tk = DSA_TQ, ATT_T
    nkb = s // tk
    top_k = min(DSA_TOPK, s // 4)
    wq = N_HEADS * tq
    full3 = lambda a, b: pl.BlockSpec((nkb, a, b), lambda i: (0, 0, 0))
    return pl.pallas_call(
        functools.partial(_dsa_kernel, top_k=top_k, pos_bits=(s - 1).bit_length()),
        grid=(s // tq,),
        in_specs=[pl.BlockSpec((None, HEAD_DIM, wq), lambda i: (i, 0, 0)),
                  pl.BlockSpec((None, 2 * LANES, wq), lambda i: (i, 0, 0)),
                  pl.BlockSpec((None, IDX_HEADS, tq), lambda i: (i, 0, 0)),
                  full3(tk, 2 * LANES), full3(tk, HEAD_DIM), full3(HEAD_DIM, tk)],
        out_specs=pl.BlockSpec((tq, BRANCH_WIDTH), lambda i: (i, 0)),
        out_shape=jax.ShapeDtypeStruct((s, BRANCH_WIDTH), BF16),
        scratch_shapes=[pltpu.VMEM((nkb + nkb % 2, tk, tq), I32), pltpu.VMEM((1, tq), I32)]
        + _attn_scratch(tk, wq),
        compiler_params=_cparams(1), name="dsa_attention",
    )(qsT, qiT, wiT, ki.reshape(nkb, tk, -1), k.reshape(nkb, tk, -1), vT)


def _merge_kernel(g_ref, o0_ref, o1_ref, o2_ref, o3_ref, wgb_ref, b_ref, wbr_ref, out_ref):
    g = g_ref[...]
    acc = None
    for n, o_ref in enumerate((o0_ref, o1_ref, o2_ref, o3_ref)):
        gate = jax.nn.sigmoid(jnp.dot(g, wgb_ref[n], preferred_element_type=F32) + b_ref[n])
        val = gate * jnp.dot(o_ref[...], wbr_ref[n], preferred_element_type=F32)
        acc = val if acc is None else acc + val
    out_ref[...] = acc.astype(out_ref.dtype)


def gated_merge(g_lat, outs, w_gate_b, b_gate, w_branch):
    s = g_lat.shape[0]
    d = w_gate_b.shape[-1]
    tm = _pick(s, (1024, 512, 256, 128))
    tn = _pick(d, (512, 256, 128))
    o_spec = pl.BlockSpec((tm, BRANCH_WIDTH), lambda i, j: (i, 0))
    return pl.pallas_call(
        _merge_kernel, grid=(s // tm, d // tn),
        in_specs=[pl.BlockSpec((tm, GATE_RANK), lambda i, j: (i, 0)), o_spec, o_spec, o_spec, o_spec,
                  pl.BlockSpec((N_BRANCHES, GATE_RANK, tn), lambda i, j: (0, 0, j)),
                  pl.BlockSpec((N_BRANCHES, 1, tn), lambda i, j: (0, 0, j)),
                  pl.BlockSpec((N_BRANCHES, BRANCH_WIDTH, tn), lambda i, j: (0, 0, j))],
        out_specs=pl.BlockSpec((tm, tn), lambda i, j: (i, j)),
        out_shape=jax.ShapeDtypeStruct((s, d), BF16),
        compiler_params=_cparams(2), name="gated_merge",
    )(g_lat, *outs, w_gate_b, b_gate.reshape(N_BRANCHES, 1, d), w_branch)


def _cross_kernel(x_ref, g_ref, wq_ref, gq_ref, kraw_ref, gk_ref, v_ref, wo_ref, gn_ref, o_ref, hn_ref):
    x = x_ref[...]
    ms = jnp.mean(x * x, axis=-1, keepdims=True)
    h = (x * lax.rsqrt(ms + RMS_EPS) * g_ref[...]).astype(BF16)
    q = jnp.dot(h, wq_ref[...], preferred_element_type=F32)
    outs = []
    for hd in range(MEM_HEADS):
        sl = slice(hd * MEM_HEAD_DIM, (hd + 1) * MEM_HEAD_DIM)
        qh = q[:, sl]
        qh = qh * (lax.rsqrt(jnp.mean(qh * qh, axis=-1, keepdims=True) + RMS_EPS) * MEM_HEAD_DIM ** -0.5) * gq_ref[...]
        kh = kraw_ref[:, sl]
        kh = kh * lax.rsqrt(jnp.mean(kh * kh, axis=-1, keepdims=True) + RMS_EPS) * gk_ref[...]
        s = lax.dot_general(qh.astype(BF16), kh.astype(BF16), NT_DIMS, preferred_element_type=F32)
        p = jnp.exp(s - jnp.max(s, axis=1, keepdims=True))
        p = p / jnp.sum(p, axis=1, keepdims=True)
        outs.append(jnp.dot(p.astype(BF16), v_ref[:, sl], preferred_element_type=F32).astype(BF16))
    o = jnp.concatenate(outs, axis=1)
    y = x + jnp.dot(o, wo_ref[...], preferred_element_type=F32)
    o_ref[...] = y
    ms = jnp.mean(y * y, axis=-1, keepdims=True)
    hn_ref[...] = (y * lax.rsqrt(ms + RMS_EPS) * gn_ref[...]).astype(hn_ref.dtype)


def cross_attention(x, g, w_xq, g_q, k_raw, g_k, v, w_xo, g_next):
    s, d = x.shape
    m = k_raw.shape[0]
    tm = _pick(s, (256, 128))
    c2 = lambda shape: pl.BlockSpec(shape, lambda i: (0, 0))
    return pl.pallas_call(
        _cross_kernel, grid=(s // tm,),
        in_specs=[_row_spec(tm, d), c2((1, d)), c2((d, MEM_WIDTH)), c2((1, MEM_HEAD_DIM)),
                  c2((m, MEM_WIDTH)), c2((1, MEM_HEAD_DIM)), c2((m, MEM_WIDTH)), c2((MEM_WIDTH, d)), c2((1, d))],
        out_specs=[_row_spec(tm, d), _row_spec(tm, d)],
        out_shape=[jax.ShapeDtypeStruct((s, d), F32), jax.ShapeDtypeStruct((s, d), BF16)],
        compiler_params=_cparams(1), name="cross_attention",
    )(x, g.reshape(1, d), w_xq, g_q.reshape(1, -1), k_raw, g_k.reshape(1, -1), v, w_xo,
      g_next.reshape(1, d).astype(F32))


def _split_w_in(w_in):
    cuts = np.cumsum([0, MLA_Q_LORA, MLA_KV_LORA, MLA_ROPE, BRANCH_WIDTH, BRANCH_WIDTH, BRANCH_WIDTH,
                      BRANCH_WIDTH, BRANCH_WIDTH, BRANCH_WIDTH, BRANCH_WIDTH, HEAD_DIM, HEAD_DIM,
                      IDX_HEADS * IDX_DIM, IDX_DIM, IDX_HEADS])
    seg = lambda a, b: w_in[:, cuts[a]:cuts[b]]
    zeros = lambda n: jnp.zeros((w_in.shape[0], n), w_in.dtype)
    w_lat = jnp.concatenate([seg(0, 3), zeros(LANES - MLA_ROPE)], axis=1)
    w_sb = seg(3, 6)
    w_dil = seg(6, 9)
    w_dsa = seg(9, 12)
    w_idx = jnp.concatenate([seg(12, 14), zeros(LANES - IDX_DIM), seg(14, 15), zeros(LANES - IDX_HEADS)], axis=1)
    return [w.astype(BF16) for w in (w_lat, w_sb, w_dil, w_dsa, w_idx)]


def _pad_mla_up(w_uq, w_ukv):
    r = w_uq.shape[0]
    wq = w_uq.reshape(r, N_HEADS, MLA_QK)
    wq = jnp.concatenate([wq, jnp.zeros((r, N_HEADS, MLA_PAD - MLA_QK), wq.dtype)], axis=2)
    wkv = w_ukv.reshape(w_ukv.shape[0], N_HEADS, 2 * HEAD_DIM)
    wkv = jnp.concatenate([wkv[:, :, :MLA_NOPE].reshape(-1, BRANCH_WIDTH),
                           wkv[:, :, MLA_NOPE:].reshape(-1, BRANCH_WIDTH)], axis=1)
    return wq.reshape(r, N_HEADS * MLA_PAD).astype(BF16), wkv.astype(BF16)


def _branches(h, p, tabs):
    tab_p, tab_i, tab_m = tabs
    w_lat, w_sb, w_dil, w_dsa, w_idx = _split_w_in(p['w_in'])
    w_uq, w_ukv = _pad_mla_up(p['w_uq'], p['w_ukv'])

    cq, ckv, kpe = prep_latent(matmul(h, w_lat, out_dtype=F32), p['g_cq'], p['g_ckv'])
    qT, k, vT = prep_mla(matmul(cq, w_uq, out_dtype=F32), matmul(ckv, w_ukv, out_dtype=F32), kpe,
                         p['g_q_mla'], p['g_k_mla'], tab_m)
    o_mla = mla_attention(qT, k, vT)
    o_sb = sb_attention(*prep_sb(matmul(h, w_sb, out_dtype=F32)))
    o_dil = dil_attention(*prep_dil(matmul(h, w_dil, out_dtype=F32), p['g_q_dil'], p['g_k_dil'], tab_p))
    qsT, ks, vsT = prep_dsa(matmul(h, w_dsa, out_dtype=F32), p['g_q_dsa'], p['g_k_dsa'], tab_p)
    qiT, ki, wiT = prep_idx(matmul(h, w_idx, out_dtype=F32), tab_i)
    o_dsa = dsa_attention(qsT, qiT, wiT, ki, ks, vsT)
    return o_mla, o_sb, o_dil, o_dsa


def _token_mixer(x, h, p, tabs):
    o_mla, o_sb, o_dil, o_dsa = _branches(h, p, tabs)
    g_lat = matmul(h, p['w_gate_a'].astype(BF16), out_dtype=BF16)
    merged = gated_merge(g_lat, (o_mla, o_sb, o_dil, o_dsa), p['w_gate_b'].astype(BF16),
                         p['b_gate'].astype(F32), p['w_branch'].astype(BF16))
    return matmul(merged, p['w_out'].astype(BF16), out_dtype=F32, res=x)


def _cross_block(x, mem, p, g_next):
    m_n = rmsnorm_rows(mem, p['ln_mem'])
    k_raw = matmul(m_n, p['w_xk'].astype(BF16), out_dtype=F32)
    v = matmul(m_n, p['w_xv'].astype(BF16), out_dtype=BF16)
    return cross_attention(x, p['ln_xattn'], p['w_xq'].astype(BF16), p['g_q_x'], k_raw, p['g_k_x'], v,
                           p['w_xo'].astype(BF16), g_next)


def _ffn_block(x, h, wg, wu, wd):
    act = glu_up(h, wg.astype(BF16), wu.astype(BF16))
    f = wd.shape[0]
    tk = max(c for c in range(256, 2049, 256) if f % c == 0)
    return matmul(act, wd.astype(BF16), out_dtype=F32, res=x, tn=1024, tk=tk)


def _moe_block(x, g, w_router, wg, wu, wd):
    sel = router_top2(x, g, w_router)
    src, pos, tile_expert = _moe_plan(sel)
    hs = gather_norm(x, g, src)
    act = _grouped_call(_glu_grouped_kernel, "moe_glu_up", hs, [wg.astype(BF16), wu.astype(BF16)],
                        tile_expert, _pick(wg.shape[2], (256, 128)), BF16)
    y = _grouped_call(_mm_grouped_kernel, "moe_down", act, [wd.astype(BF16)], tile_expert,
                      _pick(wd.shape[2], (1024, 512, 256, 128)), F32)
    return moe_combine(x, sel, y, pos)


def kernel(x, mem, ln_mix, w_in, g_cq, g_ckv, w_uq, w_ukv, g_q_mla, g_k_mla, g_q_dil, g_k_dil, g_q_dsa, g_k_dsa, w_gate_a, w_gate_b, b_gate, w_branch, w_out, ln_xattn, ln_mem, w_xq, w_xk, w_xv, g_q_x, g_k_x, w_xo, ln_ffn, w_ff_gate, w_ff_up, w_ff_down, w_router, w_e_gate, w_e_up, w_e_down):
    b, s, d = x.shape
    per_layer = dict(w_in=w_in, g_cq=g_cq, g_ckv=g_ckv, w_uq=w_uq, w_ukv=w_ukv, g_q_mla=g_q_mla,
                     g_k_mla=g_k_mla, g_q_dil=g_q_dil, g_k_dil=g_k_dil, g_q_dsa=g_q_dsa, g_k_dsa=g_k_dsa,
                     w_gate_a=w_gate_a, w_gate_b=w_gate_b, b_gate=b_gate, w_branch=w_branch, w_out=w_out,
                     ln_xattn=ln_xattn, ln_mem=ln_mem, w_xq=w_xq, w_xk=w_xk, w_xv=w_xv, g_q_x=g_q_x,
                     g_k_x=g_k_x, w_xo=w_xo)
    tabs = (_rope_tables(s, ROT_DIM, HEAD_DIM), _rope_tables(s, IDX_ROT, IDX_DIM),
            _rope_tables(s, MLA_ROPE, HEAD_DIM))
    depth = ln_mix.shape[0]
    outs = []
    for bi in range(b):
        xb, mb = x[bi], mem[bi]
        for i in range(depth):
            p = {name: val[i] for name, val in per_layer.items()}
            xb = _token_mixer(xb, rmsnorm_rows(xb, ln_mix[i]), p, tabs)
            xb, h_ffn = _cross_block(xb, mb, p, ln_ffn[i])
            j = i // 2
            if i % 2 == 0:
                xb = _ffn_block(xb, h_ffn, w_ff_gate[j], w_ff_up[j], w_ff_down[j])
            else:
                xb = _moe_block(xb, ln_ffn[i], w_router[j], w_e_gate[j], w_e_up[j], w_e_down[j])
        outs.append(xb)
    return outs[0][None] if b == 1 else jnp.stack(outs, axis=0)
```
